```python
import math
import jax
import jax.numpy as jnp
from jax import lax
import numpy as np

D_MODEL = 1024
BATCH = 4
SEQ = 8192
DEPTH = 2

CHUNK = 64

SSD_HEAD_DIM = 64
SSD_WIDTH = D_MODEL // 2
SSD_HEADS = SSD_WIDTH // SSD_HEAD_DIM
SSD_GROUPS = 2
SSD_HPG = SSD_HEADS // SSD_GROUPS
SSD_STATE = 64
SSD_CONV = 4
SSD_XBC = SSD_WIDTH + 2 * SSD_GROUPS * SSD_STATE

RWKV_HEAD_DIM = 64
RWKV_WIDTH = D_MODEL // 4
RWKV_HEADS = RWKV_WIDTH // RWKV_HEAD_DIM
RWKV_DECAY_RANK = 32
RWKV_ICLR_RANK = 32
RWKV_GATE_RANK = 64
RWKV_SPLITS = [RWKV_WIDTH, RWKV_WIDTH, RWKV_WIDTH, RWKV_DECAY_RANK, RWKV_ICLR_RANK, RWKV_GATE_RANK]
RWKV_IN = sum(RWKV_SPLITS)
RWKV_GN_EPS = 64e-5

GLA_VALUE_DIM = 64
GLA_WIDTH = D_MODEL // 4
GLA_HEADS = GLA_WIDTH // GLA_VALUE_DIM
GLA_KEY_DIM = GLA_VALUE_DIM // 2
GLA_QK_WIDTH = GLA_HEADS * GLA_KEY_DIM
GLA_GATE_RANK = 16
GLA_GATE_TEMP = 16.0

MIX_WIDTH = SSD_WIDTH + RWKV_WIDTH + GLA_WIDTH
IN_SPLITS = [SSD_WIDTH, SSD_XBC, SSD_HEADS, RWKV_IN, GLA_QK_WIDTH, GLA_QK_WIDTH, GLA_WIDTH, GLA_WIDTH, GLA_GATE_RANK]
N_IN = sum(IN_SPLITS)

N_GROUPS = 4
EXPERTS_PER_GROUP = 8
N_EXPERTS = N_GROUPS * EXPERTS_PER_GROUP
TOP_K = 2
EXPERT_FF = 512
ROW_BLOCK = 256

DEEPNORM_ALPHA = (2 * DEPTH) ** 0.25
DEEPNORM_BETA = (8 * DEPTH) ** -0.25
LN_EPS = 1e-5
RMS_EPS = 1e-6

kernel_name = 'hybrid_ssd_rwkv7_gla_hmoe'

F32 = jnp.float32


def _split(t, sizes):
    idx = [int(i) for i in np.cumsum(sizes)[:-1]]
    return jnp.split(t, idx, axis=-1)


def layer_norm(x, g, b):
    xf = x.astype(F32)
    mu = jnp.mean(xf, -1, keepdims=True)
    var = jnp.mean(jnp.square(xf - mu), -1, keepdims=True)
    return ((xf - mu) * lax.rsqrt(var + LN_EPS) * g + b).astype(x.dtype)


def causal_depthwise_conv(x, w, b):
    k, c = w.shape
    y = lax.conv_general_dilated(x, w[:, None, :], window_strides=(1,), padding=[(k - 1, 0)],
                                 dimension_numbers=('NWC', 'WIO', 'NWC'), feature_group_count=c)
    return y + b


def ssd_chunked_scan(x, a_dt, bm, cm):
    bsz, seq = x.shape[0], x.shape[1]
    nc = seq // CHUNK
    x = x.reshape(bsz, nc, CHUNK, *x.shape[2:])
    a_dt = a_dt.reshape(bsz, nc, CHUNK, *a_dt.shape[2:])
    bm = bm.reshape(bsz, nc, CHUNK, *bm.shape[2:])
    cm = cm.reshape(bsz, nc, CHUNK, *cm.shape[2:])
    a_cs = jnp.cumsum(a_dt, axis=2)
    mask = jnp.tril(jnp.ones((CHUNK, CHUNK), bool))[:, :, None, None]
    seg = jnp.exp(jnp.where(mask, a_cs[:, :, :, None] - a_cs[:, :, None, :], -jnp.inf))
    cb = jnp.einsum('bclgn,bcsgn->bclsg', cm, bm)
    y_diag = jnp.einsum('bclsgh,bcsghp->bclghp', cb[..., None] * seg, x)
    to_end = jnp.exp(a_cs[:, :, -1:] - a_cs)
    states = jnp.einsum('bclgn,bclghp->bcghpn', bm, to_end[..., None] * x)
    chunk_decay = jnp.exp(a_cs[:, :, -1])

    def step(s, inp):
        st, dec = inp
        return s * dec[..., None, None] + st, s

    s0 = jnp.zeros_like(states[:, 0])
    _, prev = lax.scan(step, s0, (jnp.moveaxis(states, 1, 0), jnp.moveaxis(chunk_decay, 1, 0)))
    prev = jnp.moveaxis(prev, 0, 1)
    y_off = jnp.einsum('bclgn,bcghpn->bclghp', cm, prev) * jnp.exp(a_cs)[..., None]
    return (y_diag + y_off).reshape(bsz, seq, *x.shape[3:])


def ssd_mixer(z, xbc, dt_raw, conv_w, conv_b, dt_bias, a_log, d_skip, norm_g):
    bsz, seq, _ = xbc.shape
    xbc = jax.nn.silu(causal_depthwise_conv(xbc, conv_w, conv_b))
    xh, bm, cm = _split(xbc, [SSD_WIDTH, SSD_GROUPS * SSD_STATE, SSD_GROUPS * SSD_STATE])
    xh = xh.astype(F32).reshape(bsz, seq, SSD_GROUPS, SSD_HPG, SSD_HEAD_DIM)
    bm = bm.astype(F32).reshape(bsz, seq, SSD_GROUPS, SSD_STATE)
    cm = cm.astype(F32).reshape(bsz, seq, SSD_GROUPS, SSD_STATE)
    dt = jax.nn.softplus((dt_raw + dt_bias).astype(F32)).reshape(bsz, seq, SSD_GROUPS, SSD_HPG)
    a = -jnp.exp(a_log.astype(F32)).reshape(SSD_GROUPS, SSD_HPG)
    y = ssd_chunked_scan(xh * dt[..., None], dt * a, bm, cm)
    y = y + d_skip.astype(F32).reshape(SSD_GROUPS, SSD_HPG)[:, :, None] * xh
    y = y.reshape(bsz, seq, SSD_WIDTH) * jax.nn.silu(z.astype(F32))
    y = y.reshape(bsz, seq, SSD_GROUPS, SSD_WIDTH // SSD_GROUPS)
    y = y * lax.rsqrt(jnp.mean(y * y, -1, keepdims=True) + RMS_EPS)
    return y.reshape(bsz, seq, SSD_WIDTH) * norm_g


def rwkv7_mixer(p, mu, w0, w2, a0, a2, g2, k_k, k_a, r_k, ln_g, ln_b):
    bsz, seq, _ = p.shape
    prev = jnp.pad(p, ((0, 0), (1, 0), (0, 0)))[:, :seq]
    p = p + (prev - p) * mu
    r, k, v, w_lr, a_lr, g_lr = _split(p, RWKV_SPLITS)
    w = -jax.nn.softplus(-(w0 + jnp.tanh(w_lr) @ w2).astype(F32)) - 0.5
    decay = jnp.exp(-jnp.exp(w))
    a = jax.nn.sigmoid((a0 + a_lr @ a2).astype(F32))
    g = (jax.nn.sigmoid(g_lr) @ g2).astype(F32)

    def heads(t):
        return t.astype(F32).reshape(bsz, seq, RWKV_HEADS, RWKV_HEAD_DIM)

    r, k, v, decay, a = heads(r), heads(k), heads(v), heads(decay), heads(a)
    kk = k * k_k.astype(F32).reshape(RWKV_HEADS, RWKV_HEAD_DIM)
    kk = kk * lax.rsqrt(jnp.sum(kk * kk, -1, keepdims=True) + 1e-12)
    k = k * (1.0 + (a - 1.0) * k_a.astype(F32).reshape(RWKV_HEADS, RWKV_HEAD_DIM))

    def step(state, inp):
        r_t, w_t, k_t, v_t, kk_t, a_t = inp
        sa = jnp.einsum('bhvk,bhk->bhv', state, -kk_t)
        state = (state * w_t[:, :, None, :] + sa[..., None] * (kk_t * a_t)[:, :, None, :]
                 + v_t[..., None] * k_t[:, :, None, :])
        return state, jnp.einsum('bhvk,bhk->bhv', state, r_t)

    s0 = jnp.zeros((bsz, RWKV_HEADS, RWKV_HEAD_DIM, RWKV_HEAD_DIM), F32)
    tm = lambda t: jnp.moveaxis(t, 1, 0)
    _, y = lax.scan(step, s0, (tm(r), tm(decay), tm(k), tm(v), tm(kk), tm(a)))
    y = jnp.moveaxis(y, 0, 1)
    mean = jnp.mean(y, -1, keepdims=True)
    var = jnp.mean(jnp.square(y - mean), -1, keepdims=True)
    y = ((y - mean) * lax.rsqrt(var + RWKV_GN_EPS)).reshape(bsz, seq, RWKV_WIDTH) * ln_g + ln_b
    bonus = jnp.sum(r * k * r_k.astype(F32), -1, keepdims=True) * v
    return (y + bonus.reshape(bsz, seq, RWKV_WIDTH)) * g


def gla_mixer(q, k, v, g, a_lr, w_a2, b_a, norm_g):
    bsz, seq, _ = q.shape
    nc = seq // CHUNK
    log_a = jax.nn.log_sigmoid((a_lr @ w_a2 + b_a).astype(F32)) / GLA_GATE_TEMP

    def chunks(t, d):
        return jnp.moveaxis(t.astype(F32).reshape(bsz, nc, CHUNK, GLA_HEADS, d), 1, 0)

    qc = chunks(q, GLA_KEY_DIM) * (GLA_KEY_DIM ** -0.5)
    kc = chunks(k, GLA_KEY_DIM)
    vc = chunks(v, GLA_VALUE_DIM)
    lac = chunks(log_a, GLA_KEY_DIM)
    mask = jnp.tril(jnp.ones((CHUNK, CHUNK), bool))[None, :, :, None, None]

    def step(state, inp):
        q_c, k_c, v_c, la_c = inp
        cum = jnp.cumsum(la_c, axis=1)
        o_inter = jnp.einsum('bihd,bhde->bihe', q_c * jnp.exp(cum), state)
        rel = jnp.exp(jnp.where(mask, cum[:, :, None] - cum[:, None, :], -jnp.inf))
        scores = jnp.einsum('bijhd,bjhd->bhij', q_c[:, :, None] * rel, k_c)
        o_intra = jnp.einsum('bhij,bjhe->bihe', scores, v_c)
        last = cum[:, -1]
        k_dec = k_c * jnp.exp(last[:, None] - cum)
        state = state * jnp.exp(last)[..., None] + jnp.einsum('bjhd,bjhe->bhde', k_dec, v_c)
        return state, o_inter + o_intra

    s0 = jnp.zeros((bsz, GLA_HEADS, GLA_KEY_DIM, GLA_VALUE_DIM), F32)
    _, o = lax.scan(step, s0, (qc, kc, vc, lac))
    o = jnp.moveaxis(o, 0, 1).reshape(bsz, seq, GLA_HEADS, GLA_VALUE_DIM)
    o = o * lax.rsqrt(jnp.mean(o * o, -1, keepdims=True) + RMS_EPS)
    return o.reshape(bsz, seq, GLA_WIDTH) * norm_g * jax.nn.silu(g.astype(F32))


def hybrid_mixer(x, w_in, ssd_conv_w, ssd_conv_b, ssd_dt_bias, ssd_a_log, ssd_d, ssd_norm_g,
                 rwkv_mu, rwkv_w0, rwkv_w2, rwkv_a0, rwkv_a2, rwkv_g2, rwkv_k_k, rwkv_k_a, rwkv_r_k,
                 rwkv_ln_g, rwkv_ln_b, gla_w_a2, gla_b_a, gla_norm_g, w_out):
    p = x @ w_in
    s_z, s_xbc, s_dt, rw, gq, gk, gv, gg, ga = _split(p, IN_SPLITS)
    y_ssd = ssd_mixer(s_z, s_xbc, s_dt, ssd_conv_w, ssd_conv_b, ssd_dt_bias, ssd_a_log, ssd_d, ssd_norm_g)
    y_rwkv = rwkv7_mixer(rw, rwkv_mu, rwkv_w0, rwkv_w2, rwkv_a0, rwkv_a2, rwkv_g2, rwkv_k_k, rwkv_k_a,
                         rwkv_r_k, rwkv_ln_g, rwkv_ln_b)
    y_gla = gla_mixer(gq, gk, gv, gg, ga, gla_w_a2, gla_b_a, gla_norm_g)
    y = jnp.concatenate([y_ssd, y_rwkv, y_gla], axis=-1).astype(x.dtype)
    return y @ w_out


def grouped_expert_mlp(xt, expert_id, gate, w_gate, w_up, w_down):
    n_tok, d = xt.shape
    n_assign = expert_id.size
    e_flat = expert_id.reshape(-1)
    order = jnp.argsort(e_flat)
    e_sorted = e_flat[order]
    counts = jnp.bincount(e_flat, length=N_EXPERTS)
    padded = (counts + ROW_BLOCK - 1) // ROW_BLOCK * ROW_BLOCK
    pad_end = jnp.cumsum(padded)
    pad_start = pad_end - padded
    start = jnp.cumsum(counts) - counts
    dest = pad_start[e_sorted] + jnp.arange(n_assign, dtype=jnp.int32) - start[e_sorted]
    n_blocks = -(-n_assign // ROW_BLOCK) + N_EXPERTS
    n_rows = n_blocks * ROW_BLOCK
    row_tok = jnp.full((n_rows,), n_tok, jnp.int32).at[dest].set((order // TOP_K).astype(jnp.int32))
    row_gate = jnp.zeros((n_rows,), xt.dtype).at[dest].set(gate.reshape(-1)[order].astype(xt.dtype))
    block_start = jnp.arange(n_blocks, dtype=jnp.int32) * ROW_BLOCK
    block_expert = jnp.minimum(jnp.searchsorted(pad_end, block_start, side='right'), N_EXPERTS - 1)
    x_pad = jnp.concatenate([xt, jnp.zeros((1, d), xt.dtype)], axis=0)
    xs = x_pad[row_tok].reshape(n_blocks, ROW_BLOCK, d)

    def block_mlp(args):
        xb, e = args
        h = jax.nn.silu(xb @ w_gate[e]) * (xb @ w_up[e])
        return h @ w_down[e]

    ys = lax.map(block_mlp, (xs, block_expert)).reshape(n_rows, d)
    out = jnp.zeros((n_tok + 1, d), ys.dtype).at[row_tok].add(ys * row_gate[:, None])
    return out[:n_tok]


def hierarchical_moe(x, w_rg, b_rg, w_re, b_re, w_gate, w_up, w_down):
    bsz, seq, d = x.shape
    xt = x.reshape(-1, d)
    n_tok = xt.shape[0]
    g_prob = jax.nn.softmax((xt @ w_rg + b_rg).astype(F32), axis=-1)
    g_w, g_idx = lax.top_k(g_prob, 1)
    e_logits = (xt @ w_re + b_re).astype(F32).reshape(n_tok, N_GROUPS, EXPERTS_PER_GROUP)
    e_logits = jnp.take_along_axis(e_logits, g_idx[:, :, None], axis=1)[:, 0]
    e_prob = jax.nn.softmax(e_logits, axis=-1)
    e_w, e_local = lax.top_k(e_prob, TOP_K)
    e_w = e_w / jnp.sum(e_w, -1, keepdims=True)
    gate = g_w * e_w
    expert_id = g_idx * EXPERTS_PER_GROUP + e_local
    y = grouped_expert_mlp(xt, expert_id, gate, w_gate, w_up, w_down)
    return y.reshape(bsz, seq, d).astype(x.dtype)


def setup_inputs(seed: int = 0) -> dict:
    key = jax.random.key(seed)
    ks = iter(jax.random.split(key, 48))
    L = DEPTH

    def nrm(shape, scale):
        return scale * jax.random.normal(next(ks), shape, F32)

    def gain(shape):
        return 1.0 + nrm(shape, 0.02)

    x = nrm((BATCH, SEQ, D_MODEL), 1.0)
    ln_in_g = gain((D_MODEL,))
    ln_in_b = nrm((D_MODEL,), 0.02)
    w_in = nrm((L, D_MODEL, N_IN), D_MODEL ** -0.5)
    ssd_conv_w = nrm((L, SSD_CONV, SSD_XBC), SSD_CONV ** -0.5)
    ssd_conv_b = nrm((L, SSD_XBC), 0.02)
    dt = jnp.exp(jax.random.uniform(next(ks), (L, SSD_HEADS), F32, math.log(1e-3), math.log(1e-1)))
    ssd_dt_bias = dt + jnp.log(-jnp.expm1(-dt))
    ssd_a_log = jnp.log(jax.random.uniform(next(ks), (L, SSD_HEADS), F32, 1.0, 16.0))
    ssd_d = 1.0 + nrm((L, SSD_HEADS), 0.1)
    ssd_norm_g = gain((L, SSD_WIDTH))
    rwkv_mu = jax.random.uniform(next(ks), (L, RWKV_IN), F32)
    rwkv_w0 = jnp.linspace(-6.5, -1.5, RWKV_WIDTH, dtype=F32)[None, :] + nrm((L, RWKV_WIDTH), 0.1)
    rwkv_w2 = nrm((L, RWKV_DECAY_RANK, RWKV_WIDTH), 0.1)
    rwkv_a0 = nrm((L, RWKV_WIDTH), 0.1)
    rwkv_a2 = nrm((L, RWKV_ICLR_RANK, RWKV_WIDTH), 0.1)
    rwkv_g2 = nrm((L, RWKV_GATE_RANK, RWKV_WIDTH), RWKV_GATE_RANK ** -0.5)
    rwkv_k_k = 1.0 + nrm((L, RWKV_WIDTH), 0.1)
    rwkv_k_a = 1.0 + nrm((L, RWKV_WIDTH), 0.1)
    rwkv_r_k = nrm((L, RWKV_HEADS, RWKV_HEAD_DIM), 0.1)
    rwkv_ln_g = gain((L, RWKV_WIDTH))
    rwkv_ln_b = nrm((L, RWKV_WIDTH), 0.02)
    gla_w_a2 = nrm((L, GLA_GATE_RANK, GLA_QK_WIDTH), GLA_GATE_RANK ** -0.5)
    gla_b_a = nrm((L, GLA_QK_WIDTH), 0.1)
    gla_norm_g = gain((L, GLA_WIDTH))
    w_out = nrm((L, MIX_WIDTH, D_MODEL), DEEPNORM_BETA * MIX_WIDTH ** -0.5)
    ln1_g = gain((L, D_MODEL))
    ln1_b = nrm((L, D_MODEL), 0.02)
    moe_w_rg = nrm((L, D_MODEL, N_GROUPS), D_MODEL ** -0.5)
    moe_b_rg = nrm((L, N_GROUPS), 0.01)
    moe_w_re = nrm((L, D_MODEL, N_EXPERTS), D_MODEL ** -0.5)
    moe_b_re = nrm((L, N_EXPERTS), 0.01)
    moe_w_gate = nrm((L, N_EXPERTS, D_MODEL, EXPERT_FF), D_MODEL ** -0.5)
    moe_w_up = nrm((L, N_EXPERTS, D_MODEL, EXPERT_FF), D_MODEL ** -0.5)
    moe_w_down = nrm((L, N_EXPERTS, EXPERT_FF, D_MODEL), DEEPNORM_BETA * EXPERT_FF ** -0.5)
    ln2_g = gain((L, D_MODEL))
    ln2_b = nrm((L, D_MODEL), 0.02)
    return {'x': x, 'ln_in_g': ln_in_g, 'ln_in_b': ln_in_b, 'w_in': w_in,
            'ssd_conv_w': ssd_conv_w, 'ssd_conv_b': ssd_conv_b, 'ssd_dt_bias': ssd_dt_bias,
            'ssd_a_log': ssd_a_log, 'ssd_d': ssd_d, 'ssd_norm_g': ssd_norm_g,
            'rwkv_mu': rwkv_mu, 'rwkv_w0': rwkv_w0, 'rwkv_w2': rwkv_w2, 'rwkv_a0': rwkv_a0,
            'rwkv_a2': rwkv_a2, 'rwkv_g2': rwkv_g2, 'rwkv_k_k': rwkv_k_k, 'rwkv_k_a': rwkv_k_a,
            'rwkv_r_k': rwkv_r_k, 'rwkv_ln_g': rwkv_ln_g, 'rwkv_ln_b': rwkv_ln_b,
            'gla_w_a2': gla_w_a2, 'gla_b_a': gla_b_a, 'gla_norm_g': gla_norm_g,
            'w_out': w_out, 'ln1_g': ln1_g, 'ln1_b': ln1_b,
            'moe_w_rg': moe_w_rg, 'moe_b_rg': moe_b_rg, 'moe_w_re': moe_w_re, 'moe_b_re': moe_b_re,
            'moe_w_gate': moe_w_gate, 'moe_w_up': moe_w_up, 'moe_w_down': moe_w_down,
            'ln2_g': ln2_g, 'ln2_b': ln2_b}


def reference(x, ln_in_g, ln_in_b, w_in, ssd_conv_w, ssd_conv_b, ssd_dt_bias, ssd_a_log, ssd_d,
              ssd_norm_g, rwkv_mu, rwkv_w0, rwkv_w2, rwkv_a0, rwkv_a2, rwkv_g2, rwkv_k_k, rwkv_k_a,
              rwkv_r_k, rwkv_ln_g, rwkv_ln_b, gla_w_a2, gla_b_a, gla_norm_g, w_out, ln1_g, ln1_b,
              moe_w_rg, moe_b_rg, moe_w_re, moe_b_re, moe_w_gate, moe_w_up, moe_w_down, ln2_g, ln2_b):
    h = layer_norm(x, ln_in_g, ln_in_b)
    for i in range(DEPTH):
        mix = hybrid_mixer(h, w_in[i], ssd_conv_w[i], ssd_conv_b[i], ssd_dt_bias[i], ssd_a_log[i],
                           ssd_d[i], ssd_norm_g[i], rwkv_mu[i], rwkv_w0[i], rwkv_w2[i], rwkv_a0[i],
                           rwkv_a2[i], rwkv_g2[i], rwkv_k_k[i], rwkv_k_a[i], rwkv_r_k[i], rwkv_ln_g[i],
                           rwkv_ln_b[i], gla_w_a2[i], gla_b_a[i], gla_norm_g[i], w_out[i])
        h = layer_norm(DEEPNORM_ALPHA * h + mix, ln1_g[i], ln1_b[i])
        ffn = hierarchical_moe(h, moe_w_rg[i], moe_b_rg[i], moe_w_re[i], moe_b_re[i],
                               moe_w_gate[i], moe_w_up[i], moe_w_down[i])
        h = layer_norm(DEEPNORM_ALPHA * h + ffn, ln2_g[i], ln2_b[i])
    return h
```

```python
import functools
import math

import jax
import jax.numpy as jnp
from jax import lax
from jax.experimental import pallas as pl
from jax.experimental.pallas import tpu as pltpu

F32 = jnp.float32
BF16 = jnp.bfloat16
HI = lax.Precision.HIGHEST

D_MODEL = 1024
DEPTH = 2

SSD_HEAD_DIM = 64
SSD_WIDTH = D_MODEL // 2
SSD_HEADS = SSD_WIDTH // SSD_HEAD_DIM
SSD_GROUPS = 2
SSD_STATE = 64
SSD_CONV = 4
SSD_BC = SSD_GROUPS * SSD_STATE
SSD_XBC = SSD_WIDTH + 2 * SSD_BC

RWKV_HEAD_DIM = 64
RWKV_WIDTH = D_MODEL // 4
RWKV_HEADS = RWKV_WIDTH // RWKV_HEAD_DIM
RWKV_DECAY_RANK = 32
RWKV_ICLR_RANK = 32
RWKV_GATE_RANK = 64
RWKV_LR = RWKV_DECAY_RANK + RWKV_ICLR_RANK + RWKV_GATE_RANK
RWKV_IN = 3 * RWKV_WIDTH + RWKV_LR
RWKV_GN_EPS = 64e-5

GLA_VALUE_DIM = 64
GLA_WIDTH = D_MODEL // 4
GLA_HEADS = GLA_WIDTH // GLA_VALUE_DIM
GLA_KEY_DIM = GLA_VALUE_DIM // 2
GLA_QK_WIDTH = GLA_HEADS * GLA_KEY_DIM
GLA_GATE_RANK = 16
GLA_GATE_TEMP = 16.0

MIX_WIDTH = SSD_WIDTH + RWKV_WIDTH + GLA_WIDTH

N_GROUPS = 4
EXPERTS_PER_GROUP = 8
N_EXPERTS = N_GROUPS * EXPERTS_PER_GROUP
TOP_K = 2
EXPERT_FF = 512

DEEPNORM_ALPHA = (2 * DEPTH) ** 0.25
LN_EPS = 1e-5
RMS_EPS = 1e-6

LANES = 128
SUBLANES = 8
SSD_IN = SSD_WIDTH + SSD_XBC + LANES
GLA_IN = 2 * GLA_QK_WIDTH + 2 * GLA_WIDTH + LANES

TOKEN_TILE = 256
SSD_BLOCK = 128
RWKV_BLOCK = 64
GLA_BLOCK = 64
ROW_BLOCK = 256
VMEM_LIMIT = 48 * 1024 * 1024


def _dot(a, b):
    return jnp.dot(a.astype(BF16), b.astype(BF16), preferred_element_type=F32)


def _dot_hi(a, b):
    return jnp.dot(a, b, precision=HI, preferred_element_type=F32)


def _split_bf16(x):
    hi = x.astype(BF16)
    return hi, (x - hi.astype(F32)).astype(BF16)


def _dot_x3(a, b):
    a_hi, a_lo = _split_bf16(a)
    b_hi, b_lo = _split_bf16(b)
    return (jnp.dot(a_hi, b_hi, preferred_element_type=F32)
            + (jnp.dot(a_hi, b_lo, preferred_element_type=F32)
               + jnp.dot(a_lo, b_hi, preferred_element_type=F32)))


def _dot_nt(a, b, precision=None):
    return lax.dot_general(a, b, (((1,), (1,)), ((), ())), precision=precision,
                           preferred_element_type=F32)


def _dot_tn(a, b, precision=None):
    return lax.dot_general(a, b, (((0,), (0,)), ((), ())), precision=precision,
                           preferred_element_type=F32)


def _sigmoid(x):
    return 1.0 / (1.0 + jnp.exp(-x))


def _silu(x):
    return x * _sigmoid(x)


def _softplus(x):
    return jnp.maximum(x, 0.0) + jnp.log1p(jnp.exp(-jnp.abs(x)))


def _iota(shape, dim):
    return lax.broadcasted_iota(jnp.int32, shape, dim)


def _block_indicator(rows, cols, row_seg, col_seg):
    r = _iota((rows, cols), 0) // row_seg
    c = _iota((rows, cols), 1) // col_seg
    return (r == c).astype(F32)


def _layer_norm(x, g, b):
    mu = jnp.mean(x, -1, keepdims=True)
    xc = x - mu
    var = jnp.mean(xc * xc, -1, keepdims=True)
    return xc * lax.rsqrt(var + LN_EPS) * g + b


def _params(*sem):
    return pltpu.CompilerParams(dimension_semantics=sem, vmem_limit_bytes=VMEM_LIMIT)


def _ln_kernel(x_ref, g_ref, b_ref, o_ref):
    o_ref[...] = _layer_norm(x_ref[...], g_ref[...], b_ref[...])


def _input_ln(x, g, b):
    n_tok, d = x.shape
    tm = TOKEN_TILE
    row = pl.BlockSpec((tm, d), lambda i: (i, 0))
    vec = pl.BlockSpec((1, d), lambda i: (0, 0))
    return pl.pallas_call(
        _ln_kernel, grid=(n_tok // tm,), in_specs=[row, vec, vec], out_specs=row,
        out_shape=jax.ShapeDtypeStruct((n_tok, d), F32), compiler_params=_params("parallel"),
        name="input_ln")(x, g.reshape(1, d), b.reshape(1, d))


def _inproj_kernel(h_ref, w1_ref, w2_ref, w3_ref, o1_ref, o2_ref, o3_ref):
    hb = h_ref[...].astype(BF16)
    o1_ref[...] = jnp.dot(hb, w1_ref[...], preferred_element_type=F32)
    o2_ref[...] = jnp.dot(hb, w2_ref[...], preferred_element_type=F32)
    o3_ref[...] = jnp.dot(hb, w3_ref[...], preferred_element_type=F32)


def _in_projection(h, w_ssd, w_rwkv, w_gla):
    n_tok, d = h.shape
    tm = TOKEN_TILE
    ws = (w_ssd, w_rwkv, w_gla)
    return pl.pallas_call(
        _inproj_kernel, grid=(n_tok // tm,),
        in_specs=[pl.BlockSpec((tm, d), lambda i: (i, 0))]
        + [pl.BlockSpec(w.shape, lambda i: (0, 0)) for w in ws],
        out_specs=[pl.BlockSpec((tm, w.shape[1]), lambda i: (i, 0)) for w in ws],
        out_shape=[jax.ShapeDtypeStruct((n_tok, w.shape[1]), F32) for w in ws],
        compiler_params=_params("parallel"), name="in_projection")(h, *ws)


def _ssd_kernel(p_ref, cw_ref, cb_ref, dtb_ref, a_ref, d_ref, ng_ref, o_ref, xs_ref, st_ref):
    blk = o_ref.shape[0]
    tail = SUBLANES

    @pl.when(pl.program_id(1) == 0)
    def _():
        xs_ref[0:tail, :] = jnp.zeros((tail, SSD_XBC), F32)
        st_ref[...] = jnp.zeros(st_ref.shape, F32)

    z = p_ref[:, 0:SSD_WIDTH]
    xs_ref[tail:tail + blk, :] = p_ref[:, SSD_WIDTH:SSD_WIDTH + SSD_XBC]
    dt_raw = p_ref[:, SSD_WIDTH + SSD_XBC:SSD_IN]

    acc = jnp.broadcast_to(cb_ref[...], (blk, SSD_XBC))
    for i in range(SSD_CONV):
        acc = acc + cw_ref[i:i + 1, :] * xs_ref[pl.ds(tail - (SSD_CONV - 1) + i, blk), :]
    xs_ref[0:tail, :] = xs_ref[blk:blk + tail, :]
    xbc = _silu(acc)
    xh = xbc[:, 0:SSD_WIDTH]
    bm = xbc[:, SSD_WIDTH:SSD_WIDTH + SSD_BC]
    cm = xbc[:, SSD_WIDTH + SSD_BC:SSD_XBC]

    dt = _softplus(dt_raw + dtb_ref[...])
    a_dt = dt * a_ref[...]
    row = _iota((blk, blk), 0)
    col = _iota((blk, blk), 1)
    causal = col <= row
    a_cs = _dot_hi(causal.astype(F32), a_dt)
    a_cs_t = a_cs.T
    a_last = a_cs[blk - 1:blk, :]

    expand = _block_indicator(LANES, SSD_WIDTH, 1, SSD_HEAD_DIM)
    dt_e = _dot_hi(dt, expand)
    dec_e = _dot_hi(jnp.exp(a_cs), expand)
    te_e = _dot_hi(jnp.exp(a_last - a_cs), expand)
    cd_e = _dot_hi(jnp.broadcast_to(jnp.exp(a_last), (SUBLANES, LANES)), expand)[0:1, :]

    x_dt = xh * dt_e
    state = st_ref[...]
    y_off = _dot(cm, state) * dec_e

    lane = _iota((blk, LANES), 1)
    hpg = SSD_HEADS // SSD_GROUPS
    pairs = []
    for j in range(SSD_HEADS // 2):
        grp = (2 * j) // hpg
        in_grp = (lane // SSD_STATE) == grp
        cb = _dot_nt(jnp.where(in_grp, cm, 0.0).astype(BF16), bm.astype(BF16))
        xp = x_dt[:, j * LANES:(j + 1) * LANES]
        ys = []
        for h in (2 * j, 2 * j + 1):
            diff = jnp.broadcast_to(a_cs[:, h:h + 1], (blk, blk)) - a_cs_t[h:h + 1, :]
            seg = jnp.exp(jnp.where(causal, diff, -jnp.inf))
            ys.append(_dot(cb * seg, xp))
        pairs.append(jnp.where(lane < SSD_HEAD_DIM, ys[0], ys[1]))
    y = jnp.concatenate(pairs, axis=1) + y_off + d_ref[...] * xh

    new = _dot_tn(bm.astype(BF16), (x_dt * te_e).astype(BF16))
    keep = _block_indicator(SSD_BC, SSD_WIDTH, SSD_STATE, SSD_WIDTH // SSD_GROUPS)
    st_ref[...] = state * cd_e + keep * new

    y = y * _silu(z)
    gw = SSD_WIDTH // SSD_GROUPS
    for g in range(SSD_GROUPS):
        yg = y[:, g * gw:(g + 1) * gw]
        ms = jnp.mean(yg * yg, -1, keepdims=True)
        o_ref[:, g * gw:(g + 1) * gw] = (yg * lax.rsqrt(ms + RMS_EPS)
                                         * ng_ref[:, g * gw:(g + 1) * gw]).astype(o_ref.dtype)


def _ssd_mixer(p, conv_w, conv_b, dt_bias, a_neg, d_skip, norm_g):
    bsz, seq, _ = p.shape
    blk = SSD_BLOCK
    small = lambda a: pl.BlockSpec(a.shape, lambda b, c: (0, 0))
    args = (conv_w, conv_b, dt_bias, a_neg, d_skip, norm_g)
    return pl.pallas_call(
        _ssd_kernel, grid=(bsz, seq // blk),
        in_specs=[pl.BlockSpec((None, blk, SSD_IN), lambda b, c: (b, c, 0))] + [small(a) for a in args],
        out_specs=pl.BlockSpec((None, blk, SSD_WIDTH), lambda b, c: (b, c, 0)),
        out_shape=jax.ShapeDtypeStruct((bsz, seq, SSD_WIDTH), BF16),
        scratch_shapes=[pltpu.VMEM((blk + 2 * SUBLANES, SSD_XBC), F32),
                        pltpu.VMEM((SSD_BC, SSD_WIDTH), F32)],
        compiler_params=_params("parallel", "arbitrary"), name="ssd_mixer")(p, *args)


def _rwkv_kernel(p_ref, mu_ref, wlr_ref, w0_ref, a0_ref, kk_ref, ka_ref, rk_ref, lng_ref, lnb_ref,
                 o_ref, xs_ref, st_ref):
    blk = o_ref.shape[0]
    tail = SUBLANES
    width = RWKV_WIDTH

    @pl.when(pl.program_id(1) == 0)
    def _():
        xs_ref[0:tail, :] = jnp.zeros((tail, RWKV_IN), F32)
        st_ref[...] = jnp.zeros(st_ref.shape, F32)

    p = p_ref[...]
    xs_ref[tail:tail + blk, :] = p
    prev = xs_ref[pl.ds(tail - 1, blk), :]
    xs_ref[0:tail, :] = xs_ref[blk:blk + tail, :]
    pm = p + (prev - p) * mu_ref[...]
    r = pm[:, 0:width]
    k = pm[:, width:2 * width]
    v = pm[:, 2 * width:3 * width]
    lr = pm[:, 3 * width:RWKV_IN]

    lane = _iota((blk, RWKV_LR), 1)
    lr_act = jnp.where(lane < RWKV_DECAY_RANK, jnp.tanh(lr),
                       jnp.where(lane < RWKV_DECAY_RANK + RWKV_ICLR_RANK, lr, _sigmoid(lr)))
    proj = _dot_hi(lr_act, wlr_ref[...])
    w = -_softplus(-(w0_ref[...] + proj[:, 0:width])) - 0.5
    log_decay = -jnp.exp(w)
    a = _sigmoid(a0_ref[...] + proj[:, width:2 * width])
    g = proj[:, 2 * width:3 * width]

    head_sum = _block_indicator(width, width, RWKV_HEAD_DIM, RWKV_HEAD_DIM)
    kk = k * kk_ref[...]
    kk = kk * lax.rsqrt(_dot_hi(kk * kk, head_sum) + 1e-12)
    k2 = k * (1.0 + (a - 1.0) * ka_ref[...])
    alpha = -kk
    beta = kk * a

    row = _iota((blk, blk), 0)
    col = _iota((blk, blk), 1)
    incl = col <= row
    strict = col < row
    cs = _dot_hi(incl.astype(F32), log_decay)
    e_neg = jnp.exp(-cs)
    a_t = alpha * jnp.exp(cs - log_decay)
    b_t = beta * e_neg
    k_t = k2 * e_neg
    r_t = r * jnp.exp(cs)

    state = st_ref[...]
    a_s = _dot_x3(a_t, state)
    r_s = _dot_x3(r_t, state)
    b_tt = b_t.T
    k_tt = k_t.T

    wlane = _iota((blk, width), 1)
    u_all = jnp.zeros((blk, width), F32)
    y_all = jnp.zeros((blk, width), F32)
    for h in range(RWKV_HEADS):
        in_head = (wlane // RWKV_HEAD_DIM) == h
        lhs = jnp.concatenate([jnp.where(in_head, a_t, 0.0), jnp.where(in_head, r_t, 0.0)], axis=0)
        pb = _dot_hi(lhs, b_tt)
        pk = _dot_hi(lhs, k_tt)
        a_ab = jnp.where(strict, pb[0:blk], 0.0)
        a_ak = jnp.where(strict, pk[0:blk], 0.0)
        a_rb = jnp.where(incl, pb[blk:2 * blk], 0.0)
        a_rk = jnp.where(incl, pk[blk:2 * blk], 0.0)
        u = a_s + _dot_hi(a_ak, v)
        npow = a_ab
        steps = int(math.log2(blk))
        for i in range(steps):
            u = u + _dot_hi(npow, u)
            if i + 1 < steps:
                npow = _dot_hi(npow, npow)
        y = _dot_hi(a_rb, u) + _dot_hi(a_rk, v)
        u_all = jnp.where(in_head, u, u_all)
        y_all = jnp.where(in_head, y, y_all)
    y_all = y_all + r_s

    c_last = cs[blk - 1:blk, :]
    to_end = jnp.exp(c_last - cs)
    new = _dot_tn(beta * to_end, u_all, HI) + _dot_tn(k2 * to_end, v, HI)
    chunk_decay = jnp.broadcast_to(jnp.exp(c_last), (SUBLANES, width)).T[:, 0:1]
    st_ref[...] = state * chunk_decay + head_sum * new

    head_mean = head_sum * (1.0 / RWKV_HEAD_DIM)
    mean = _dot_hi(y_all, head_mean)
    yc = y_all - mean
    var = _dot_hi(yc * yc, head_mean)
    yn = yc * lax.rsqrt(var + RWKV_GN_EPS) * lng_ref[...] + lnb_ref[...]
    bonus = _dot_hi(r * k2 * rk_ref[...], head_sum) * v
    o_ref[...] = ((yn + bonus) * g).astype(o_ref.dtype)


def _rwkv_mixer(p, mu, wlr, w0, a0, k_k, k_a, r_k, ln_g, ln_b):
    bsz, seq, _ = p.shape
    blk = RWKV_BLOCK
    small = lambda a: pl.BlockSpec(a.shape, lambda b, c: (0, 0))
    args = (mu, wlr, w0, a0, k_k, k_a, r_k, ln_g, ln_b)
    return pl.pallas_call(
        _rwkv_kernel, grid=(bsz, seq // blk),
        in_specs=[pl.BlockSpec((None, blk, RWKV_IN), lambda b, c: (b, c, 0))] + [small(a) for a in args],
        out_specs=pl.BlockSpec((None, blk, RWKV_WIDTH), lambda b, c: (b, c, 0)),
        out_shape=jax.ShapeDtypeStruct((bsz, seq, RWKV_WIDTH), BF16),
        scratch_shapes=[pltpu.VMEM((blk + 2 * SUBLANES, RWKV_IN), F32),
                        pltpu.VMEM((RWKV_WIDTH, RWKV_WIDTH), F32)],
        compiler_params=_params("parallel", "arbitrary"), name="rwkv7_mixer")(p, *args)


def _gla_kernel(p_ref, wa_ref, ba_ref, ng_ref, o_ref, st_ref, z_ref, acc_ref):
    blk = o_ref.shape[0]
    qk = GLA_QK_WIDTH
    vw = GLA_WIDTH

    @pl.when(pl.program_id(1) == 0)
    def _():
        st_ref[...] = jnp.zeros(st_ref.shape, F32)

    q = p_ref[:, 0:qk] * (GLA_KEY_DIM ** -0.5)
    k = p_ref[:, qk:2 * qk]
    g = p_ref[:, 2 * qk + vw:2 * qk + 2 * vw]
    a_lr = p_ref[:, 2 * qk + 2 * vw:GLA_IN]
    v_off = 2 * qk

    logit = _dot_hi(a_lr, wa_ref[...]) + ba_ref[...]
    log_a = -_softplus(-logit) / GLA_GATE_TEMP
    row = _iota((blk, blk), 0)
    col = _iota((blk, blk), 1)
    cum = _dot_hi((col <= row).astype(F32), log_a)

    state = st_ref[...]
    acc_ref[...] = _dot_nt((q * jnp.exp(cum)).astype(BF16), state.astype(BF16))

    spread = _block_indicator(qk, vw, GLA_KEY_DIM, GLA_VALUE_DIM).astype(BF16)
    rows = _iota((blk, qk), 0)
    group = 2 * SUBLANES
    for jb in range(blk // group):
        r0 = jb * group
        n = blk - r0
        for jj in range(group):
            j = r0 + jj
            decay = jnp.exp(jnp.where(rows[r0:] >= j, cum[r0:] - cum[j:j + 1, :], -jnp.inf))
            z_ref[jj * n:(jj + 1) * n, :] = (q[r0:] * decay * k[j:j + 1, :]).astype(BF16)
        res = jnp.dot(z_ref[0:group * n, :], spread, preferred_element_type=F32)
        part = jnp.zeros((n, vw), F32)
        for jj in range(group):
            j = r0 + jj
            part = part + res[jj * n:(jj + 1) * n] * p_ref[j:j + 1, v_off:v_off + vw]
        acc_ref[r0:blk, :] = acc_ref[r0:blk, :] + part

    v = p_ref[:, v_off:v_off + vw]
    last = cum[blk - 1:blk, :]
    k_dec = k * jnp.exp(last - cum)
    keep = _block_indicator(vw, qk, GLA_VALUE_DIM, GLA_KEY_DIM)
    st_ref[...] = state * jnp.exp(last) + keep * _dot_tn(v.astype(BF16), k_dec.astype(BF16))

    o = acc_ref[...]
    head_mean = _block_indicator(vw, vw, GLA_VALUE_DIM, GLA_VALUE_DIM) * (1.0 / GLA_VALUE_DIM)
    ms = _dot_hi(o * o, head_mean)
    o_ref[...] = (o * lax.rsqrt(ms + RMS_EPS) * ng_ref[...] * _silu(g)).astype(o_ref.dtype)


def _gla_mixer(p, w_a2, b_a, norm_g):
    bsz, seq, _ = p.shape
    blk = GLA_BLOCK
    small = lambda a: pl.BlockSpec(a.shape, lambda b, c: (0, 0))
    args = (w_a2, b_a, norm_g)
    return pl.pallas_call(
        _gla_kernel, grid=(bsz, seq // blk),
        in_specs=[pl.BlockSpec((None, blk, GLA_IN), lambda b, c: (b, c, 0))] + [small(a) for a in args],
        out_specs=pl.BlockSpec((None, blk, GLA_WIDTH), lambda b, c: (b, c, 0)),
        out_shape=jax.ShapeDtypeStruct((bsz, seq, GLA_WIDTH), BF16),
        scratch_shapes=[pltpu.VMEM((GLA_WIDTH, GLA_QK_WIDTH), F32),
                        pltpu.VMEM((2 * SUBLANES * blk, GLA_QK_WIDTH), BF16),
                        pltpu.VMEM((blk, GLA_WIDTH), F32)],
        compiler_params=_params("parallel", "arbitrary"), name="gla_mixer")(p, *args)


def _outproj_kernel(ys_ref, yr_ref, yg_ref, h_ref, w1_ref, w2_ref, w3_ref, g_ref, b_ref, wr_ref, br_ref,
                    h1_ref, eid_ref, gate_ref):
    mix = (jnp.dot(ys_ref[...], w1_ref[...], preferred_element_type=F32)
           + jnp.dot(yr_ref[...], w2_ref[...], preferred_element_type=F32)
           + jnp.dot(yg_ref[...], w3_ref[...], preferred_element_type=F32))
    h1 = _layer_norm(DEEPNORM_ALPHA * h_ref[...] + mix, g_ref[...], b_ref[...])
    h1_ref[...] = h1

    logits = _dot_hi(h1, wr_ref[...]) + br_ref[...]
    tm = logits.shape[0]
    lane = _iota((tm, LANES), 1)
    lane_f = lane.astype(F32)

    def masked_softmax(mask):
        m = jnp.max(jnp.where(mask, logits, -jnp.inf), -1, keepdims=True)
        e = jnp.where(mask, jnp.exp(logits - m), 0.0)
        return e / jnp.sum(e, -1, keepdims=True)

    def first_argmax(vals, mask):
        m = jnp.max(jnp.where(mask, vals, -jnp.inf), -1, keepdims=True)
        idx = jnp.min(jnp.where(mask & (vals == m), lane_f, float(LANES)), -1, keepdims=True)
        return m, idx.astype(jnp.int32)

    is_group = lane < N_GROUPS
    g_prob = masked_softmax(is_group)
    g_w, g_idx = first_argmax(g_prob, is_group)
    e_lo = N_GROUPS + g_idx * EXPERTS_PER_GROUP
    in_group = (lane >= e_lo) & (lane < e_lo + EXPERTS_PER_GROUP)
    e_prob = masked_softmax(in_group)
    p1, i1 = first_argmax(e_prob, in_group)
    p2, i2 = first_argmax(e_prob, in_group & (lane != i1))
    denom = p1 + p2
    eid_ref[...] = jnp.where(lane == 0, i1 - N_GROUPS, jnp.where(lane == 1, i2 - N_GROUPS, 0))
    gate_ref[...] = jnp.where(lane == 0, g_w * (p1 / denom), jnp.where(lane == 1, g_w * (p2 / denom), 0.0))


def _out_projection(y_ssd, y_rwkv, y_gla, h, w_out, ln_g, ln_b, w_r, b_r):
    n_tok, d = h.shape
    tm = TOKEN_TILE
    row = lambda n: pl.BlockSpec((tm, n), lambda i: (i, 0))
    full = lambda a: pl.BlockSpec(a.shape, lambda i: (0, 0))
    w1 = w_out[0:SSD_WIDTH]
    w2 = w_out[SSD_WIDTH:SSD_WIDTH + RWKV_WIDTH]
    w3 = w_out[SSD_WIDTH + RWKV_WIDTH:MIX_WIDTH]
    consts = (w1, w2, w3, ln_g, ln_b, w_r, b_r)
    return pl.pallas_call(
        _outproj_kernel, grid=(n_tok // tm,),
        in_specs=[row(SSD_WIDTH), row(RWKV_WIDTH), row(GLA_WIDTH), row(d)] + [full(a) for a in consts],
        out_specs=[row(d), row(LANES), row(LANES)],
        out_shape=[jax.ShapeDtypeStruct((n_tok, d), F32),
                   jax.ShapeDtypeStruct((n_tok, LANES), jnp.int32),
                   jax.ShapeDtypeStruct((n_tok, LANES), F32)],
        compiler_params=_params("parallel"), name="out_projection")(y_ssd, y_rwkv, y_gla, h, *consts)


def _dest_kernel(eid_ref, dest_ref, cnt_ref, run_ref, base_ref):
    phase = pl.program_id(0)
    step = pl.program_id(1)
    tm = eid_ref.shape[0]
    e = eid_ref[...]
    lane = _iota((tm, LANES), 1)
    oh0 = (lane == e[:, 0:1]).astype(F32)
    oh1 = (lane == e[:, 1:2]).astype(F32)
    oh = oh0 + oh1
    tile_count = jnp.sum(oh, 0, keepdims=True)

    @pl.when(step == 0)
    def _():
        run_ref[...] = jnp.zeros(run_ref.shape, F32)

    @pl.when((phase == 1) & (step == 0))
    def _():
        blocks = jnp.ceil(cnt_ref[...] * (1.0 / ROW_BLOCK))
        before = (_iota((LANES, LANES), 0) < _iota((LANES, LANES), 1)).astype(BF16)
        start = jnp.dot(jnp.broadcast_to(blocks, (SUBLANES, LANES)).astype(BF16), before,
                        preferred_element_type=F32)[0:1, :]
        base_ref[...] = start * ROW_BLOCK

    @pl.when(phase == 0)
    def _():
        total = run_ref[...] + tile_count
        run_ref[...] = total
        cnt_ref[...] = total

    @pl.when(phase == 1)
    def _():
        earlier = (_iota((tm, tm), 1) < _iota((tm, tm), 0)).astype(BF16)
        rank = jnp.dot(earlier, oh.astype(BF16), preferred_element_type=F32)
        pos = rank + run_ref[...] + base_ref[...]
        d0 = jnp.sum(oh0 * pos, -1, keepdims=True)
        d1 = jnp.sum(oh1 * pos, -1, keepdims=True)
        dest_ref[...] = jnp.where(lane == 0, d0, jnp.where(lane == 1, d1, 0.0)).astype(jnp.int32)
        run_ref[...] = run_ref[...] + tile_count


def _dispatch_plan(eid):
    n_tok = eid.shape[0]
    tm = TOKEN_TILE
    return pl.pallas_call(
        _dest_kernel, grid=(2, n_tok // tm),
        in_specs=[pl.BlockSpec((tm, LANES), lambda ph, i: (i, 0))],
        out_specs=[pl.BlockSpec((tm, LANES), lambda ph, i: (i * ph, 0)),
                   pl.BlockSpec((1, LANES), lambda ph, i: (0, 0))],
        out_shape=[jax.ShapeDtypeStruct((n_tok, LANES), jnp.int32),
                   jax.ShapeDtypeStruct((1, LANES), F32)],
        scratch_shapes=[pltpu.VMEM((1, LANES), F32), pltpu.VMEM((1, LANES), F32)],
        compiler_params=_params("arbitrary", "arbitrary"), name="dispatch_plan")(eid)


def _row_copy(src_ref, src_row, dst_ref, dst_row, sem):
    return pltpu.make_async_copy(src_ref.at[pl.ds(src_row, 1)], dst_ref.at[pl.ds(dst_row, 1)], sem)


def _scatter_kernel(dest_ref, h_ref, xs_in_ref, xs_ref, sem):
    del xs_in_ref
    tm = dest_ref.shape[2] // TOP_K
    base = pl.program_id(0) * tm

    def start(t, carry):
        for k in range(TOP_K):
            _row_copy(h_ref, base + t, xs_ref, dest_ref[0, 0, TOP_K * t + k], sem).start()
        return carry

    def wait(t, carry):
        for k in range(TOP_K):
            _row_copy(h_ref, base + t, xs_ref, dest_ref[0, 0, TOP_K * t + k], sem).wait()
        return carry

    lax.fori_loop(0, tm, start, 0)
    lax.fori_loop(0, tm, wait, 0)


def _scatter_rows(dest3, h1, n_rows):
    n_tok, d = h1.shape
    n_tiles = dest3.shape[0]
    xs0 = jnp.zeros((n_rows, d), h1.dtype)
    return pl.pallas_call(
        _scatter_kernel, grid=(n_tiles,),
        in_specs=[pl.BlockSpec((1, 1, dest3.shape[2]), lambda i: (i, 0, 0), memory_space=pltpu.SMEM),
                  pl.BlockSpec(memory_space=pl.ANY), pl.BlockSpec(memory_space=pl.ANY)],
        out_specs=pl.BlockSpec(memory_space=pl.ANY),
        out_shape=jax.ShapeDtypeStruct((n_rows, d), h1.dtype),
        scratch_shapes=[pltpu.SemaphoreType.DMA(())],
        input_output_aliases={2: 0},
        compiler_params=_params("arbitrary"), name="expert_scatter")(dest3, h1, xs0)


def _expert_kernel(be_ref, x_ref, wg_ref, wu_ref, wd_ref, y_ref):
    del be_ref
    x = x_ref[...].astype(BF16)
    gate = jnp.dot(x, wg_ref[...].astype(BF16), preferred_element_type=F32)
    up = jnp.dot(x, wu_ref[...].astype(BF16), preferred_element_type=F32)
    mid = (_silu(gate) * up).astype(BF16)
    y_ref[...] = jnp.dot(mid, wd_ref[...].astype(BF16), preferred_element_type=F32)


def _expert_mlp(block_expert, xs, w_gate, w_up, w_down, layer):
    n_rows, d = xs.shape
    n_blocks = n_rows // ROW_BLOCK
    ff = w_gate.shape[-1]
    return pl.pallas_call(
        _expert_kernel,
        grid_spec=pltpu.PrefetchScalarGridSpec(
            num_scalar_prefetch=1, grid=(n_blocks,),
            in_specs=[pl.BlockSpec((ROW_BLOCK, d), lambda b, be: (b, 0)),
                      pl.BlockSpec((None, None, d, ff), lambda b, be: (layer, be[b], 0, 0)),
                      pl.BlockSpec((None, None, d, ff), lambda b, be: (layer, be[b], 0, 0)),
                      pl.BlockSpec((None, None, ff, d), lambda b, be: (layer, be[b], 0, 0))],
            out_specs=pl.BlockSpec((ROW_BLOCK, d), lambda b, be: (b, 0))),
        out_shape=jax.ShapeDtypeStruct((n_rows, d), F32),
        compiler_params=_params("arbitrary"), name="expert_mlp")(block_expert, xs, w_gate, w_up, w_down)


def _combine_kernel(dest_ref, y_ref, h_ref, gate_ref, g_ref, b_ref, o_ref, buf_ref, sem):
    tm = o_ref.shape[0]

    def start(t, carry):
        for k in range(TOP_K):
            _row_copy(y_ref, dest_ref[0, 0, TOP_K * t + k], buf_ref.at[k], t, sem).start()
        return carry

    def wait(t, carry):
        for k in range(TOP_K):
            _row_copy(y_ref, dest_ref[0, 0, TOP_K * t + k], buf_ref.at[k], t, sem).wait()
        return carry

    lax.fori_loop(0, tm, start, 0)
    lax.fori_loop(0, tm, wait, 0)
    gate = gate_ref[...]
    ffn = gate[:, 0:1] * buf_ref[0] + gate[:, 1:2] * buf_ref[1]
    o_ref[...] = _layer_norm(DEEPNORM_ALPHA * h_ref[...] + ffn, g_ref[...], b_ref[...])


def _combine(dest3, ys, h1, gate, ln_g, ln_b):
    n_tok, d = h1.shape
    tm = TOKEN_TILE
    row = lambda n: pl.BlockSpec((tm, n), lambda i: (i, 0))
    vec = pl.BlockSpec((1, d), lambda i: (0, 0))
    return pl.pallas_call(
        _combine_kernel, grid=(n_tok // tm,),
        in_specs=[pl.BlockSpec((1, 1, dest3.shape[2]), lambda i: (i, 0, 0), memory_space=pltpu.SMEM),
                  pl.BlockSpec(memory_space=pl.ANY), row(d), row(LANES), vec, vec],
        out_specs=row(d),
        out_shape=jax.ShapeDtypeStruct((n_tok, d), F32),
        scratch_shapes=[pltpu.VMEM((TOP_K, tm, d), F32), pltpu.SemaphoreType.DMA(())],
        compiler_params=_params("arbitrary"), name="expert_combine")(dest3, ys, h1, gate, ln_g, ln_b)


def _pad_lanes(a, n):
    return jnp.pad(a, [(0, 0)] * (a.ndim - 1) + [(0, n - a.shape[-1])])


def _row(a, n=None):
    a = a.reshape(1, -1).astype(F32)
    return a if n is None else _pad_lanes(a, n)


def _mixer_layer(h, w_in, ssd_conv_w, ssd_conv_b, ssd_dt_bias, ssd_a_log, ssd_d, ssd_norm_g,
                 rwkv_mu, rwkv_w0, rwkv_w2, rwkv_a0, rwkv_a2, rwkv_g2, rwkv_k_k, rwkv_k_a, rwkv_r_k,
                 rwkv_ln_g, rwkv_ln_b, gla_w_a2, gla_b_a, gla_norm_g, bsz, seq):
    o = 0
    cols = {}
    for name, n in (("z", SSD_WIDTH), ("xbc", SSD_XBC), ("dt", SSD_HEADS), ("rwkv", RWKV_IN),
                    ("q", GLA_QK_WIDTH), ("k", GLA_QK_WIDTH), ("v", GLA_WIDTH), ("g", GLA_WIDTH),
                    ("ga", GLA_GATE_RANK)):
        cols[name] = w_in[:, o:o + n]
        o += n
    w_ssd = jnp.concatenate([cols["z"], cols["xbc"], _pad_lanes(cols["dt"], LANES)], 1).astype(BF16)
    w_rwkv = cols["rwkv"].astype(BF16)
    w_gla = jnp.concatenate([cols["q"], cols["k"], cols["v"], cols["g"], _pad_lanes(cols["ga"], LANES)],
                            1).astype(BF16)
    p_ssd, p_rwkv, p_gla = _in_projection(h, w_ssd, w_rwkv, w_gla)

    y_ssd = _ssd_mixer(
        p_ssd.reshape(bsz, seq, SSD_IN), ssd_conv_w.astype(F32), _row(ssd_conv_b), _row(ssd_dt_bias, LANES),
        _row(-jnp.exp(ssd_a_log.astype(F32)), LANES), _row(jnp.repeat(ssd_d.astype(F32), SSD_HEAD_DIM)),
        _row(ssd_norm_g))

    w_lr = jnp.zeros((RWKV_LR, 3 * RWKV_WIDTH), F32)
    w_lr = w_lr.at[0:RWKV_DECAY_RANK, 0:RWKV_WIDTH].set(rwkv_w2)
    w_lr = w_lr.at[RWKV_DECAY_RANK:RWKV_DECAY_RANK + RWKV_ICLR_RANK, RWKV_WIDTH:2 * RWKV_WIDTH].set(rwkv_a2)
    w_lr = w_lr.at[RWKV_DECAY_RANK + RWKV_ICLR_RANK:, 2 * RWKV_WIDTH:].set(rwkv_g2)
    y_rwkv = _rwkv_mixer(
        p_rwkv.reshape(bsz, seq, RWKV_IN), _row(rwkv_mu), w_lr, _row(rwkv_w0), _row(rwkv_a0), _row(rwkv_k_k),
        _row(rwkv_k_a), _row(rwkv_r_k), _row(rwkv_ln_g), _row(rwkv_ln_b))

    w_a2 = jnp.zeros((LANES, GLA_QK_WIDTH), F32).at[0:GLA_GATE_RANK].set(gla_w_a2)
    y_gla = _gla_mixer(p_gla.reshape(bsz, seq, GLA_IN), w_a2, _row(gla_b_a), _row(gla_norm_g))

    n_tok = bsz * seq
    return (y_ssd.reshape(n_tok, SSD_WIDTH), y_rwkv.reshape(n_tok, RWKV_WIDTH),
            y_gla.reshape(n_tok, GLA_WIDTH))


def _moe_layer(h1, eid, gate, layer, w_gate, w_up, w_down, ln_g, ln_b):
    n_tok = h1.shape[0]
    tm = TOKEN_TILE
    n_blocks = n_tok * TOP_K // ROW_BLOCK + N_EXPERTS
    dest, counts = _dispatch_plan(eid)
    blocks = jnp.ceil(counts[0, 0:N_EXPERTS] / ROW_BLOCK).astype(jnp.int32)
    block_end = jnp.cumsum(blocks)
    block_expert = jnp.minimum(
        jnp.searchsorted(block_end, jnp.arange(n_blocks, dtype=jnp.int32), side="right"),
        N_EXPERTS - 1).astype(jnp.int32)
    dest3 = dest[:, 0:TOP_K].reshape(n_tok // tm, 1, tm * TOP_K)
    xs = _scatter_rows(dest3, h1, n_blocks * ROW_BLOCK)
    ys = _expert_mlp(block_expert, xs, w_gate, w_up, w_down, layer)
    return _combine(dest3, ys, h1, gate, ln_g, ln_b)


def kernel(x, ln_in_g, ln_in_b, w_in, ssd_conv_w, ssd_conv_b, ssd_dt_bias, ssd_a_log, ssd_d, ssd_norm_g, rwkv_mu, rwkv_w0, rwkv_w2, rwkv_a0, rwkv_a2, rwkv_g2, rwkv_k_k, rwkv_k_a, rwkv_r_k, rwkv_ln_g, rwkv_ln_b, gla_w_a2, gla_b_a, gla_norm_g, w_out, ln1_g, ln1_b, moe_w_rg, moe_b_rg, moe_w_re, moe_b_re, moe_w_gate, moe_w_up, moe_w_down, ln2_g, ln2_b):
    bsz, seq, d = x.shape
    n_tok = bsz * seq
    h = _input_ln(x.reshape(n_tok, d), ln_in_g, ln_in_b)
    for i in range(w_in.shape[0]):
        y_ssd, y_rwkv, y_gla = _mixer_layer(
            h, w_in[i], ssd_conv_w[i], ssd_conv_b[i], ssd_dt_bias[i], ssd_a_log[i], ssd_d[i], ssd_norm_g[i],
            rwkv_mu[i], rwkv_w0[i], rwkv_w2[i], rwkv_a0[i], rwkv_a2[i], rwkv_g2[i], rwkv_k_k[i],
            rwkv_k_a[i], rwkv_r_k[i], rwkv_ln_g[i], rwkv_ln_b[i], gla_w_a2[i], gla_b_a[i], gla_norm_g[i],
            bsz, seq)
        w_r = _pad_lanes(jnp.concatenate([moe_w_rg[i], moe_w_re[i]], 1).astype(F32), LANES)
        b_r = _row(jnp.concatenate([moe_b_rg[i], moe_b_re[i]]), LANES)
        h1, eid, gate = _out_projection(
            y_ssd, y_rwkv, y_gla, h, w_out[i].astype(BF16), _row(ln1_g[i]), _row(ln1_b[i]), w_r, b_r)
        h = _moe_layer(h1, eid, gate, i, moe_w_gate, moe_w_up, moe_w_down, _row(ln2_g[i]), _row(ln2_b[i]))
    return h.reshape(bsz, seq, d)
```

```python
import functools
import math

import jax
import jax.numpy as jnp
from jax import lax
from jax.experimental import pallas as pl
from jax.experimental.pallas import tpu as pltpu

F32 = jnp.float32
BF16 = jnp.bfloat16
HI = lax.Precision.HIGHEST

D_MODEL = 1024
DEPTH = 2

SSD_HEAD_DIM = 64
SSD_WIDTH = D_MODEL // 2
SSD_HEADS = SSD_WIDTH // SSD_HEAD_DIM
SSD_GROUPS = 2
SSD_STATE = 64
SSD_CONV = 4
SSD_BC = SSD_GROUPS * SSD_STATE
SSD_XBC = SSD_WIDTH + 2 * SSD_BC

RWKV_HEAD_DIM = 64
RWKV_WIDTH = D_MODEL // 4
RWKV_HEADS = RWKV_WIDTH // RWKV_HEAD_DIM
RWKV_DECAY_RANK = 32
RWKV_ICLR_RANK = 32
RWKV_GATE_RANK = 64
RWKV_LR = RWKV_DECAY_RANK + RWKV_ICLR_RANK + RWKV_GATE_RANK
RWKV_IN = 3 * RWKV_WIDTH + RWKV_LR
RWKV_GN_EPS = 64e-5

GLA_VALUE_DIM = 64
GLA_WIDTH = D_MODEL // 4
GLA_HEADS = GLA_WIDTH // GLA_VALUE_DIM
GLA_KEY_DIM = GLA_VALUE_DIM // 2
GLA_QK_WIDTH = GLA_HEADS * GLA_KEY_DIM
GLA_GATE_RANK = 16
GLA_GATE_TEMP = 16.0

MIX_WIDTH = SSD_WIDTH + RWKV_WIDTH + GLA_WIDTH

N_GROUPS = 4
EXPERTS_PER_GROUP = 8
N_EXPERTS = N_GROUPS * EXPERTS_PER_GROUP
TOP_K = 2
EXPERT_FF = 512

DEEPNORM_ALPHA = (2 * DEPTH) ** 0.25
LN_EPS = 1e-5
RMS_EPS = 1e-6

LANES = 128
SUBLANES = 8
SSD_IN = SSD_WIDTH + SSD_XBC + LANES
GLA_IN = 2 * GLA_QK_WIDTH + 2 * GLA_WIDTH + LANES

TOKEN_TILE = 256
SSD_BLOCK = 128
RWKV_BLOCK = 64
GLA_BLOCK = 64
ROW_BLOCK = 256
VMEM_LIMIT = 48 * 1024 * 1024


def _dot(a, b):
    return jnp.dot(a.astype(BF16), b.astype(BF16), preferred_element_type=F32)


def _dot_hi(a, b):
    return jnp.dot(a, b, precision=HI, preferred_element_type=F32)


def _split_bf16(x):
    hi = x.astype(BF16)
    return hi, (x - hi.astype(F32)).astype(BF16)


def _dot_rhs2(a_exact, b):
    b_hi, b_lo = _split_bf16(b)
    return (jnp.dot(a_exact, b_hi, preferred_element_type=F32)
            + jnp.dot(a_exact, b_lo, preferred_element_type=F32))


def _dot_lhs2(a, b_exact):
    a_hi, a_lo = _split_bf16(a)
    return (jnp.dot(a_hi, b_exact, preferred_element_type=F32)
            + jnp.dot(a_lo, b_exact, preferred_element_type=F32))


def _dot_nt(a, b, precision=None):
    return lax.dot_general(a, b, (((1,), (1,)), ((), ())), precision=precision,
                           preferred_element_type=F32)


def _dot_tn(a, b, precision=None):
    return lax.dot_general(a, b, (((0,), (0,)), ((), ())), precision=precision,
                           preferred_element_type=F32)


def _sigmoid(x):
    return 1.0 / (1.0 + jnp.exp(-x))


def _silu(x):
    return x * _sigmoid(x)


def _softplus(x):
    return jnp.maximum(x, 0.0) + jnp.log1p(jnp.exp(-jnp.abs(x)))


def _iota(shape, dim):
    return lax.broadcasted_iota(jnp.int32, shape, dim)


def _block_indicator(rows, cols, row_seg, col_seg):
    r = _iota((rows, cols), 0) // row_seg
    c = _iota((rows, cols), 1) // col_seg
    return (r == c).astype(F32)


def _layer_norm(x, g, b):
    mu = jnp.mean(x, -1, keepdims=True)
    xc = x - mu
    var = jnp.mean(xc * xc, -1, keepdims=True)
    return xc * lax.rsqrt(var + LN_EPS) * g + b


def _params(*sem):
    return pltpu.CompilerParams(dimension_semantics=sem, vmem_limit_bytes=VMEM_LIMIT)


def _ln_kernel(x_ref, g_ref, b_ref, o_ref):
    o_ref[...] = _layer_norm(x_ref[...], g_ref[...], b_ref[...])


def _input_ln(x, g, b):
    n_tok, d = x.shape
    tm = TOKEN_TILE
    row = pl.BlockSpec((tm, d), lambda i: (i, 0))
    vec = pl.BlockSpec((1, d), lambda i: (0, 0))
    return pl.pallas_call(
        _ln_kernel, grid=(n_tok // tm,), in_specs=[row, vec, vec], out_specs=row,
        out_shape=jax.ShapeDtypeStruct((n_tok, d), F32), compiler_params=_params("parallel"),
        name="input_ln")(x, g.reshape(1, d), b.reshape(1, d))


def _inproj_kernel(h_ref, w1_ref, w2_ref, w3_ref, o1_ref, o2_ref, o3_ref):
    hb = h_ref[...].astype(BF16)
    o1_ref[...] = jnp.dot(hb, w1_ref[...], preferred_element_type=F32)
    o2_ref[...] = jnp.dot(hb, w2_ref[...], preferred_element_type=F32)
    o3_ref[...] = jnp.dot(hb, w3_ref[...], preferred_element_type=F32)


def _in_projection(h, w_ssd, w_rwkv, w_gla):
    n_tok, d = h.shape
    tm = TOKEN_TILE
    ws = (w_ssd, w_rwkv, w_gla)
    return pl.pallas_call(
        _inproj_kernel, grid=(n_tok // tm,),
        in_specs=[pl.BlockSpec((tm, d), lambda i: (i, 0))]
        + [pl.BlockSpec(w.shape, lambda i: (0, 0)) for w in ws],
        out_specs=[pl.BlockSpec((tm, w.shape[1]), lambda i: (i, 0)) for w in ws],
        out_shape=[jax.ShapeDtypeStruct((n_tok, w.shape[1]), F32) for w in ws],
        compiler_params=_params("parallel"), name="in_projection")(h, *ws)


def _ssd_kernel(p_ref, cw_ref, cb_ref, dtb_ref, a_ref, d_ref, ng_ref, o_ref, xs_ref, st_ref):
    blk = o_ref.shape[0]
    tail = SUBLANES

    @pl.when(pl.program_id(1) == 0)
    def _():
        xs_ref[0:tail, :] = jnp.zeros((tail, SSD_XBC), F32)
        st_ref[...] = jnp.zeros(st_ref.shape, F32)

    z = p_ref[:, 0:SSD_WIDTH]
    xs_ref[tail:tail + blk, :] = p_ref[:, SSD_WIDTH:SSD_WIDTH + SSD_XBC]
    dt_raw = p_ref[:, SSD_WIDTH + SSD_XBC:SSD_IN]

    acc = jnp.broadcast_to(cb_ref[...], (blk, SSD_XBC))
    for i in range(SSD_CONV):
        acc = acc + cw_ref[i:i + 1, :] * xs_ref[pl.ds(tail - (SSD_CONV - 1) + i, blk), :]
    xs_ref[0:tail, :] = xs_ref[blk:blk + tail, :]
    xbc = _silu(acc)
    xh = xbc[:, 0:SSD_WIDTH]
    bm = xbc[:, SSD_WIDTH:SSD_WIDTH + SSD_BC]
    cm = xbc[:, SSD_WIDTH + SSD_BC:SSD_XBC]

    dt = _softplus(dt_raw + dtb_ref[...])
    a_dt = dt * a_ref[...]
    row = _iota((blk, blk), 0)
    col = _iota((blk, blk), 1)
    causal = col <= row
    a_cs = _dot_hi(causal.astype(F32), a_dt)
    a_cs_t = a_cs.T
    a_last = a_cs[blk - 1:blk, :]

    expand = _block_indicator(LANES, SSD_WIDTH, 1, SSD_HEAD_DIM)
    dt_e = _dot_hi(dt, expand)
    dec_e = _dot_hi(jnp.exp(a_cs), expand)
    te_e = _dot_hi(jnp.exp(a_last - a_cs), expand)
    cd_e = _dot_hi(jnp.broadcast_to(jnp.exp(a_last), (SUBLANES, LANES)), expand)[0:1, :]

    x_dt = xh * dt_e
    state = st_ref[...]
    y_off = _dot(cm, state) * dec_e

    lane = _iota((blk, LANES), 1)
    hpg = SSD_HEADS // SSD_GROUPS
    pairs = []
    for j in range(SSD_HEADS // 2):
        grp = (2 * j) // hpg
        in_grp = (lane // SSD_STATE) == grp
        cb = _dot_nt(jnp.where(in_grp, cm, 0.0).astype(BF16), bm.astype(BF16))
        xp = x_dt[:, j * LANES:(j + 1) * LANES]
        ys = []
        for h in (2 * j, 2 * j + 1):
            diff = jnp.broadcast_to(a_cs[:, h:h + 1], (blk, blk)) - a_cs_t[h:h + 1, :]
            seg = jnp.exp(jnp.where(causal, diff, -jnp.inf))
            ys.append(_dot(cb * seg, xp))
        pairs.append(jnp.where(lane < SSD_HEAD_DIM, ys[0], ys[1]))
    y = jnp.concatenate(pairs, axis=1) + y_off + d_ref[...] * xh

    new = _dot_tn(bm.astype(BF16), (x_dt * te_e).astype(BF16))
    keep = _block_indicator(SSD_BC, SSD_WIDTH, SSD_STATE, SSD_WIDTH // SSD_GROUPS)
    st_ref[...] = state * cd_e + keep * new

    y = y * _silu(z)
    gw = SSD_WIDTH // SSD_GROUPS
    for g in range(SSD_GROUPS):
        yg = y[:, g * gw:(g + 1) * gw]
        ms = jnp.mean(yg * yg, -1, keepdims=True)
        o_ref[:, g * gw:(g + 1) * gw] = (yg * lax.rsqrt(ms + RMS_EPS)
                                         * ng_ref[:, g * gw:(g + 1) * gw]).astype(o_ref.dtype)


def _ssd_mixer(p, conv_w, conv_b, dt_bias, a_neg, d_skip, norm_g):
    bsz, seq, _ = p.shape
    blk = SSD_BLOCK
    small = lambda a: pl.BlockSpec(a.shape, lambda b, c: (0, 0))
    args = (conv_w, conv_b, dt_bias, a_neg, d_skip, norm_g)
    return pl.pallas_call(
        _ssd_kernel, grid=(bsz, seq // blk),
        in_specs=[pl.BlockSpec((None, blk, SSD_IN), lambda b, c: (b, c, 0))] + [small(a) for a in args],
        out_specs=pl.BlockSpec((None, blk, SSD_WIDTH), lambda b, c: (b, c, 0)),
        out_shape=jax.ShapeDtypeStruct((bsz, seq, SSD_WIDTH), BF16),
        scratch_shapes=[pltpu.VMEM((blk + 2 * SUBLANES, SSD_XBC), F32),
                        pltpu.VMEM((SSD_BC, SSD_WIDTH), F32)],
        compiler_params=_params("parallel", "arbitrary"), name="ssd_mixer")(p, *args)


def _rwkv_kernel(p_ref, mu_ref, wlr_ref, w0_ref, a0_ref, kk_ref, ka_ref, rk_ref, lng_ref, lnb_ref,
                 o_ref, xs_ref, st_ref):
    nb, blk, _ = o_ref.shape
    rows = nb * blk
    tail = SUBLANES
    width = RWKV_WIDTH

    @pl.when(pl.program_id(0) == 0)
    def _():
        xs_ref[:, 0:tail, :] = jnp.zeros((nb, tail, RWKV_IN), F32)
        st_ref[...] = jnp.zeros(st_ref.shape, F32)

    ps, prevs = [], []
    for b in range(nb):
        ps.append(p_ref[b])
        xs_ref[b, tail:tail + blk, :] = ps[b]
        prevs.append(xs_ref[b, pl.ds(tail - 1, blk), :])
        xs_ref[b, 0:tail, :] = xs_ref[b, blk:blk + tail, :]
    p = jnp.concatenate(ps, axis=0)
    prev = jnp.concatenate(prevs, axis=0)
    pm = p + (prev - p) * mu_ref[...]
    r = pm[:, 0:width]
    k = pm[:, width:2 * width]
    v = pm[:, 2 * width:3 * width]
    lr = pm[:, 3 * width:RWKV_IN]

    lane = _iota((rows, RWKV_LR), 1)
    lr_act = jnp.where(lane < RWKV_DECAY_RANK, jnp.tanh(lr),
                       jnp.where(lane < RWKV_DECAY_RANK + RWKV_ICLR_RANK, lr, _sigmoid(lr)))
    proj = _dot(lr_act, wlr_ref[...])
    w = -_softplus(-(w0_ref[...] + proj[:, 0:width])) - 0.5
    log_decay = -jnp.exp(w)
    a = _sigmoid(a0_ref[...] + proj[:, width:2 * width])
    g = proj[:, 2 * width:3 * width]

    head_sum = _block_indicator(width, width, RWKV_HEAD_DIM, RWKV_HEAD_DIM)
    head_sum_b = head_sum.astype(BF16)
    kk = k * kk_ref[...]
    kk = kk * lax.rsqrt(_dot(kk * kk, head_sum_b) + 1e-12)
    k2 = k * (1.0 + (a - 1.0) * ka_ref[...])
    alpha = -kk
    beta = kk * a

    row = _iota((rows, rows), 0)
    col = _iota((rows, rows), 1)
    same_row = (row // blk) == (col // blk)
    incl = same_row & (col <= row)
    strict = same_row & (col < row)
    cs = _dot_rhs2(incl.astype(BF16), log_decay)
    e_neg = jnp.exp(-cs)
    a_t = alpha * jnp.exp(cs - log_decay)
    b_t = (beta * e_neg).astype(BF16)
    k_t = (k2 * e_neg).astype(BF16)
    r_t = r * jnp.exp(cs)
    v_b = v.astype(BF16)

    states = [st_ref[b] for b in range(nb)]
    reads = []
    for b in range(nb):
        sl = slice(b * blk, (b + 1) * blk)
        reads.append(_dot(jnp.concatenate([a_t[sl], r_t[sl]], axis=0), states[b]))
    a_s = jnp.concatenate([x[0:blk] for x in reads], axis=0)
    r_s = jnp.concatenate([x[blk:2 * blk] for x in reads], axis=0)

    wlane = _iota((rows, width), 1)
    in_head = [(wlane // RWKV_HEAD_DIM) == h for h in range(RWKV_HEADS)]
    parts = []
    for h in range(RWKV_HEADS):
        parts += [jnp.where(in_head[h], a_t, 0.0), jnp.where(in_head[h], r_t, 0.0)]
    lhs = jnp.concatenate(parts, axis=0).astype(BF16)
    pb_all = _dot_nt(lhs, b_t)
    pk_all = _dot_nt(lhs, k_t)

    heads = range(RWKV_HEADS)
    base = [2 * h * rows for h in heads]
    npow = [jnp.where(strict, pb_all[base[h]:base[h] + rows], 0.0).astype(BF16) for h in heads]
    u = [a_s + _dot(jnp.where(strict, pk_all[base[h]:base[h] + rows], 0.0), v_b) for h in heads]
    steps = int(math.log2(blk))
    for i in range(steps):
        u = [u[h] + _dot(npow[h], u[h]) for h in heads]
        if i + 1 < steps:
            npow = [_dot(npow[h], npow[h]).astype(BF16) for h in heads]
    y = [_dot(jnp.where(incl, pb_all[base[h] + rows:base[h] + 2 * rows], 0.0), u[h])
         + _dot(jnp.where(incl, pk_all[base[h] + rows:base[h] + 2 * rows], 0.0), v_b) for h in heads]
    u_all = u[0]
    y_all = y[0]
    for h in range(1, RWKV_HEADS):
        u_all = jnp.where(in_head[h], u[h], u_all)
        y_all = jnp.where(in_head[h], y[h], y_all)
    y_all = y_all + r_s

    for b in range(nb):
        sl = slice(b * blk, (b + 1) * blk)
        c_last = cs[(b + 1) * blk - 1:(b + 1) * blk, :]
        to_end = jnp.exp(c_last - cs[sl])
        new = (_dot_tn((beta[sl] * to_end).astype(BF16), u_all[sl].astype(BF16))
               + _dot_tn((k2[sl] * to_end).astype(BF16), v_b[sl]))
        chunk_decay = jnp.broadcast_to(jnp.exp(c_last), (SUBLANES, width)).T[:, 0:1]
        st_ref[b] = states[b] * chunk_decay + head_sum * new

    head_mean_b = (head_sum * (1.0 / RWKV_HEAD_DIM)).astype(BF16)
    mean = _dot_lhs2(y_all, head_mean_b)
    yc = y_all - mean
    var = _dot(yc * yc, head_mean_b)
    yn = yc * lax.rsqrt(var + RWKV_GN_EPS) * lng_ref[...] + lnb_ref[...]
    bonus = _dot(r * k2 * rk_ref[...], head_sum_b) * v
    out = ((yn + bonus) * g).astype(o_ref.dtype)
    for b in range(nb):
        o_ref[b] = out[b * blk:(b + 1) * blk]


def _rwkv_mixer(p, mu, wlr, w0, a0, k_k, k_a, r_k, ln_g, ln_b):
    bsz, seq, _ = p.shape
    blk = RWKV_BLOCK
    small = lambda a: pl.BlockSpec(a.shape, lambda c: (0, 0))
    args = (mu, wlr, w0, a0, k_k, k_a, r_k, ln_g, ln_b)
    return pl.pallas_call(
        _rwkv_kernel, grid=(seq // blk,),
        in_specs=[pl.BlockSpec((bsz, blk, RWKV_IN), lambda c: (0, c, 0))] + [small(a) for a in args],
        out_specs=pl.BlockSpec((bsz, blk, RWKV_WIDTH), lambda c: (0, c, 0)),
        out_shape=jax.ShapeDtypeStruct((bsz, seq, RWKV_WIDTH), BF16),
        scratch_shapes=[pltpu.VMEM((bsz, blk + 2 * SUBLANES, RWKV_IN), F32),
                        pltpu.VMEM((bsz, RWKV_WIDTH, RWKV_WIDTH), F32)],
        compiler_params=_params("arbitrary"), name="rwkv7_mixer")(p, *args)


def _gla_kernel(p_ref, wa_ref, ba_ref, ng_ref, o_ref, st_ref, z_ref, acc_ref):
    blk = o_ref.shape[0]
    qk = GLA_QK_WIDTH
    vw = GLA_WIDTH

    @pl.when(pl.program_id(1) == 0)
    def _():
        st_ref[...] = jnp.zeros(st_ref.shape, F32)

    q = p_ref[:, 0:qk] * (GLA_KEY_DIM ** -0.5)
    k = p_ref[:, qk:2 * qk]
    g = p_ref[:, 2 * qk + vw:2 * qk + 2 * vw]
    a_lr = p_ref[:, 2 * qk + 2 * vw:GLA_IN]
    v_off = 2 * qk

    logit = _dot_hi(a_lr, wa_ref[...]) + ba_ref[...]
    log_a = -_softplus(-logit) / GLA_GATE_TEMP
    row = _iota((blk, blk), 0)
    col = _iota((blk, blk), 1)
    cum = _dot_hi((col <= row).astype(F32), log_a)

    state = st_ref[...]
    acc_ref[...] = _dot_nt((q * jnp.exp(cum)).astype(BF16), state.astype(BF16))

    spread = _block_indicator(qk, vw, GLA_KEY_DIM, GLA_VALUE_DIM).astype(BF16)
    rows = _iota((blk, qk), 0)
    group = 2 * SUBLANES
    for jb in range(blk // group):
        r0 = jb * group
        n = blk - r0
        for jj in range(group):
            j = r0 + jj
            decay = jnp.exp(jnp.where(rows[r0:] >= j, cum[r0:] - cum[j:j + 1, :], -jnp.inf))
            z_ref[jj * n:(jj + 1) * n, :] = (q[r0:] * decay * k[j:j + 1, :]).astype(BF16)
        res = jnp.dot(z_ref[0:group * n, :], spread, preferred_element_type=F32)
        part = jnp.zeros((n, vw), F32)
        for jj in range(group):
            j = r0 + jj
            part = part + res[jj * n:(jj + 1) * n] * p_ref[j:j + 1, v_off:v_off + vw]
        acc_ref[r0:blk, :] = acc_ref[r0:blk, :] + part

    v = p_ref[:, v_off:v_off + vw]
    last = cum[blk - 1:blk, :]
    k_dec = k * jnp.exp(last - cum)
    keep = _block_indicator(vw, qk, GLA_VALUE_DIM, GLA_KEY_DIM)
    st_ref[...] = state * jnp.exp(last) + keep * _dot_tn(v.astype(BF16), k_dec.astype(BF16))

    o = acc_ref[...]
    head_mean = _block_indicator(vw, vw, GLA_VALUE_DIM, GLA_VALUE_DIM) * (1.0 / GLA_VALUE_DIM)
    ms = _dot_hi(o * o, head_mean)
    o_ref[...] = (o * lax.rsqrt(ms + RMS_EPS) * ng_ref[...] * _silu(g)).astype(o_ref.dtype)


def _gla_mixer(p, w_a2, b_a, norm_g):
    bsz, seq, _ = p.shape
    blk = GLA_BLOCK
    small = lambda a: pl.BlockSpec(a.shape, lambda b, c: (0, 0))
    args = (w_a2, b_a, norm_g)
    return pl.pallas_call(
        _gla_kernel, grid=(bsz, seq // blk),
        in_specs=[pl.BlockSpec((None, blk, GLA_IN), lambda b, c: (b, c, 0))] + [small(a) for a in args],
        out_specs=pl.BlockSpec((None, blk, GLA_WIDTH), lambda b, c: (b, c, 0)),
        out_shape=jax.ShapeDtypeStruct((bsz, seq, GLA_WIDTH), BF16),
        scratch_shapes=[pltpu.VMEM((GLA_WIDTH, GLA_QK_WIDTH), F32),
                        pltpu.VMEM((2 * SUBLANES * blk, GLA_QK_WIDTH), BF16),
                        pltpu.VMEM((blk, GLA_WIDTH), F32)],
        compiler_params=_params("parallel", "arbitrary"), name="gla_mixer")(p, *args)


def _outproj_kernel(ys_ref, yr_ref, yg_ref, h_ref, w1_ref, w2_ref, w3_ref, g_ref, b_ref, wr_ref, br_ref,
                    h1_ref, eid_ref, gate_ref):
    mix = (jnp.dot(ys_ref[...], w1_ref[...], preferred_element_type=F32)
           + jnp.dot(yr_ref[...], w2_ref[...], preferred_element_type=F32)
           + jnp.dot(yg_ref[...], w3_ref[...], preferred_element_type=F32))
    h1 = _layer_norm(DEEPNORM_ALPHA * h_ref[...] + mix, g_ref[...], b_ref[...])
    h1_ref[...] = h1

    logits = _dot_hi(h1, wr_ref[...]) + br_ref[...]
    tm = logits.shape[0]
    lane = _iota((tm, LANES), 1)
    lane_f = lane.astype(F32)

    def masked_softmax(mask):
        m = jnp.max(jnp.where(mask, logits, -jnp.inf), -1, keepdims=True)
        e = jnp.where(mask, jnp.exp(logits - m), 0.0)
        return e / jnp.sum(e, -1, keepdims=True)

    def first_argmax(vals, mask):
        m = jnp.max(jnp.where(mask, vals, -jnp.inf), -1, keepdims=True)
        idx = jnp.min(jnp.where(mask & (vals == m), lane_f, float(LANES)), -1, keepdims=True)
        return m, idx.astype(jnp.int32)

    is_group = lane < N_GROUPS
    g_prob = masked_softmax(is_group)
    g_w, g_idx = first_argmax(g_prob, is_group)
    e_lo = N_GROUPS + g_idx * EXPERTS_PER_GROUP
    in_group = (lane >= e_lo) & (lane < e_lo + EXPERTS_PER_GROUP)
    e_prob = masked_softmax(in_group)
    p1, i1 = first_argmax(e_prob, in_group)
    p2, i2 = first_argmax(e_prob, in_group & (lane != i1))
    denom = p1 + p2
    eid_ref[...] = jnp.where(lane == 0, i1 - N_GROUPS, jnp.where(lane == 1, i2 - N_GROUPS, 0))
    gate_ref[...] = jnp.where(lane == 0, g_w * (p1 / denom), jnp.where(lane == 1, g_w * (p2 / denom), 0.0))


def _out_projection(y_ssd, y_rwkv, y_gla, h, w_out, ln_g, ln_b, w_r, b_r):
    n_tok, d = h.shape
    tm = TOKEN_TILE
    row = lambda n: pl.BlockSpec((tm, n), lambda i: (i, 0))
    full = lambda a: pl.BlockSpec(a.shape, lambda i: (0, 0))
    w1 = w_out[0:SSD_WIDTH]
    w2 = w_out[SSD_WIDTH:SSD_WIDTH + RWKV_WIDTH]
    w3 = w_out[SSD_WIDTH + RWKV_WIDTH:MIX_WIDTH]
    consts = (w1, w2, w3, ln_g, ln_b, w_r, b_r)
    return pl.pallas_call(
        _outproj_kernel, grid=(n_tok // tm,),
        in_specs=[row(SSD_WIDTH), row(RWKV_WIDTH), row(GLA_WIDTH), row(d)] + [full(a) for a in consts],
        out_specs=[row(d), row(LANES), row(LANES)],
        out_shape=[jax.ShapeDtypeStruct((n_tok, d), F32),
                   jax.ShapeDtypeStruct((n_tok, LANES), jnp.int32),
                   jax.ShapeDtypeStruct((n_tok, LANES), F32)],
        compiler_params=_params("parallel"), name="out_projection")(y_ssd, y_rwkv, y_gla, h, *consts)


def _dest_kernel(eid_ref, dest_ref, cnt_ref, run_ref, base_ref):
    phase = pl.program_id(0)
    step = pl.program_id(1)
    tm = eid_ref.shape[0]
    e = eid_ref[...]
    lane = _iota((tm, LANES), 1)
    oh0 = (lane == e[:, 0:1]).astype(F32)
    oh1 = (lane == e[:, 1:2]).astype(F32)
    oh = oh0 + oh1
    tile_count = jnp.sum(oh, 0, keepdims=True)

    @pl.when(step == 0)
    def _():
        run_ref[...] = jnp.zeros(run_ref.shape, F32)

    @pl.when((phase == 1) & (step == 0))
    def _():
        blocks = jnp.ceil(cnt_ref[...] * (1.0 / ROW_BLOCK))
        before = (_iota((LANES, LANES), 0) < _iota((LANES, LANES), 1)).astype(BF16)
        start = jnp.dot(jnp.broadcast_to(blocks, (SUBLANES, LANES)).astype(BF16), before,
                        preferred_element_type=F32)[0:1, :]
        base_ref[...] = start * ROW_BLOCK

    @pl.when(phase == 0)
    def _():
        total = run_ref[...] + tile_count
        run_ref[...] = total
        cnt_ref[...] = total

    @pl.when(phase == 1)
    def _():
        earlier = (_iota((tm, tm), 1) < _iota((tm, tm), 0)).astype(BF16)
        rank = jnp.dot(earlier, oh.astype(BF16), preferred_element_type=F32)
        pos = rank + run_ref[...] + base_ref[...]
        d0 = jnp.sum(oh0 * pos, -1, keepdims=True)
        d1 = jnp.sum(oh1 * pos, -1, keepdims=True)
        dest_ref[...] = jnp.where(lane == 0, d0, jnp.where(lane == 1, d1, 0.0)).astype(jnp.int32)
        run_ref[...] = run_ref[...] + tile_count


def _dispatch_plan(eid):
    n_tok = eid.shape[0]
    tm = TOKEN_TILE
    return pl.pallas_call(
        _dest_kernel, grid=(2, n_tok // tm),
        in_specs=[pl.BlockSpec((tm, LANES), lambda ph, i: (i, 0))],
        out_specs=[pl.BlockSpec((tm, LANES), lambda ph, i: (i * ph, 0)),
                   pl.BlockSpec((1, LANES), lambda ph, i: (0, 0))],
        out_shape=[jax.ShapeDtypeStruct((n_tok, LANES), jnp.int32),
                   jax.ShapeDtypeStruct((1, LANES), F32)],
        scratch_shapes=[pltpu.VMEM((1, LANES), F32), pltpu.VMEM((1, LANES), F32)],
        compiler_params=_params("arbitrary", "arbitrary"), name="dispatch_plan")(eid)


def _row_copy(src_ref, src_row, dst_ref, dst_row, sem):
    return pltpu.make_async_copy(src_ref.at[pl.ds(src_row, 1)], dst_ref.at[pl.ds(dst_row, 1)], sem)


def _scatter_kernel(dest_ref, h_ref, xs_in_ref, xs_ref, sem):
    del xs_in_ref
    tm = h_ref.shape[0]

    def start(t, carry):
        for k in range(TOP_K):
            _row_copy(h_ref, t, xs_ref, dest_ref[0, 0, TOP_K * t + k], sem).start()
        return carry

    def wait(t, carry):
        for k in range(TOP_K):
            _row_copy(h_ref, t, xs_ref, dest_ref[0, 0, TOP_K * t + k], sem).wait()
        return carry

    lax.fori_loop(0, tm, start, 0)
    lax.fori_loop(0, tm, wait, 0)


def _scatter_rows(dest3, h1, n_rows):
    n_tok, d = h1.shape
    n_tiles = dest3.shape[0]
    xs0 = jnp.zeros((n_rows, d), h1.dtype)
    return pl.pallas_call(
        _scatter_kernel, grid=(n_tiles,),
        in_specs=[pl.BlockSpec((1, 1, dest3.shape[2]), lambda i: (i, 0, 0), memory_space=pltpu.SMEM),
                  pl.BlockSpec((n_tok // n_tiles, d), lambda i: (i, 0)), pl.BlockSpec(memory_space=pl.ANY)],
        out_specs=pl.BlockSpec(memory_space=pl.ANY),
        out_shape=jax.ShapeDtypeStruct((n_rows, d), h1.dtype),
        scratch_shapes=[pltpu.SemaphoreType.DMA(())],
        input_output_aliases={2: 0},
        compiler_params=_params("arbitrary"), name="expert_scatter")(dest3, h1, xs0)


def _expert_kernel(be_ref, x_ref, wg_ref, wu_ref, wd_ref, y_ref):
    del be_ref
    x = x_ref[...].astype(BF16)
    gate = jnp.dot(x, wg_ref[...].astype(BF16), preferred_element_type=F32)
    up = jnp.dot(x, wu_ref[...].astype(BF16), preferred_element_type=F32)
    mid = (_silu(gate) * up).astype(BF16)
    y_ref[...] = jnp.dot(mid, wd_ref[...].astype(BF16), preferred_element_type=F32)


def _expert_mlp(block_expert, xs, w_gate, w_up, w_down, layer):
    n_rows, d = xs.shape
    n_blocks = n_rows // ROW_BLOCK
    ff = w_gate.shape[-1]
    return pl.pallas_call(
        _expert_kernel,
        grid_spec=pltpu.PrefetchScalarGridSpec(
            num_scalar_prefetch=1, grid=(n_blocks,),
            in_specs=[pl.BlockSpec((ROW_BLOCK, d), lambda b, be: (b, 0)),
                      pl.BlockSpec((None, None, d, ff), lambda b, be: (layer, be[b], 0, 0)),
                      pl.BlockSpec((None, None, d, ff), lambda b, be: (layer, be[b], 0, 0)),
                      pl.BlockSpec((None, None, ff, d), lambda b, be: (layer, be[b], 0, 0))],
            out_specs=pl.BlockSpec((ROW_BLOCK, d), lambda b, be: (b, 0))),
        out_shape=jax.ShapeDtypeStruct((n_rows, d), F32),
        compiler_params=_params("arbitrary"), name="expert_mlp")(block_expert, xs, w_gate, w_up, w_down)


def _combine_kernel(dest_ref, y_ref, h_ref, gate_ref, g_ref, b_ref, o_ref, buf_ref, sem):
    tm = o_ref.shape[0]

    def start(t, carry):
        for k in range(TOP_K):
            _row_copy(y_ref, dest_ref[0, 0, TOP_K * t + k], buf_ref.at[k], t, sem).start()
        return carry

    def wait(t, carry):
        for k in range(TOP_K):
            _row_copy(y_ref, dest_ref[0, 0, TOP_K * t + k], buf_ref.at[k], t, sem).wait()
        return carry

    lax.fori_loop(0, tm, start, 0)
    lax.fori_loop(0, tm, wait, 0)
    gate = gate_ref[...]
    ffn = gate[:, 0:1] * buf_ref[0] + gate[:, 1:2] * buf_ref[1]
    o_ref[...] = _layer_norm(DEEPNORM_ALPHA * h_ref[...] + ffn, g_ref[...], b_ref[...])


def _combine(dest3, ys, h1, gate, ln_g, ln_b):
    n_tok, d = h1.shape
    tm = TOKEN_TILE
    row = lambda n: pl.BlockSpec((tm, n), lambda i: (i, 0))
    vec = pl.BlockSpec((1, d), lambda i: (0, 0))
    return pl.pallas_call(
        _combine_kernel, grid=(n_tok // tm,),
        in_specs=[pl.BlockSpec((1, 1, dest3.shape[2]), lambda i: (i, 0, 0), memory_space=pltpu.SMEM),
                  pl.BlockSpec(memory_space=pl.ANY), row(d), row(LANES), vec, vec],
        out_specs=row(d),
        out_shape=jax.ShapeDtypeStruct((n_tok, d), F32),
        scratch_shapes=[pltpu.VMEM((TOP_K, tm, d), F32), pltpu.SemaphoreType.DMA(())],
        compiler_params=_params("arbitrary"), name="expert_combine")(dest3, ys, h1, gate, ln_g, ln_b)


def _pad_lanes(a, n):
    return jnp.pad(a, [(0, 0)] * (a.ndim - 1) + [(0, n - a.shape[-1])])


def _row(a, n=None):
    a = a.reshape(1, -1).astype(F32)
    return a if n is None else _pad_lanes(a, n)


def _mixer_layer(h, w_in, ssd_conv_w, ssd_conv_b, ssd_dt_bias, ssd_a_log, ssd_d, ssd_norm_g,
                 rwkv_mu, rwkv_w0, rwkv_w2, rwkv_a0, rwkv_a2, rwkv_g2, rwkv_k_k, rwkv_k_a, rwkv_r_k,
                 rwkv_ln_g, rwkv_ln_b, gla_w_a2, gla_b_a, gla_norm_g, bsz, seq):
    o = 0
    cols = {}
    for name, n in (("z", SSD_WIDTH), ("xbc", SSD_XBC), ("dt", SSD_HEADS), ("rwkv", RWKV_IN),
                    ("q", GLA_QK_WIDTH), ("k", GLA_QK_WIDTH), ("v", GLA_WIDTH), ("g", GLA_WIDTH),
                    ("ga", GLA_GATE_RANK)):
        cols[name] = w_in[:, o:o + n]
        o += n
    w_ssd = jnp.concatenate([cols["z"], cols["xbc"], _pad_lanes(cols["dt"], LANES)], 1).astype(BF16)
    w_rwkv = cols["rwkv"].astype(BF16)
    w_gla = jnp.concatenate([cols["q"], cols["k"], cols["v"], cols["g"], _pad_lanes(cols["ga"], LANES)],
                            1).astype(BF16)
    p_ssd, p_rwkv, p_gla = _in_projection(h, w_ssd, w_rwkv, w_gla)

    y_ssd = _ssd_mixer(
        p_ssd.reshape(bsz, seq, SSD_IN), ssd_conv_w.astype(F32), _row(ssd_conv_b), _row(ssd_dt_bias, LANES),
        _row(-jnp.exp(ssd_a_log.astype(F32)), LANES), _row(jnp.repeat(ssd_d.astype(F32), SSD_HEAD_DIM)),
        _row(ssd_norm_g))

    w_lr = jnp.zeros((RWKV_LR, 3 * RWKV_WIDTH), F32)
    w_lr = w_lr.at[0:RWKV_DECAY_RANK, 0:RWKV_WIDTH].set(rwkv_w2)
    w_lr = w_lr.at[RWKV_DECAY_RANK:RWKV_DECAY_RANK + RWKV_ICLR_RANK, RWKV_WIDTH:2 * RWKV_WIDTH].set(rwkv_a2)
    w_lr = w_lr.at[RWKV_DECAY_RANK + RWKV_ICLR_RANK:, 2 * RWKV_WIDTH:].set(rwkv_g2)
    y_rwkv = _rwkv_mixer(
        p_rwkv.reshape(bsz, seq, RWKV_IN), _row(rwkv_mu), w_lr, _row(rwkv_w0), _row(rwkv_a0), _row(rwkv_k_k),
        _row(rwkv_k_a), _row(rwkv_r_k), _row(rwkv_ln_g), _row(rwkv_ln_b))

    w_a2 = jnp.zeros((LANES, GLA_QK_WIDTH), F32).at[0:GLA_GATE_RANK].set(gla_w_a2)
    y_gla = _gla_mixer(p_gla.reshape(bsz, seq, GLA_IN), w_a2, _row(gla_b_a), _row(gla_norm_g))

    n_tok = bsz * seq
    return (y_ssd.reshape(n_tok, SSD_WIDTH), y_rwkv.reshape(n_tok, RWKV_WIDTH),
            y_gla.reshape(n_tok, GLA_WIDTH))


def _moe_layer(h1, eid, gate, layer, w_gate, w_up, w_down, ln_g, ln_b):
    n_tok = h1.shape[0]
    tm = TOKEN_TILE
    n_blocks = n_tok * TOP_K // ROW_BLOCK + N_EXPERTS
    dest, counts = _dispatch_plan(eid)
    blocks = jnp.ceil(counts[0, 0:N_EXPERTS] / ROW_BLOCK).astype(jnp.int32)
    block_end = jnp.cumsum(blocks)
    block_expert = jnp.minimum(
        jnp.searchsorted(block_end, jnp.arange(n_blocks, dtype=jnp.int32), side="right"),
        N_EXPERTS - 1).astype(jnp.int32)
    dest3 = dest[:, 0:TOP_K].reshape(n_tok // tm, 1, tm * TOP_K)
    xs = _scatter_rows(dest3, h1, n_blocks * ROW_BLOCK)
    ys = _expert_mlp(block_expert, xs, w_gate, w_up, w_down, layer)
    return _combine(dest3, ys, h1, gate, ln_g, ln_b)


def kernel(x, ln_in_g, ln_in_b, w_in, ssd_conv_w, ssd_conv_b, ssd_dt_bias, ssd_a_log, ssd_d, ssd_norm_g, rwkv_mu, rwkv_w0, rwkv_w2, rwkv_a0, rwkv_a2, rwkv_g2, rwkv_k_k, rwkv_k_a, rwkv_r_k, rwkv_ln_g, rwkv_ln_b, gla_w_a2, gla_b_a, gla_norm_g, w_out, ln1_g, ln1_b, moe_w_rg, moe_b_rg, moe_w_re, moe_b_re, moe_w_gate, moe_w_up, moe_w_down, ln2_g, ln2_b):
    bsz, seq, d = x.shape
    n_tok = bsz * seq
    h = _input_ln(x.reshape(n_tok, d), ln_in_g, ln_in_b)
    for i in range(w_in.shape[0]):
        y_ssd, y_rwkv, y_gla = _mixer_layer(
            h, w_in[i], ssd_conv_w[i], ssd_conv_b[i], ssd_dt_bias[i], ssd_a_log[i], ssd_d[i], ssd_norm_g[i],
            rwkv_mu[i], rwkv_w0[i], rwkv_w2[i], rwkv_a0[i], rwkv_a2[i], rwkv_g2[i], rwkv_k_k[i],
            rwkv_k_a[i], rwkv_r_k[i], rwkv_ln_g[i], rwkv_ln_b[i], gla_w_a2[i], gla_b_a[i], gla_norm_g[i],
            bsz, seq)
        w_r = _pad_lanes(jnp.concatenate([moe_w_rg[i], moe_w_re[i]], 1).astype(F32), LANES)
        b_r = _row(jnp.concatenate([moe_b_rg[i], moe_b_re[i]]), LANES)
        h1, eid, gate = _out_projection(
            y_ssd, y_rwkv, y_gla, h, w_out[i].astype(BF16), _row(ln1_g[i]), _row(ln1_b[i]), w_r, b_r)
        h = _moe_layer(h1, eid, gate, i, moe_w_gate, moe_w_up, moe_w_down, _row(ln2_g[i]), _row(ln2_b[i]))
    return h.reshape(bsz, seq, d)
```

```python
import functools
import math

import jax
import jax.numpy as jnp
from jax import lax
from jax.experimental import pallas as pl
from jax.experimental.pallas import tpu as pltpu

F32 = jnp.float32
BF16 = jnp.bfloat16
HI = lax.Precision.HIGHEST

D_MODEL = 1024
DEPTH = 2

SSD_HEAD_DIM = 64
SSD_WIDTH = D_MODEL // 2
SSD_HEADS = SSD_WIDTH // SSD_HEAD_DIM
SSD_GROUPS = 2
SSD_STATE = 64
SSD_CONV = 4
SSD_BC = SSD_GROUPS * SSD_STATE
SSD_XBC = SSD_WIDTH + 2 * SSD_BC

RWKV_HEAD_DIM = 64
RWKV_WIDTH = D_MODEL // 4
RWKV_HEADS = RWKV_WIDTH // RWKV_HEAD_DIM
RWKV_DECAY_RANK = 32
RWKV_ICLR_RANK = 32
RWKV_GATE_RANK = 64
RWKV_LR = RWKV_DECAY_RANK + RWKV_ICLR_RANK + RWKV_GATE_RANK
RWKV_IN = 3 * RWKV_WIDTH + RWKV_LR
RWKV_GN_EPS = 64e-5

GLA_VALUE_DIM = 64
GLA_WIDTH = D_MODEL // 4
GLA_HEADS = GLA_WIDTH // GLA_VALUE_DIM
GLA_KEY_DIM = GLA_VALUE_DIM // 2
GLA_QK_WIDTH = GLA_HEADS * GLA_KEY_DIM
GLA_GATE_RANK = 16
GLA_GATE_TEMP = 16.0

MIX_WIDTH = SSD_WIDTH + RWKV_WIDTH + GLA_WIDTH

N_GROUPS = 4
EXPERTS_PER_GROUP = 8
N_EXPERTS = N_GROUPS * EXPERTS_PER_GROUP
TOP_K = 2
EXPERT_FF = 512

DEEPNORM_ALPHA = (2 * DEPTH) ** 0.25
LN_EPS = 1e-5
RMS_EPS = 1e-6

LANES = 128
SUBLANES = 8
SSD_IN = SSD_WIDTH + SSD_XBC + LANES
GLA_IN = 2 * GLA_QK_WIDTH + 2 * GLA_WIDTH + LANES

TOKEN_TILE = 256
SSD_BLOCK = 128
RWKV_BLOCK = 64
GLA_BLOCK = 64
ROW_BLOCK = 256
DMA_UNROLL = 8
VMEM_LIMIT = 48 * 1024 * 1024


def _dot(a, b):
    return jnp.dot(a.astype(BF16), b.astype(BF16), preferred_element_type=F32)


def _dot_hi(a, b):
    return jnp.dot(a, b, precision=HI, preferred_element_type=F32)


def _split_bf16(x):
    hi = x.astype(BF16)
    return hi, (x - hi.astype(F32)).astype(BF16)


def _dot_rhs2(a_exact, b):
    b_hi, b_lo = _split_bf16(b)
    return (jnp.dot(a_exact, b_hi, preferred_element_type=F32)
            + jnp.dot(a_exact, b_lo, preferred_element_type=F32))


def _dot_lhs2(a, b_exact):
    a_hi, a_lo = _split_bf16(a)
    return (jnp.dot(a_hi, b_exact, preferred_element_type=F32)
            + jnp.dot(a_lo, b_exact, preferred_element_type=F32))


def _dot_nt(a, b, precision=None):
    return lax.dot_general(a, b, (((1,), (1,)), ((), ())), precision=precision,
                           preferred_element_type=F32)


def _dot_tn(a, b, precision=None):
    return lax.dot_general(a, b, (((0,), (0,)), ((), ())), precision=precision,
                           preferred_element_type=F32)


def _sigmoid(x):
    return 1.0 / (1.0 + jnp.exp(-x))


def _silu(x):
    return x * _sigmoid(x)


def _softplus(x):
    return jnp.maximum(x, 0.0) + jnp.log1p(jnp.exp(-jnp.abs(x)))


def _iota(shape, dim):
    return lax.broadcasted_iota(jnp.int32, shape, dim)


def _block_indicator(rows, cols, row_seg, col_seg):
    r = _iota((rows, cols), 0) // row_seg
    c = _iota((rows, cols), 1) // col_seg
    return (r == c).astype(F32)


def _layer_norm(x, g, b):
    mu = jnp.mean(x, -1, keepdims=True)
    xc = x - mu
    var = jnp.mean(xc * xc, -1, keepdims=True)
    return xc * lax.rsqrt(var + LN_EPS) * g + b


def _params(*sem):
    return pltpu.CompilerParams(dimension_semantics=sem, vmem_limit_bytes=VMEM_LIMIT)


def _ln_kernel(x_ref, g_ref, b_ref, o_ref):
    o_ref[...] = _layer_norm(x_ref[...], g_ref[...], b_ref[...])


def _input_ln(x, g, b):
    n_tok, d = x.shape
    tm = TOKEN_TILE
    row = pl.BlockSpec((tm, d), lambda i: (i, 0))
    vec = pl.BlockSpec((1, d), lambda i: (0, 0))
    return pl.pallas_call(
        _ln_kernel, grid=(n_tok // tm,), in_specs=[row, vec, vec], out_specs=row,
        out_shape=jax.ShapeDtypeStruct((n_tok, d), F32), compiler_params=_params("parallel"),
        name="input_ln")(x, g.reshape(1, d), b.reshape(1, d))


def _inproj_kernel(h_ref, w1_ref, w2_ref, w3_ref, o1_ref, o2_ref, o3_ref):
    hb = h_ref[...].astype(BF16)
    o1_ref[...] = jnp.dot(hb, w1_ref[...], preferred_element_type=F32)
    o2_ref[...] = jnp.dot(hb, w2_ref[...], preferred_element_type=F32)
    o3_ref[...] = jnp.dot(hb, w3_ref[...], preferred_element_type=F32)


def _in_projection(h, w_ssd, w_rwkv, w_gla):
    n_tok, d = h.shape
    tm = TOKEN_TILE
    ws = (w_ssd, w_rwkv, w_gla)
    return pl.pallas_call(
        _inproj_kernel, grid=(n_tok // tm,),
        in_specs=[pl.BlockSpec((tm, d), lambda i: (i, 0))]
        + [pl.BlockSpec(w.shape, lambda i: (0, 0)) for w in ws],
        out_specs=[pl.BlockSpec((tm, w.shape[1]), lambda i: (i, 0)) for w in ws],
        out_shape=[jax.ShapeDtypeStruct((n_tok, w.shape[1]), F32) for w in ws],
        compiler_params=_params("parallel"), name="in_projection")(h, *ws)


def _ssd_kernel(p_ref, cw_ref, cb_ref, dtb_ref, a_ref, d_ref, ng_ref, o_ref, xs_ref, st_ref):
    blk = o_ref.shape[0]
    tail = SUBLANES

    @pl.when(pl.program_id(1) == 0)
    def _():
        xs_ref[0:tail, :] = jnp.zeros((tail, SSD_XBC), F32)
        st_ref[...] = jnp.zeros(st_ref.shape, F32)

    z = p_ref[:, 0:SSD_WIDTH]
    xs_ref[tail:tail + blk, :] = p_ref[:, SSD_WIDTH:SSD_WIDTH + SSD_XBC]
    dt_raw = p_ref[:, SSD_WIDTH + SSD_XBC:SSD_IN]

    acc = jnp.broadcast_to(cb_ref[...], (blk, SSD_XBC))
    for i in range(SSD_CONV):
        acc = acc + cw_ref[i:i + 1, :] * xs_ref[pl.ds(tail - (SSD_CONV - 1) + i, blk), :]
    xs_ref[0:tail, :] = xs_ref[blk:blk + tail, :]
    xbc = _silu(acc)
    xh = xbc[:, 0:SSD_WIDTH]
    bm = xbc[:, SSD_WIDTH:SSD_WIDTH + SSD_BC]
    cm = xbc[:, SSD_WIDTH + SSD_BC:SSD_XBC]

    dt = _softplus(dt_raw + dtb_ref[...])
    a_dt = dt * a_ref[...]
    row = _iota((blk, blk), 0)
    col = _iota((blk, blk), 1)
    causal = col <= row
    a_cs = _dot_hi(causal.astype(F32), a_dt)
    a_cs_t = a_cs.T
    a_last = a_cs[blk - 1:blk, :]

    expand = _block_indicator(LANES, SSD_WIDTH, 1, SSD_HEAD_DIM)
    dt_e = _dot_hi(dt, expand)
    dec_e = _dot_hi(jnp.exp(a_cs), expand)
    te_e = _dot_hi(jnp.exp(a_last - a_cs), expand)
    cd_e = _dot_hi(jnp.broadcast_to(jnp.exp(a_last), (SUBLANES, LANES)), expand)[0:1, :]

    x_dt = xh * dt_e
    state = st_ref[...]
    y_off = _dot(cm, state) * dec_e

    lane = _iota((blk, LANES), 1)
    hpg = SSD_HEADS // SSD_GROUPS
    pairs = []
    for j in range(SSD_HEADS // 2):
        grp = (2 * j) // hpg
        in_grp = (lane // SSD_STATE) == grp
        cb = _dot_nt(jnp.where(in_grp, cm, 0.0).astype(BF16), bm.astype(BF16))
        xp = x_dt[:, j * LANES:(j + 1) * LANES]
        ys = []
        for h in (2 * j, 2 * j + 1):
            diff = jnp.broadcast_to(a_cs[:, h:h + 1], (blk, blk)) - a_cs_t[h:h + 1, :]
            seg = jnp.exp(jnp.where(causal, diff, -jnp.inf))
            ys.append(_dot(cb * seg, xp))
        pairs.append(jnp.where(lane < SSD_HEAD_DIM, ys[0], ys[1]))
    y = jnp.concatenate(pairs, axis=1) + y_off + d_ref[...] * xh

    new = _dot_tn(bm.astype(BF16), (x_dt * te_e).astype(BF16))
    keep = _block_indicator(SSD_BC, SSD_WIDTH, SSD_STATE, SSD_WIDTH // SSD_GROUPS)
    st_ref[...] = state * cd_e + keep * new

    y = y * _silu(z)
    gw = SSD_WIDTH // SSD_GROUPS
    for g in range(SSD_GROUPS):
        yg = y[:, g * gw:(g + 1) * gw]
        ms = jnp.mean(yg * yg, -1, keepdims=True)
        o_ref[:, g * gw:(g + 1) * gw] = (yg * lax.rsqrt(ms + RMS_EPS)
                                         * ng_ref[:, g * gw:(g + 1) * gw]).astype(o_ref.dtype)


def _ssd_mixer(p, conv_w, conv_b, dt_bias, a_neg, d_skip, norm_g):
    bsz, seq, _ = p.shape
    blk = SSD_BLOCK
    small = lambda a: pl.BlockSpec(a.shape, lambda b, c: (0, 0))
    args = (conv_w, conv_b, dt_bias, a_neg, d_skip, norm_g)
    return pl.pallas_call(
        _ssd_kernel, grid=(bsz, seq // blk),
        in_specs=[pl.BlockSpec((None, blk, SSD_IN), lambda b, c: (b, c, 0))] + [small(a) for a in args],
        out_specs=pl.BlockSpec((None, blk, SSD_WIDTH), lambda b, c: (b, c, 0)),
        out_shape=jax.ShapeDtypeStruct((bsz, seq, SSD_WIDTH), BF16),
        scratch_shapes=[pltpu.VMEM((blk + 2 * SUBLANES, SSD_XBC), F32),
                        pltpu.VMEM((SSD_BC, SSD_WIDTH), F32)],
        compiler_params=_params("parallel", "arbitrary"), name="ssd_mixer")(p, *args)


def _rwkv_kernel(p_ref, mu_ref, wlr_ref, w0_ref, a0_ref, kk_ref, ka_ref, rk_ref, lng_ref, lnb_ref,
                 o_ref, xs_ref, st_ref):
    nb, blk, _ = o_ref.shape
    rows = nb * blk
    tail = SUBLANES
    width = RWKV_WIDTH

    @pl.when(pl.program_id(0) == 0)
    def _():
        xs_ref[:, 0:tail, :] = jnp.zeros((nb, tail, RWKV_IN), F32)
        st_ref[...] = jnp.zeros(st_ref.shape, F32)

    ps, prevs = [], []
    for b in range(nb):
        ps.append(p_ref[b])
        xs_ref[b, tail:tail + blk, :] = ps[b]
        prevs.append(xs_ref[b, pl.ds(tail - 1, blk), :])
        xs_ref[b, 0:tail, :] = xs_ref[b, blk:blk + tail, :]
    p = jnp.concatenate(ps, axis=0)
    prev = jnp.concatenate(prevs, axis=0)
    pm = p + (prev - p) * mu_ref[...]
    r = pm[:, 0:width]
    k = pm[:, width:2 * width]
    v = pm[:, 2 * width:3 * width]
    lr = pm[:, 3 * width:RWKV_IN]

    lane = _iota((rows, RWKV_LR), 1)
    lr_act = jnp.where(lane < RWKV_DECAY_RANK, jnp.tanh(lr),
                       jnp.where(lane < RWKV_DECAY_RANK + RWKV_ICLR_RANK, lr, _sigmoid(lr)))
    proj = _dot(lr_act, wlr_ref[...])
    w = -_softplus(-(w0_ref[...] + proj[:, 0:width])) - 0.5
    log_decay = -jnp.exp(w)
    a = _sigmoid(a0_ref[...] + proj[:, width:2 * width])
    g = proj[:, 2 * width:3 * width]

    head_sum = _block_indicator(width, width, RWKV_HEAD_DIM, RWKV_HEAD_DIM)
    head_sum_b = head_sum.astype(BF16)
    kk = k * kk_ref[...]
    kk = kk * lax.rsqrt(_dot(kk * kk, head_sum_b) + 1e-12)
    k2 = k * (1.0 + (a - 1.0) * ka_ref[...])
    alpha = -kk
    beta = kk * a

    row = _iota((rows, rows), 0)
    col = _iota((rows, rows), 1)
    same_row = (row // blk) == (col // blk)
    incl = same_row & (col <= row)
    strict = same_row & (col < row)
    cs = _dot_rhs2(incl.astype(BF16), log_decay)
    e_neg = jnp.exp(-cs)
    a_t = alpha * jnp.exp(cs - log_decay)
    b_t = (beta * e_neg).astype(BF16)
    k_t = (k2 * e_neg).astype(BF16)
    r_t = r * jnp.exp(cs)
    v_b = v.astype(BF16)

    states = [st_ref[b] for b in range(nb)]
    reads = []
    for b in range(nb):
        sl = slice(b * blk, (b + 1) * blk)
        reads.append(_dot(jnp.concatenate([a_t[sl], r_t[sl]], axis=0), states[b]))
    a_s = jnp.concatenate([x[0:blk] for x in reads], axis=0)
    r_s = jnp.concatenate([x[blk:2 * blk] for x in reads], axis=0)

    wlane = _iota((rows, width), 1)
    in_head = [(wlane // RWKV_HEAD_DIM) == h for h in range(RWKV_HEADS)]
    parts = []
    for h in range(RWKV_HEADS):
        parts += [jnp.where(in_head[h], a_t, 0.0), jnp.where(in_head[h], r_t, 0.0)]
    lhs = jnp.concatenate(parts, axis=0).astype(BF16)
    pb_all = _dot_nt(lhs, b_t)
    pk_all = _dot_nt(lhs, k_t)

    heads = range(RWKV_HEADS)
    base = [2 * h * rows for h in heads]
    npow = [jnp.where(strict, pb_all[base[h]:base[h] + rows], 0.0).astype(BF16) for h in heads]
    u = [a_s + _dot(jnp.where(strict, pk_all[base[h]:base[h] + rows], 0.0), v_b) for h in heads]
    steps = int(math.log2(blk))
    for i in range(steps):
        u = [u[h] + _dot(npow[h], u[h]) for h in heads]
        if i + 1 < steps:
            npow = [_dot(npow[h], npow[h]).astype(BF16) for h in heads]
    y = [_dot(jnp.where(incl, pb_all[base[h] + rows:base[h] + 2 * rows], 0.0), u[h])
         + _dot(jnp.where(incl, pk_all[base[h] + rows:base[h] + 2 * rows], 0.0), v_b) for h in heads]
    u_all = u[0]
    y_all = y[0]
    for h in range(1, RWKV_HEADS):
        u_all = jnp.where(in_head[h], u[h], u_all)
        y_all = jnp.where(in_head[h], y[h], y_all)
    y_all = y_all + r_s

    for b in range(nb):
        sl = slice(b * blk, (b + 1) * blk)
        c_last = cs[(b + 1) * blk - 1:(b + 1) * blk, :]
        to_end = jnp.exp(c_last - cs[sl])
        new = (_dot_tn((beta[sl] * to_end).astype(BF16), u_all[sl].astype(BF16))
               + _dot_tn((k2[sl] * to_end).astype(BF16), v_b[sl]))
        chunk_decay = jnp.broadcast_to(jnp.exp(c_last), (SUBLANES, width)).T[:, 0:1]
        st_ref[b] = states[b] * chunk_decay + head_sum * new

    head_mean_b = (head_sum * (1.0 / RWKV_HEAD_DIM)).astype(BF16)
    mean = _dot_lhs2(y_all, head_mean_b)
    yc = y_all - mean
    var = _dot(yc * yc, head_mean_b)
    yn = yc * lax.rsqrt(var + RWKV_GN_EPS) * lng_ref[...] + lnb_ref[...]
    bonus = _dot(r * k2 * rk_ref[...], head_sum_b) * v
    out = ((yn + bonus) * g).astype(o_ref.dtype)
    for b in range(nb):
        o_ref[b] = out[b * blk:(b + 1) * blk]


def _rwkv_mixer(p, mu, wlr, w0, a0, k_k, k_a, r_k, ln_g, ln_b):
    bsz, seq, _ = p.shape
    blk = RWKV_BLOCK
    small = lambda a: pl.BlockSpec(a.shape, lambda c: (0, 0))
    args = (mu, wlr, w0, a0, k_k, k_a, r_k, ln_g, ln_b)
    return pl.pallas_call(
        _rwkv_kernel, grid=(seq // blk,),
        in_specs=[pl.BlockSpec((bsz, blk, RWKV_IN), lambda c: (0, c, 0))] + [small(a) for a in args],
        out_specs=pl.BlockSpec((bsz, blk, RWKV_WIDTH), lambda c: (0, c, 0)),
        out_shape=jax.ShapeDtypeStruct((bsz, seq, RWKV_WIDTH), BF16),
        scratch_shapes=[pltpu.VMEM((bsz, blk + 2 * SUBLANES, RWKV_IN), F32),
                        pltpu.VMEM((bsz, RWKV_WIDTH, RWKV_WIDTH), F32)],
        compiler_params=_params("arbitrary"), name="rwkv7_mixer")(p, *args)


def _gla_kernel(p_ref, wa_ref, ba_ref, ng_ref, o_ref, st_ref, z_ref, acc_ref):
    blk = o_ref.shape[0]
    qk = GLA_QK_WIDTH
    vw = GLA_WIDTH

    @pl.when(pl.program_id(1) == 0)
    def _():
        st_ref[...] = jnp.zeros(st_ref.shape, F32)

    q = p_ref[:, 0:qk] * (GLA_KEY_DIM ** -0.5)
    k = p_ref[:, qk:2 * qk]
    g = p_ref[:, 2 * qk + vw:2 * qk + 2 * vw]
    a_lr = p_ref[:, 2 * qk + 2 * vw:GLA_IN]
    v_off = 2 * qk

    logit = _dot_hi(a_lr, wa_ref[...]) + ba_ref[...]
    log_a = -_softplus(-logit) / GLA_GATE_TEMP
    row = _iota((blk, blk), 0)
    col = _iota((blk, blk), 1)
    cum = _dot_hi((col <= row).astype(F32), log_a)

    state = st_ref[...]
    acc_ref[...] = _dot_nt((q * jnp.exp(cum)).astype(BF16), state.astype(BF16))

    spread = _block_indicator(qk, vw, GLA_KEY_DIM, GLA_VALUE_DIM).astype(BF16)
    rows = _iota((blk, qk), 0)
    group = 2 * SUBLANES
    for jb in range(blk // group):
        r0 = jb * group
        n = blk - r0
        for jj in range(group):
            j = r0 + jj
            decay = jnp.exp(jnp.where(rows[r0:] >= j, cum[r0:] - cum[j:j + 1, :], -jnp.inf))
            z_ref[jj * n:(jj + 1) * n, :] = (q[r0:] * decay * k[j:j + 1, :]).astype(BF16)
        res = jnp.dot(z_ref[0:group * n, :], spread, preferred_element_type=F32)
        part = jnp.zeros((n, vw), F32)
        for jj in range(group):
            j = r0 + jj
            part = part + res[jj * n:(jj + 1) * n] * p_ref[j:j + 1, v_off:v_off + vw]
        acc_ref[r0:blk, :] = acc_ref[r0:blk, :] + part

    v = p_ref[:, v_off:v_off + vw]
    last = cum[blk - 1:blk, :]
    k_dec = k * jnp.exp(last - cum)
    keep = _block_indicator(vw, qk, GLA_VALUE_DIM, GLA_KEY_DIM)
    st_ref[...] = state * jnp.exp(last) + keep * _dot_tn(v.astype(BF16), k_dec.astype(BF16))

    o = acc_ref[...]
    head_mean = _block_indicator(vw, vw, GLA_VALUE_DIM, GLA_VALUE_DIM) * (1.0 / GLA_VALUE_DIM)
    ms = _dot_hi(o * o, head_mean)
    o_ref[...] = (o * lax.rsqrt(ms + RMS_EPS) * ng_ref[...] * _silu(g)).astype(o_ref.dtype)


def _gla_mixer(p, w_a2, b_a, norm_g):
    bsz, seq, _ = p.shape
    blk = GLA_BLOCK
    small = lambda a: pl.BlockSpec(a.shape, lambda b, c: (0, 0))
    args = (w_a2, b_a, norm_g)
    return pl.pallas_call(
        _gla_kernel, grid=(bsz, seq // blk),
        in_specs=[pl.BlockSpec((None, blk, GLA_IN), lambda b, c: (b, c, 0))] + [small(a) for a in args],
        out_specs=pl.BlockSpec((None, blk, GLA_WIDTH), lambda b, c: (b, c, 0)),
        out_shape=jax.ShapeDtypeStruct((bsz, seq, GLA_WIDTH), BF16),
        scratch_shapes=[pltpu.VMEM((GLA_WIDTH, GLA_QK_WIDTH), F32),
                        pltpu.VMEM((2 * SUBLANES * blk, GLA_QK_WIDTH), BF16),
                        pltpu.VMEM((blk, GLA_WIDTH), F32)],
        compiler_params=_params("parallel", "arbitrary"), name="gla_mixer")(p, *args)


def _outproj_kernel(ys_ref, yr_ref, yg_ref, h_ref, w1_ref, w2_ref, w3_ref, g_ref, b_ref, wr_ref, br_ref,
                    h1_ref, eid_ref, gate_ref):
    mix = (jnp.dot(ys_ref[...], w1_ref[...], preferred_element_type=F32)
           + jnp.dot(yr_ref[...], w2_ref[...], preferred_element_type=F32)
           + jnp.dot(yg_ref[...], w3_ref[...], preferred_element_type=F32))
    h1 = _layer_norm(DEEPNORM_ALPHA * h_ref[...] + mix, g_ref[...], b_ref[...])
    h1_ref[...] = h1

    logits = _dot_hi(h1, wr_ref[...]) + br_ref[...]
    tm = logits.shape[0]
    lane = _iota((tm, LANES), 1)
    lane_f = lane.astype(F32)

    def masked_softmax(mask):
        m = jnp.max(jnp.where(mask, logits, -jnp.inf), -1, keepdims=True)
        e = jnp.where(mask, jnp.exp(logits - m), 0.0)
        return e / jnp.sum(e, -1, keepdims=True)

    def first_argmax(vals, mask):
        m = jnp.max(jnp.where(mask, vals, -jnp.inf), -1, keepdims=True)
        idx = jnp.min(jnp.where(mask & (vals == m), lane_f, float(LANES)), -1, keepdims=True)
        return m, idx.astype(jnp.int32)

    is_group = lane < N_GROUPS
    g_prob = masked_softmax(is_group)
    g_w, g_idx = first_argmax(g_prob, is_group)
    e_lo = N_GROUPS + g_idx * EXPERTS_PER_GROUP
    in_group = (lane >= e_lo) & (lane < e_lo + EXPERTS_PER_GROUP)
    e_prob = masked_softmax(in_group)
    p1, i1 = first_argmax(e_prob, in_group)
    p2, i2 = first_argmax(e_prob, in_group & (lane != i1))
    denom = p1 + p2
    eid_ref[...] = jnp.where(lane == 0, i1 - N_GROUPS, jnp.where(lane == 1, i2 - N_GROUPS, 0))
    gate_ref[...] = jnp.where(lane == 0, g_w * (p1 / denom), jnp.where(lane == 1, g_w * (p2 / denom), 0.0))


def _out_projection(y_ssd, y_rwkv, y_gla, h, w_out, ln_g, ln_b, w_r, b_r):
    n_tok, d = h.shape
    tm = TOKEN_TILE
    row = lambda n: pl.BlockSpec((tm, n), lambda i: (i, 0))
    full = lambda a: pl.BlockSpec(a.shape, lambda i: (0, 0))
    w1 = w_out[0:SSD_WIDTH]
    w2 = w_out[SSD_WIDTH:SSD_WIDTH + RWKV_WIDTH]
    w3 = w_out[SSD_WIDTH + RWKV_WIDTH:MIX_WIDTH]
    consts = (w1, w2, w3, ln_g, ln_b, w_r, b_r)
    return pl.pallas_call(
        _outproj_kernel, grid=(n_tok // tm,),
        in_specs=[row(SSD_WIDTH), row(RWKV_WIDTH), row(GLA_WIDTH), row(d)] + [full(a) for a in consts],
        out_specs=[row(d), row(LANES), row(LANES)],
        out_shape=[jax.ShapeDtypeStruct((n_tok, d), F32),
                   jax.ShapeDtypeStruct((n_tok, LANES), jnp.int32),
                   jax.ShapeDtypeStruct((n_tok, LANES), F32)],
        compiler_params=_params("parallel"), name="out_projection")(y_ssd, y_rwkv, y_gla, h, *consts)


def _dest_kernel(eid_ref, dest_ref, cnt_ref, run_ref, base_ref):
    phase = pl.program_id(0)
    step = pl.program_id(1)
    tm = eid_ref.shape[0]
    e = eid_ref[...]
    lane = _iota((tm, LANES), 1)
    oh0 = (lane == e[:, 0:1]).astype(F32)
    oh1 = (lane == e[:, 1:2]).astype(F32)
    oh = oh0 + oh1
    tile_count = jnp.sum(oh, 0, keepdims=True)

    @pl.when(step == 0)
    def _():
        run_ref[...] = jnp.zeros(run_ref.shape, F32)

    @pl.when((phase == 1) & (step == 0))
    def _():
        blocks = jnp.ceil(cnt_ref[...] * (1.0 / ROW_BLOCK))
        before = (_iota((LANES, LANES), 0) < _iota((LANES, LANES), 1)).astype(BF16)
        start = jnp.dot(jnp.broadcast_to(blocks, (SUBLANES, LANES)).astype(BF16), before,
                        preferred_element_type=F32)[0:1, :]
        base_ref[...] = start * ROW_BLOCK

    @pl.when(phase == 0)
    def _():
        total = run_ref[...] + tile_count
        run_ref[...] = total
        cnt_ref[...] = total

    @pl.when(phase == 1)
    def _():
        earlier = (_iota((tm, tm), 1) < _iota((tm, tm), 0)).astype(BF16)
        rank = jnp.dot(earlier, oh.astype(BF16), preferred_element_type=F32)
        pos = rank + run_ref[...] + base_ref[...]
        d0 = jnp.sum(oh0 * pos, -1, keepdims=True)
        d1 = jnp.sum(oh1 * pos, -1, keepdims=True)
        dest_ref[...] = jnp.where(lane == 0, d0, jnp.where(lane == 1, d1, 0.0)).astype(jnp.int32)
        run_ref[...] = run_ref[...] + tile_count


def _dispatch_plan(eid):
    n_tok = eid.shape[0]
    tm = TOKEN_TILE
    return pl.pallas_call(
        _dest_kernel, grid=(2, n_tok // tm),
        in_specs=[pl.BlockSpec((tm, LANES), lambda ph, i: (i, 0))],
        out_specs=[pl.BlockSpec((tm, LANES), lambda ph, i: (i * ph, 0)),
                   pl.BlockSpec((1, LANES), lambda ph, i: (0, 0))],
        out_shape=[jax.ShapeDtypeStruct((n_tok, LANES), jnp.int32),
                   jax.ShapeDtypeStruct((1, LANES), F32)],
        scratch_shapes=[pltpu.VMEM((1, LANES), F32), pltpu.VMEM((1, LANES), F32)],
        compiler_params=_params("arbitrary", "arbitrary"), name="dispatch_plan")(eid)


def _row_copy(src_ref, src_row, dst_ref, dst_row, sem):
    return pltpu.make_async_copy(src_ref.at[pl.ds(src_row, 1)], dst_ref.at[pl.ds(dst_row, 1)], sem)


def _scatter_kernel(dest_ref, h_ref, xs_in_ref, xs_ref, sem):
    del xs_in_ref
    tm = h_ref.shape[0]

    def start(t, carry):
        for k in range(TOP_K):
            _row_copy(h_ref, t, xs_ref, dest_ref[0, 0, TOP_K * t + k], sem).start()
        return carry

    lax.fori_loop(0, tm, start, 0, unroll=DMA_UNROLL)
    all_rows = xs_ref.at[pl.ds(0, TOP_K * tm)]
    pltpu.make_async_copy(all_rows, all_rows, sem).wait()


def _scatter_rows(dest3, h1, n_rows):
    n_tok, d = h1.shape
    n_tiles = dest3.shape[0]
    xs0 = jnp.zeros((n_rows, d), h1.dtype)
    return pl.pallas_call(
        _scatter_kernel, grid=(n_tiles,),
        in_specs=[pl.BlockSpec((1, 1, dest3.shape[2]), lambda i: (i, 0, 0), memory_space=pltpu.SMEM),
                  pl.BlockSpec((n_tok // n_tiles, d), lambda i: (i, 0)), pl.BlockSpec(memory_space=pl.ANY)],
        out_specs=pl.BlockSpec(memory_space=pl.ANY),
        out_shape=jax.ShapeDtypeStruct((n_rows, d), h1.dtype),
        scratch_shapes=[pltpu.SemaphoreType.DMA(())],
        input_output_aliases={2: 0},
        compiler_params=_params("arbitrary"), name="expert_scatter")(dest3, h1, xs0)


def _expert_kernel(be_ref, used_ref, x_ref, wg_ref, wu_ref, wd_ref, y_ref, wg_b, wu_b, wd_b):
    b = pl.program_id(0)

    @pl.when((b == 0) | (be_ref[b] != be_ref[jnp.maximum(b - 1, 0)]))
    def _():
        wg_b[...] = wg_ref[...].astype(BF16)
        wu_b[...] = wu_ref[...].astype(BF16)
        wd_b[...] = wd_ref[...].astype(BF16)

    @pl.when(b < used_ref[0])
    def _():
        x = x_ref[...].astype(BF16)
        gate = jnp.dot(x, wg_b[...], preferred_element_type=F32)
        up = jnp.dot(x, wu_b[...], preferred_element_type=F32)
        mid = (_silu(gate) * up).astype(BF16)
        y_ref[...] = jnp.dot(mid, wd_b[...], preferred_element_type=F32)

    @pl.when(b >= used_ref[0])
    def _():
        y_ref[...] = jnp.zeros(y_ref.shape, y_ref.dtype)


def _expert_mlp(block_expert, n_used, xs, w_gate, w_up, w_down, layer):
    n_rows, d = xs.shape
    n_blocks = n_rows // ROW_BLOCK
    ff = w_gate.shape[-1]
    return pl.pallas_call(
        _expert_kernel,
        grid_spec=pltpu.PrefetchScalarGridSpec(
            num_scalar_prefetch=2, grid=(n_blocks,),
            in_specs=[pl.BlockSpec((ROW_BLOCK, d), lambda b, be, nu: (b, 0)),
                      pl.BlockSpec((None, None, d, ff), lambda b, be, nu: (layer, be[b], 0, 0)),
                      pl.BlockSpec((None, None, d, ff), lambda b, be, nu: (layer, be[b], 0, 0)),
                      pl.BlockSpec((None, None, ff, d), lambda b, be, nu: (layer, be[b], 0, 0))],
            out_specs=pl.BlockSpec((ROW_BLOCK, d), lambda b, be, nu: (b, 0)),
            scratch_shapes=[pltpu.VMEM((d, ff), BF16), pltpu.VMEM((d, ff), BF16), pltpu.VMEM((ff, d), BF16)]),
        out_shape=jax.ShapeDtypeStruct((n_rows, d), F32),
        compiler_params=_params("arbitrary"), name="expert_mlp")(
            block_expert, n_used, xs, w_gate, w_up, w_down)


def _combine_kernel(dest_ref, y_ref, h_ref, gate_ref, g_ref, b_ref, o_ref, buf_ref, sem):
    tm = o_ref.shape[0]

    def start(t, carry):
        for k in range(TOP_K):
            _row_copy(y_ref, dest_ref[0, 0, TOP_K * t + k], buf_ref.at[k], t, sem).start()
        return carry

    lax.fori_loop(0, tm, start, 0, unroll=DMA_UNROLL)
    pltpu.make_async_copy(buf_ref, buf_ref, sem).wait()
    gate = gate_ref[...]
    ffn = gate[:, 0:1] * buf_ref[0] + gate[:, 1:2] * buf_ref[1]
    o_ref[...] = _layer_norm(DEEPNORM_ALPHA * h_ref[...] + ffn, g_ref[...], b_ref[...])


def _combine(dest3, ys, h1, gate, ln_g, ln_b):
    n_tok, d = h1.shape
    tm = TOKEN_TILE
    row = lambda n: pl.BlockSpec((tm, n), lambda i: (i, 0))
    vec = pl.BlockSpec((1, d), lambda i: (0, 0))
    return pl.pallas_call(
        _combine_kernel, grid=(n_tok // tm,),
        in_specs=[pl.BlockSpec((1, 1, dest3.shape[2]), lambda i: (i, 0, 0), memory_space=pltpu.SMEM),
                  pl.BlockSpec(memory_space=pl.ANY), row(d), row(LANES), vec, vec],
        out_specs=row(d),
        out_shape=jax.ShapeDtypeStruct((n_tok, d), F32),
        scratch_shapes=[pltpu.VMEM((TOP_K, tm, d), F32), pltpu.SemaphoreType.DMA(())],
        compiler_params=_params("arbitrary"), name="expert_combine")(dest3, ys, h1, gate, ln_g, ln_b)


def _pad_lanes(a, n):
    return jnp.pad(a, [(0, 0)] * (a.ndim - 1) + [(0, n - a.shape[-1])])


def _row(a, n=None):
    a = a.reshape(1, -1).astype(F32)
    return a if n is None else _pad_lanes(a, n)


def _mixer_layer(h, w_in, ssd_conv_w, ssd_conv_b, ssd_dt_bias, ssd_a_log, ssd_d, ssd_norm_g,
                 rwkv_mu, rwkv_w0, rwkv_w2, rwkv_a0, rwkv_a2, rwkv_g2, rwkv_k_k, rwkv_k_a, rwkv_r_k,
                 rwkv_ln_g, rwkv_ln_b, gla_w_a2, gla_b_a, gla_norm_g, bsz, seq):
    o = 0
    cols = {}
    for name, n in (("z", SSD_WIDTH), ("xbc", SSD_XBC), ("dt", SSD_HEADS), ("rwkv", RWKV_IN),
                    ("q", GLA_QK_WIDTH), ("k", GLA_QK_WIDTH), ("v", GLA_WIDTH), ("g", GLA_WIDTH),
                    ("ga", GLA_GATE_RANK)):
        cols[name] = w_in[:, o:o + n]
        o += n
    w_ssd = jnp.concatenate([cols["z"], cols["xbc"], _pad_lanes(cols["dt"], LANES)], 1).astype(BF16)
    w_rwkv = cols["rwkv"].astype(BF16)
    w_gla = jnp.concatenate([cols["q"], cols["k"], cols["v"], cols["g"], _pad_lanes(cols["ga"], LANES)],
                            1).astype(BF16)
    p_ssd, p_rwkv, p_gla = _in_projection(h, w_ssd, w_rwkv, w_gla)

    y_ssd = _ssd_mixer(
        p_ssd.reshape(bsz, seq, SSD_IN), ssd_conv_w.astype(F32), _row(ssd_conv_b), _row(ssd_dt_bias, LANES),
        _row(-jnp.exp(ssd_a_log.astype(F32)), LANES), _row(jnp.repeat(ssd_d.astype(F32), SSD_HEAD_DIM)),
        _row(ssd_norm_g))

    w_lr = jnp.zeros((RWKV_LR, 3 * RWKV_WIDTH), F32)
    w_lr = w_lr.at[0:RWKV_DECAY_RANK, 0:RWKV_WIDTH].set(rwkv_w2)
    w_lr = w_lr.at[RWKV_DECAY_RANK:RWKV_DECAY_RANK + RWKV_ICLR_RANK, RWKV_WIDTH:2 * RWKV_WIDTH].set(rwkv_a2)
    w_lr = w_lr.at[RWKV_DECAY_RANK + RWKV_ICLR_RANK:, 2 * RWKV_WIDTH:].set(rwkv_g2)
    y_rwkv = _rwkv_mixer(
        p_rwkv.reshape(bsz, seq, RWKV_IN), _row(rwkv_mu), w_lr, _row(rwkv_w0), _row(rwkv_a0), _row(rwkv_k_k),
        _row(rwkv_k_a), _row(rwkv_r_k), _row(rwkv_ln_g), _row(rwkv_ln_b))

    w_a2 = jnp.zeros((LANES, GLA_QK_WIDTH), F32).at[0:GLA_GATE_RANK].set(gla_w_a2)
    y_gla = _gla_mixer(p_gla.reshape(bsz, seq, GLA_IN), w_a2, _row(gla_b_a), _row(gla_norm_g))

    n_tok = bsz * seq
    return (y_ssd.reshape(n_tok, SSD_WIDTH), y_rwkv.reshape(n_tok, RWKV_WIDTH),
            y_gla.reshape(n_tok, GLA_WIDTH))


def _moe_layer(h1, eid, gate, layer, w_gate, w_up, w_down, ln_g, ln_b):
    n_tok = h1.shape[0]
    tm = TOKEN_TILE
    n_blocks = n_tok * TOP_K // ROW_BLOCK + N_EXPERTS
    dest, counts = _dispatch_plan(eid)
    blocks = jnp.ceil(counts[0, 0:N_EXPERTS] / ROW_BLOCK).astype(jnp.int32)
    block_end = jnp.cumsum(blocks)
    block_ids = jnp.arange(n_blocks, dtype=jnp.int32)
    block_expert = jnp.minimum(jnp.sum((block_end[None, :] <= block_ids[:, None]).astype(jnp.int32), axis=1),
                               N_EXPERTS - 1)
    n_used = block_end[N_EXPERTS - 1:N_EXPERTS]
    dest3 = dest[:, 0:TOP_K].reshape(n_tok // tm, 1, tm * TOP_K)
    xs = _scatter_rows(dest3, h1, n_blocks * ROW_BLOCK)
    ys = _expert_mlp(block_expert, n_used, xs, w_gate, w_up, w_down, layer)
    return _combine(dest3, ys, h1, gate, ln_g, ln_b)


def kernel(x, ln_in_g, ln_in_b, w_in, ssd_conv_w, ssd_conv_b, ssd_dt_bias, ssd_a_log, ssd_d, ssd_norm_g, rwkv_mu, rwkv_w0, rwkv_w2, rwkv_a0, rwkv_a2, rwkv_g2, rwkv_k_k, rwkv_k_a, rwkv_r_k, rwkv_ln_g, rwkv_ln_b, gla_w_a2, gla_b_a, gla_norm_g, w_out, ln1_g, ln1_b, moe_w_rg, moe_b_rg, moe_w_re, moe_b_re, moe_w_gate, moe_w_up, moe_w_down, ln2_g, ln2_b):
    bsz, seq, d = x.shape
    n_tok = bsz * seq
    h = _input_ln(x.reshape(n_tok, d), ln_in_g, ln_in_b)
    for i in range(w_in.shape[0]):
        y_ssd, y_rwkv, y_gla = _mixer_layer(
            h, w_in[i], ssd_conv_w[i], ssd_conv_b[i], ssd_dt_bias[i], ssd_a_log[i], ssd_d[i], ssd_norm_g[i],
            rwkv_mu[i], rwkv_w0[i], rwkv_w2[i], rwkv_a0[i], rwkv_a2[i], rwkv_g2[i], rwkv_k_k[i],
            rwkv_k_a[i], rwkv_r_k[i], rwkv_ln_g[i], rwkv_ln_b[i], gla_w_a2[i], gla_b_a[i], gla_norm_g[i],
            bsz, seq)
        w_r = _pad_lanes(jnp.concatenate([moe_w_rg[i], moe_w_re[i]], 1).astype(F32), LANES)
        b_r = _row(jnp.concatenate([moe_b_rg[i], moe_b_re[i]]), LANES)
        h1, eid, gate = _out_projection(
            y_ssd, y_rwkv, y_gla, h, w_out[i].astype(BF16), _row(ln1_g[i]), _row(ln1_b[i]), w_r, b_r)
        h = _moe_layer(h1, eid, gate, i, moe_w_gate, moe_w_up, moe_w_down, _row(ln2_g[i]), _row(ln2_b[i]))
    return h.reshape(bsz, seq, d)
```

```python
import math

import jax
import jax.numpy as jnp
from jax import lax
from jax.experimental import pallas as pl
from jax.experimental.pallas import tpu as pltpu

F32 = jnp.float32
BF16 = jnp.bfloat16

D_MODEL = 1024
DEPTH = 2

SSD_HEAD_DIM = 64
SSD_WIDTH = D_MODEL // 2
SSD_HEADS = SSD_WIDTH // SSD_HEAD_DIM
SSD_GROUPS = 2
SSD_STATE = 64
SSD_CONV = 4
SSD_BC = SSD_GROUPS * SSD_STATE
SSD_XBC = SSD_WIDTH + 2 * SSD_BC

RWKV_HEAD_DIM = 64
RWKV_WIDTH = D_MODEL // 4
RWKV_HEADS = RWKV_WIDTH // RWKV_HEAD_DIM
RWKV_DECAY_RANK = 32
RWKV_ICLR_RANK = 32
RWKV_GATE_RANK = 64
RWKV_LR = RWKV_DECAY_RANK + RWKV_ICLR_RANK + RWKV_GATE_RANK
RWKV_IN = 3 * RWKV_WIDTH + RWKV_LR
RWKV_GN_EPS = 64e-5

GLA_VALUE_DIM = 64
GLA_WIDTH = D_MODEL // 4
GLA_HEADS = GLA_WIDTH // GLA_VALUE_DIM
GLA_KEY_DIM = GLA_VALUE_DIM // 2
GLA_QK_WIDTH = GLA_HEADS * GLA_KEY_DIM
GLA_GATE_RANK = 16
GLA_GATE_TEMP = 16.0

MIX_WIDTH = SSD_WIDTH + RWKV_WIDTH + GLA_WIDTH

N_GROUPS = 4
EXPERTS_PER_GROUP = 8
N_EXPERTS = N_GROUPS * EXPERTS_PER_GROUP
TOP_K = 2
EXPERT_FF = 512

DEEPNORM_ALPHA = (2 * DEPTH) ** 0.25
LN_EPS = 1e-5
RMS_EPS = 1e-6

LANES = 128
SUBLANES = 8
SSD_IN = SSD_WIDTH + SSD_XBC + LANES
GLA_IN = 2 * GLA_QK_WIDTH + 2 * GLA_WIDTH + LANES

TOKEN_TILE = 256
SSD_BLOCK = 128
RWKV_BLOCK = 64
GLA_BLOCK = 64
CHUNKS_PER_STEP = 4
ROW_BLOCK = 256
DMA_UNROLL = 8
VMEM_LIMIT = 48 * 1024 * 1024


def _dot(a, b):
    return jnp.dot(a.astype(BF16), b.astype(BF16), preferred_element_type=F32)


def _split_bf16(x):
    hi = x.astype(BF16)
    return hi, (x - hi.astype(F32)).astype(BF16)


def _dot_rhs2(a_exact, b):
    b_hi, b_lo = _split_bf16(b)
    return (jnp.dot(a_exact, b_hi, preferred_element_type=F32)
            + jnp.dot(a_exact, b_lo, preferred_element_type=F32))


def _dot_lhs2(a, b_exact):
    a_hi, a_lo = _split_bf16(a)
    return (jnp.dot(a_hi, b_exact, preferred_element_type=F32)
            + jnp.dot(a_lo, b_exact, preferred_element_type=F32))


def _cumsum_rows(mask, x):
    m = mask.astype(BF16)
    x_hi, rest = _split_bf16(x)
    x_mid, x_lo = _split_bf16(x - x_hi.astype(F32))
    del rest
    return (jnp.dot(m, x_hi, preferred_element_type=F32)
            + (jnp.dot(m, x_mid, preferred_element_type=F32) + jnp.dot(m, x_lo, preferred_element_type=F32)))


def _dot_x3(a, b):
    a_hi, a_lo = _split_bf16(a)
    b_hi, b_lo = _split_bf16(b)
    return (jnp.dot(a_hi, b_hi, preferred_element_type=F32)
            + (jnp.dot(a_hi, b_lo, preferred_element_type=F32)
               + jnp.dot(a_lo, b_hi, preferred_element_type=F32)))


def _dot_nt(a, b, precision=None):
    return lax.dot_general(a, b, (((1,), (1,)), ((), ())), precision=precision,
                           preferred_element_type=F32)


def _dot_tn(a, b, precision=None):
    return lax.dot_general(a, b, (((0,), (0,)), ((), ())), precision=precision,
                           preferred_element_type=F32)


def _sigmoid(x):
    return 1.0 / (1.0 + jnp.exp(-x))


def _silu(x):
    return x * _sigmoid(x)


def _softplus(x):
    return jnp.maximum(x, 0.0) + jnp.log1p(jnp.exp(-jnp.abs(x)))


def _iota(shape, dim):
    return lax.broadcasted_iota(jnp.int32, shape, dim)


def _block_indicator(rows, cols, row_seg, col_seg):
    r = _iota((rows, cols), 0) // row_seg
    c = _iota((rows, cols), 1) // col_seg
    return (r == c).astype(F32)


def _layer_norm(x, g, b):
    mu = jnp.mean(x, -1, keepdims=True)
    xc = x - mu
    var = jnp.mean(xc * xc, -1, keepdims=True)
    return xc * lax.rsqrt(var + LN_EPS) * g + b


def _params(*sem):
    return pltpu.CompilerParams(dimension_semantics=sem, vmem_limit_bytes=VMEM_LIMIT)


def _ln_kernel(x_ref, g_ref, b_ref, o_ref):
    o_ref[...] = _layer_norm(x_ref[...], g_ref[...], b_ref[...])


def _input_ln(x, g, b):
    n_tok, d = x.shape
    tm = TOKEN_TILE
    row = pl.BlockSpec((tm, d), lambda i: (i, 0))
    vec = pl.BlockSpec((1, d), lambda i: (0, 0))
    return pl.pallas_call(
        _ln_kernel, grid=(n_tok // tm,), in_specs=[row, vec, vec], out_specs=row,
        out_shape=jax.ShapeDtypeStruct((n_tok, d), F32), compiler_params=_params("parallel"),
        name="input_ln")(x, g.reshape(1, d), b.reshape(1, d))


def _inproj_kernel(h_ref, w1_ref, w2_ref, w3_ref, o1_ref, o2_ref, o3_ref):
    hb = h_ref[...].astype(BF16)
    o1_ref[...] = jnp.dot(hb, w1_ref[...], preferred_element_type=F32)
    o2_ref[...] = jnp.dot(hb, w2_ref[...], preferred_element_type=F32)
    o3_ref[...] = jnp.dot(hb, w3_ref[...], preferred_element_type=F32)


def _in_projection(h, w_ssd, w_rwkv, w_gla):
    n_tok, d = h.shape
    tm = TOKEN_TILE
    ws = (w_ssd, w_rwkv, w_gla)
    return pl.pallas_call(
        _inproj_kernel, grid=(n_tok // tm,),
        in_specs=[pl.BlockSpec((tm, d), lambda i: (i, 0))]
        + [pl.BlockSpec(w.shape, lambda i: (0, 0)) for w in ws],
        out_specs=[pl.BlockSpec((tm, w.shape[1]), lambda i: (i, 0)) for w in ws],
        out_shape=[jax.ShapeDtypeStruct((n_tok, w.shape[1]), F32) for w in ws],
        compiler_params=_params("parallel"), name="in_projection")(h, *ws)


def _ssd_kernel(p_ref, cw_ref, cb_ref, dtb_ref, a_ref, d_ref, ng_ref, o_ref, xs_ref, st_ref):
    @pl.when(pl.program_id(1) == 0)
    def _():
        xs_ref[0:SUBLANES, :] = jnp.zeros((SUBLANES, SSD_XBC), F32)
        st_ref[...] = jnp.zeros(st_ref.shape, F32)

    for s in range(o_ref.shape[0] // SSD_BLOCK):
        rows = pl.ds(s * SSD_BLOCK, SSD_BLOCK)
        _ssd_chunk(p_ref.at[rows], cw_ref, cb_ref, dtb_ref, a_ref, d_ref, ng_ref, o_ref.at[rows], xs_ref, st_ref)


def _ssd_chunk(p_ref, cw_ref, cb_ref, dtb_ref, a_ref, d_ref, ng_ref, o_ref, xs_ref, st_ref):
    blk = o_ref.shape[0]
    tail = SUBLANES
    z = p_ref[:, 0:SSD_WIDTH]
    xs_ref[tail:tail + blk, :] = p_ref[:, SSD_WIDTH:SSD_WIDTH + SSD_XBC]
    dt_raw = p_ref[:, SSD_WIDTH + SSD_XBC:SSD_IN]

    acc = jnp.broadcast_to(cb_ref[...], (blk, SSD_XBC))
    for i in range(SSD_CONV):
        acc = acc + cw_ref[i:i + 1, :] * xs_ref[pl.ds(tail - (SSD_CONV - 1) + i, blk), :]
    xs_ref[0:tail, :] = xs_ref[blk:blk + tail, :]
    xbc = _silu(acc)
    xh = xbc[:, 0:SSD_WIDTH]
    bm = xbc[:, SSD_WIDTH:SSD_WIDTH + SSD_BC]
    cm = xbc[:, SSD_WIDTH + SSD_BC:SSD_XBC]

    dt = _softplus(dt_raw + dtb_ref[...])
    a_dt = dt * a_ref[...]
    row = _iota((blk, blk), 0)
    col = _iota((blk, blk), 1)
    causal = col <= row
    a_cs = _cumsum_rows(causal, a_dt)
    a_cs_t = a_cs.T
    a_last = a_cs[blk - 1:blk, :]

    expand = _block_indicator(LANES, SSD_WIDTH, 1, SSD_HEAD_DIM).astype(BF16)
    per_head = jnp.concatenate([dt, jnp.exp(a_cs), jnp.exp(a_last - a_cs),
                                jnp.broadcast_to(jnp.exp(a_last), (SUBLANES, LANES))], axis=0)
    per_chan = _dot_lhs2(per_head, expand)
    dt_e = per_chan[0:blk]
    dec_e = per_chan[blk:2 * blk]
    te_e = per_chan[2 * blk:3 * blk]
    cd_e = per_chan[3 * blk:3 * blk + 1]

    x_dt = xh * dt_e
    state = st_ref[...]
    y_off = _dot(cm, state) * dec_e

    lane = _iota((blk, LANES), 1)
    hpg = SSD_HEADS // SSD_GROUPS
    pairs = []
    for j in range(SSD_HEADS // 2):
        grp = (2 * j) // hpg
        in_grp = (lane // SSD_STATE) == grp
        cb = _dot_nt(jnp.where(in_grp, cm, 0.0).astype(BF16), bm.astype(BF16))
        xp = x_dt[:, j * LANES:(j + 1) * LANES]
        ys = []
        for h in (2 * j, 2 * j + 1):
            diff = jnp.broadcast_to(a_cs[:, h:h + 1], (blk, blk)) - a_cs_t[h:h + 1, :]
            seg = jnp.exp(jnp.where(causal, diff, -jnp.inf))
            ys.append(_dot(cb * seg, xp))
        pairs.append(jnp.where(lane < SSD_HEAD_DIM, ys[0], ys[1]))
    y = jnp.concatenate(pairs, axis=1) + y_off + d_ref[...] * xh

    new = _dot_tn(bm.astype(BF16), (x_dt * te_e).astype(BF16))
    keep = _block_indicator(SSD_BC, SSD_WIDTH, SSD_STATE, SSD_WIDTH // SSD_GROUPS)
    st_ref[...] = state * cd_e + keep * new

    y = y * _silu(z)
    gw = SSD_WIDTH // SSD_GROUPS
    for g in range(SSD_GROUPS):
        yg = y[:, g * gw:(g + 1) * gw]
        ms = jnp.mean(yg * yg, -1, keepdims=True)
        o_ref[:, g * gw:(g + 1) * gw] = (yg * lax.rsqrt(ms + RMS_EPS)
                                         * ng_ref[:, g * gw:(g + 1) * gw]).astype(o_ref.dtype)


def _ssd_mixer(p, conv_w, conv_b, dt_bias, a_neg, d_skip, norm_g):
    bsz, seq, _ = p.shape
    blk = SSD_BLOCK * CHUNKS_PER_STEP
    small = lambda a: pl.BlockSpec(a.shape, lambda b, c: (0, 0))
    args = (conv_w, conv_b, dt_bias, a_neg, d_skip, norm_g)
    return pl.pallas_call(
        _ssd_kernel, grid=(bsz, seq // blk),
        in_specs=[pl.BlockSpec((None, blk, SSD_IN), lambda b, c: (b, c, 0))] + [small(a) for a in args],
        out_specs=pl.BlockSpec((None, blk, SSD_WIDTH), lambda b, c: (b, c, 0)),
        out_shape=jax.ShapeDtypeStruct((bsz, seq, SSD_WIDTH), BF16),
        scratch_shapes=[pltpu.VMEM((SSD_BLOCK + 2 * SUBLANES, SSD_XBC), F32),
                        pltpu.VMEM((SSD_BC, SSD_WIDTH), F32)],
        compiler_params=_params("parallel", "arbitrary"), name="ssd_mixer")(p, *args)


def _rwkv_kernel(p_ref, mu_ref, wlr_ref, w0_ref, a0_ref, kk_ref, ka_ref, rk_ref, lng_ref, lnb_ref,
                 o_ref, xs_ref, st_ref):
    nb, blk, _ = o_ref.shape
    rows = nb * blk
    tail = SUBLANES
    width = RWKV_WIDTH

    @pl.when(pl.program_id(0) == 0)
    def _():
        xs_ref[:, 0:tail, :] = jnp.zeros((nb, tail, RWKV_IN), F32)
        st_ref[...] = jnp.zeros(st_ref.shape, F32)

    ps, prevs = [], []
    for b in range(nb):
        ps.append(p_ref[b])
        xs_ref[b, tail:tail + blk, :] = ps[b]
        prevs.append(xs_ref[b, pl.ds(tail - 1, blk), :])
        xs_ref[b, 0:tail, :] = xs_ref[b, blk:blk + tail, :]
    p = jnp.concatenate(ps, axis=0)
    prev = jnp.concatenate(prevs, axis=0)
    pm = p + (prev - p) * mu_ref[...]
    r = pm[:, 0:width]
    k = pm[:, width:2 * width]
    v = pm[:, 2 * width:3 * width]
    lr = pm[:, 3 * width:RWKV_IN]

    lane = _iota((rows, RWKV_LR), 1)
    lr_act = jnp.where(lane < RWKV_DECAY_RANK, jnp.tanh(lr),
                       jnp.where(lane < RWKV_DECAY_RANK + RWKV_ICLR_RANK, lr, _sigmoid(lr)))
    proj = _dot(lr_act, wlr_ref[...])
    w = -_softplus(-(w0_ref[...] + proj[:, 0:width])) - 0.5
    log_decay = -jnp.exp(w)
    a = _sigmoid(a0_ref[...] + proj[:, width:2 * width])
    g = proj[:, 2 * width:3 * width]

    head_sum = _block_indicator(width, width, RWKV_HEAD_DIM, RWKV_HEAD_DIM)
    head_sum_b = head_sum.astype(BF16)
    kk = k * kk_ref[...]
    kk = kk * lax.rsqrt(_dot(kk * kk, head_sum_b) + 1e-12)
    k2 = k * (1.0 + (a - 1.0) * ka_ref[...])
    alpha = -kk
    beta = kk * a

    row = _iota((rows, rows), 0)
    col = _iota((rows, rows), 1)
    same_row = (row // blk) == (col // blk)
    incl = same_row & (col <= row)
    strict = same_row & (col < row)
    cs = _dot_rhs2(incl.astype(BF16), log_decay)
    e_neg = jnp.exp(-cs)
    a_t = alpha * jnp.exp(cs - log_decay)
    b_t = (beta * e_neg).astype(BF16)
    k_t = (k2 * e_neg).astype(BF16)
    r_t = r * jnp.exp(cs)
    v_b = v.astype(BF16)

    states = [st_ref[b] for b in range(nb)]
    reads = []
    for b in range(nb):
        sl = slice(b * blk, (b + 1) * blk)
        reads.append(_dot(jnp.concatenate([a_t[sl], r_t[sl]], axis=0), states[b]))
    a_s = jnp.concatenate([x[0:blk] for x in reads], axis=0)
    r_s = jnp.concatenate([x[blk:2 * blk] for x in reads], axis=0)

    wlane = _iota((rows, width), 1)
    in_head = [(wlane // RWKV_HEAD_DIM) == h for h in range(RWKV_HEADS)]
    parts = []
    for h in range(RWKV_HEADS):
        parts += [jnp.where(in_head[h], a_t, 0.0), jnp.where(in_head[h], r_t, 0.0)]
    lhs = jnp.concatenate(parts, axis=0).astype(BF16)
    pb_all = _dot_nt(lhs, b_t)
    pk_all = _dot_nt(lhs, k_t)

    heads = range(RWKV_HEADS)
    base = [2 * h * rows for h in heads]
    npow = [jnp.where(strict, pb_all[base[h]:base[h] + rows], 0.0).astype(BF16) for h in heads]
    u = [a_s + _dot(jnp.where(strict, pk_all[base[h]:base[h] + rows], 0.0), v_b) for h in heads]
    steps = int(math.log2(blk))
    for i in range(steps):
        u = [u[h] + _dot(npow[h], u[h]) for h in heads]
        if i + 1 < steps:
            npow = [_dot(npow[h], npow[h]).astype(BF16) for h in heads]
    y = [_dot(jnp.where(incl, pb_all[base[h] + rows:base[h] + 2 * rows], 0.0), u[h])
         + _dot(jnp.where(incl, pk_all[base[h] + rows:base[h] + 2 * rows], 0.0), v_b) for h in heads]
    u_all = u[0]
    y_all = y[0]
    for h in range(1, RWKV_HEADS):
        u_all = jnp.where(in_head[h], u[h], u_all)
        y_all = jnp.where(in_head[h], y[h], y_all)
    y_all = y_all + r_s

    for b in range(nb):
        sl = slice(b * blk, (b + 1) * blk)
        c_last = cs[(b + 1) * blk - 1:(b + 1) * blk, :]
        to_end = jnp.exp(c_last - cs[sl])
        new = (_dot_tn((beta[sl] * to_end).astype(BF16), u_all[sl].astype(BF16))
               + _dot_tn((k2[sl] * to_end).astype(BF16), v_b[sl]))
        chunk_decay = jnp.broadcast_to(jnp.exp(c_last), (SUBLANES, width)).T[:, 0:1]
        st_ref[b] = states[b] * chunk_decay + head_sum * new

    head_mean_b = (head_sum * (1.0 / RWKV_HEAD_DIM)).astype(BF16)
    mean = _dot_lhs2(y_all, head_mean_b)
    yc = y_all - mean
    var = _dot(yc * yc, head_mean_b)
    yn = yc * lax.rsqrt(var + RWKV_GN_EPS) * lng_ref[...] + lnb_ref[...]
    bonus = _dot(r * k2 * rk_ref[...], head_sum_b) * v
    out = ((yn + bonus) * g).astype(o_ref.dtype)
    for b in range(nb):
        o_ref[b] = out[b * blk:(b + 1) * blk]


def _rwkv_mixer(p, mu, wlr, w0, a0, k_k, k_a, r_k, ln_g, ln_b):
    bsz, seq, _ = p.shape
    blk = RWKV_BLOCK
    small = lambda a: pl.BlockSpec(a.shape, lambda c: (0, 0))
    args = (mu, wlr, w0, a0, k_k, k_a, r_k, ln_g, ln_b)
    return pl.pallas_call(
        _rwkv_kernel, grid=(seq // blk,),
        in_specs=[pl.BlockSpec((bsz, blk, RWKV_IN), lambda c: (0, c, 0))] + [small(a) for a in args],
        out_specs=pl.BlockSpec((bsz, blk, RWKV_WIDTH), lambda c: (0, c, 0)),
        out_shape=jax.ShapeDtypeStruct((bsz, seq, RWKV_WIDTH), BF16),
        scratch_shapes=[pltpu.VMEM((bsz, blk + 2 * SUBLANES, RWKV_IN), F32),
                        pltpu.VMEM((bsz, RWKV_WIDTH, RWKV_WIDTH), F32)],
        compiler_params=_params("arbitrary"), name="rwkv7_mixer")(p, *args)


def _gla_kernel(p_ref, wa_ref, ba_ref, ng_ref, o_ref, st_ref, z_ref, acc_ref):
    @pl.when(pl.program_id(1) == 0)
    def _():
        st_ref[...] = jnp.zeros(st_ref.shape, F32)

    for s in range(o_ref.shape[0] // GLA_BLOCK):
        rows = pl.ds(s * GLA_BLOCK, GLA_BLOCK)
        _gla_chunk(p_ref.at[rows], wa_ref, ba_ref, ng_ref, o_ref.at[rows], st_ref, z_ref, acc_ref)


def _gla_chunk(p_ref, wa_ref, ba_ref, ng_ref, o_ref, st_ref, z_ref, acc_ref):
    blk = o_ref.shape[0]
    qk = GLA_QK_WIDTH
    vw = GLA_WIDTH
    q = p_ref[:, 0:qk] * (GLA_KEY_DIM ** -0.5)
    k = p_ref[:, qk:2 * qk]
    g = p_ref[:, 2 * qk + vw:2 * qk + 2 * vw]
    a_lr = p_ref[:, 2 * qk + 2 * vw:GLA_IN]
    v_off = 2 * qk

    logit = _dot_x3(a_lr, wa_ref[...]) + ba_ref[...]
    log_a = -_softplus(-logit) / GLA_GATE_TEMP
    row = _iota((blk, blk), 0)
    col = _iota((blk, blk), 1)
    cum = _cumsum_rows(col <= row, log_a)

    state = st_ref[...]
    acc_ref[...] = _dot_nt((q * jnp.exp(cum)).astype(BF16), state.astype(BF16))

    spread = _block_indicator(qk, vw, GLA_KEY_DIM, GLA_VALUE_DIM).astype(BF16)
    rows = _iota((blk, qk), 0)
    group = 2 * SUBLANES
    for jb in range(blk // group):
        r0 = jb * group
        n = blk - r0
        for jj in range(group):
            j = r0 + jj
            decay = jnp.exp(jnp.where(rows[r0:] >= j, cum[r0:] - cum[j:j + 1, :], -jnp.inf))
            z_ref[jj * n:(jj + 1) * n, :] = (q[r0:] * decay * k[j:j + 1, :]).astype(BF16)
        res = jnp.dot(z_ref[0:group * n, :], spread, preferred_element_type=F32)
        part = jnp.zeros((n, vw), F32)
        for jj in range(group):
            j = r0 + jj
            part = part + res[jj * n:(jj + 1) * n] * p_ref[j:j + 1, v_off:v_off + vw]
        acc_ref[r0:blk, :] = acc_ref[r0:blk, :] + part

    v = p_ref[:, v_off:v_off + vw]
    last = cum[blk - 1:blk, :]
    k_dec = k * jnp.exp(last - cum)
    keep = _block_indicator(vw, qk, GLA_VALUE_DIM, GLA_KEY_DIM)
    st_ref[...] = state * jnp.exp(last) + keep * _dot_tn(v.astype(BF16), k_dec.astype(BF16))

    o = acc_ref[...]
    head_mean = (_block_indicator(vw, vw, GLA_VALUE_DIM, GLA_VALUE_DIM) * (1.0 / GLA_VALUE_DIM)).astype(BF16)
    ms = _dot_lhs2(o * o, head_mean)
    o_ref[...] = (o * lax.rsqrt(ms + RMS_EPS) * ng_ref[...] * _silu(g)).astype(o_ref.dtype)


def _gla_mixer(p, w_a2, b_a, norm_g):
    bsz, seq, _ = p.shape
    blk = GLA_BLOCK * CHUNKS_PER_STEP
    small = lambda a: pl.BlockSpec(a.shape, lambda b, c: (0, 0))
    args = (w_a2, b_a, norm_g)
    return pl.pallas_call(
        _gla_kernel, grid=(bsz, seq // blk),
        in_specs=[pl.BlockSpec((None, blk, GLA_IN), lambda b, c: (b, c, 0))] + [small(a) for a in args],
        out_specs=pl.BlockSpec((None, blk, GLA_WIDTH), lambda b, c: (b, c, 0)),
        out_shape=jax.ShapeDtypeStruct((bsz, seq, GLA_WIDTH), BF16),
        scratch_shapes=[pltpu.VMEM((GLA_WIDTH, GLA_QK_WIDTH), F32),
                        pltpu.VMEM((2 * SUBLANES * GLA_BLOCK, GLA_QK_WIDTH), BF16),
                        pltpu.VMEM((GLA_BLOCK, GLA_WIDTH), F32)],
        compiler_params=_params("parallel", "arbitrary"), name="gla_mixer")(p, *args)


def _outproj_kernel(ys_ref, yr_ref, yg_ref, h_ref, w1_ref, w2_ref, w3_ref, g_ref, b_ref, wr_ref, br_ref,
                    h1_ref, eid_ref, gate_ref):
    mix = (jnp.dot(ys_ref[...], w1_ref[...], preferred_element_type=F32)
           + jnp.dot(yr_ref[...], w2_ref[...], preferred_element_type=F32)
           + jnp.dot(yg_ref[...], w3_ref[...], preferred_element_type=F32))
    h1 = _layer_norm(DEEPNORM_ALPHA * h_ref[...] + mix, g_ref[...], b_ref[...])
    h1_ref[...] = h1

    logits = _dot_x3(h1, wr_ref[...]) + br_ref[...]
    tm = logits.shape[0]
    lane = _iota((tm, LANES), 1)
    lane_f = lane.astype(F32)

    def masked_softmax(mask):
        m = jnp.max(jnp.where(mask, logits, -jnp.inf), -1, keepdims=True)
        e = jnp.where(mask, jnp.exp(logits - m), 0.0)
        return e / jnp.sum(e, -1, keepdims=True)

    def first_argmax(vals, mask):
        m = jnp.max(jnp.where(mask, vals, -jnp.inf), -1, keepdims=True)
        idx = jnp.min(jnp.where(mask & (vals == m), lane_f, float(LANES)), -1, keepdims=True)
        return m, idx.astype(jnp.int32)

    is_group = lane < N_GROUPS
    g_prob = masked_softmax(is_group)
    g_w, g_idx = first_argmax(g_prob, is_group)
    e_lo = N_GROUPS + g_idx * EXPERTS_PER_GROUP
    in_group = (lane >= e_lo) & (lane < e_lo + EXPERTS_PER_GROUP)
    e_prob = masked_softmax(in_group)
    p1, i1 = first_argmax(e_prob, in_group)
    p2, i2 = first_argmax(e_prob, in_group & (lane != i1))
    denom = p1 + p2
    eid_ref[...] = jnp.where(lane == 0, i1 - N_GROUPS, jnp.where(lane == 1, i2 - N_GROUPS, 0))
    gate_ref[...] = jnp.where(lane == 0, g_w * (p1 / denom), jnp.where(lane == 1, g_w * (p2 / denom), 0.0))


def _out_projection(y_ssd, y_rwkv, y_gla, h, w_out, ln_g, ln_b, w_r, b_r):
    n_tok, d = h.shape
    tm = TOKEN_TILE
    row = lambda n: pl.BlockSpec((tm, n), lambda i: (i, 0))
    full = lambda a: pl.BlockSpec(a.shape, lambda i: (0, 0))
    w1 = w_out[0:SSD_WIDTH]
    w2 = w_out[SSD_WIDTH:SSD_WIDTH + RWKV_WIDTH]
    w3 = w_out[SSD_WIDTH + RWKV_WIDTH:MIX_WIDTH]
    consts = (w1, w2, w3, ln_g, ln_b, w_r, b_r)
    return pl.pallas_call(
        _outproj_kernel, grid=(n_tok // tm,),
        in_specs=[row(SSD_WIDTH), row(RWKV_WIDTH), row(GLA_WIDTH), row(d)] + [full(a) for a in consts],
        out_specs=[row(d), row(LANES), row(LANES)],
        out_shape=[jax.ShapeDtypeStruct((n_tok, d), F32),
                   jax.ShapeDtypeStruct((n_tok, LANES), jnp.int32),
                   jax.ShapeDtypeStruct((n_tok, LANES), F32)],
        compiler_params=_params("parallel"), name="out_projection")(y_ssd, y_rwkv, y_gla, h, *consts)


def _dest_kernel(eid_ref, dest_ref, cnt_ref, run_ref, base_ref):
    phase = pl.program_id(0)
    step = pl.program_id(1)
    tm = eid_ref.shape[0]
    e = eid_ref[...]
    lane = _iota((tm, LANES), 1)
    oh0 = (lane == e[:, 0:1]).astype(F32)
    oh1 = (lane == e[:, 1:2]).astype(F32)
    oh = oh0 + oh1
    tile_count = jnp.sum(oh, 0, keepdims=True)

    @pl.when(step == 0)
    def _():
        run_ref[...] = jnp.zeros(run_ref.shape, F32)

    @pl.when((phase == 1) & (step == 0))
    def _():
        blocks = jnp.ceil(cnt_ref[...] * (1.0 / ROW_BLOCK))
        before = (_iota((LANES, LANES), 0) < _iota((LANES, LANES), 1)).astype(BF16)
        start = jnp.dot(jnp.broadcast_to(blocks, (SUBLANES, LANES)).astype(BF16), before,
                        preferred_element_type=F32)[0:1, :]
        base_ref[...] = start * ROW_BLOCK

    @pl.when(phase == 0)
    def _():
        total = run_ref[...] + tile_count
        run_ref[...] = total
        cnt_ref[...] = total

    @pl.when(phase == 1)
    def _():
        earlier = (_iota((tm, tm), 1) < _iota((tm, tm), 0)).astype(BF16)
        rank = jnp.dot(earlier, oh.astype(BF16), preferred_element_type=F32)
        pos = rank + run_ref[...] + base_ref[...]
        d0 = jnp.sum(oh0 * pos, -1, keepdims=True)
        d1 = jnp.sum(oh1 * pos, -1, keepdims=True)
        dest_ref[...] = jnp.where(lane == 0, d0, jnp.where(lane == 1, d1, 0.0)).astype(jnp.int32)
        run_ref[...] = run_ref[...] + tile_count


def _dispatch_plan(eid):
    n_tok = eid.shape[0]
    tm = TOKEN_TILE
    return pl.pallas_call(
        _dest_kernel, grid=(2, n_tok // tm),
        in_specs=[pl.BlockSpec((tm, LANES), lambda ph, i: (i, 0))],
        out_specs=[pl.BlockSpec((tm, LANES), lambda ph, i: (i * ph, 0)),
                   pl.BlockSpec((1, LANES), lambda ph, i: (0, 0))],
        out_shape=[jax.ShapeDtypeStruct((n_tok, LANES), jnp.int32),
                   jax.ShapeDtypeStruct((1, LANES), F32)],
        scratch_shapes=[pltpu.VMEM((1, LANES), F32), pltpu.VMEM((1, LANES), F32)],
        compiler_params=_params("arbitrary", "arbitrary"), name="dispatch_plan")(eid)


def _row_copy(src_ref, src_row, dst_ref, dst_row, sem):
    return pltpu.make_async_copy(src_ref.at[pl.ds(src_row, 1)], dst_ref.at[pl.ds(dst_row, 1)], sem)


def _scatter_kernel(dest_ref, h_ref, xs_in_ref, xs_ref, sem):
    del xs_in_ref
    tm = h_ref.shape[0]

    def start(t, carry):
        for k in range(TOP_K):
            _row_copy(h_ref, t, xs_ref, dest_ref[0, 0, TOP_K * t + k], sem).start()
        return carry

    lax.fori_loop(0, tm, start, 0, unroll=DMA_UNROLL)
    all_rows = xs_ref.at[pl.ds(0, TOP_K * tm)]
    pltpu.make_async_copy(all_rows, all_rows, sem).wait()


def _scatter_rows(dest3, h1, n_rows):
    n_tok, d = h1.shape
    n_tiles = dest3.shape[0]
    xs0 = jnp.zeros((n_rows, d), h1.dtype)
    return pl.pallas_call(
        _scatter_kernel, grid=(n_tiles,),
        in_specs=[pl.BlockSpec((1, 1, dest3.shape[2]), lambda i: (i, 0, 0), memory_space=pltpu.SMEM),
                  pl.BlockSpec((n_tok // n_tiles, d), lambda i: (i, 0)), pl.BlockSpec(memory_space=pl.ANY)],
        out_specs=pl.BlockSpec(memory_space=pl.ANY),
        out_shape=jax.ShapeDtypeStruct((n_rows, d), h1.dtype),
        scratch_shapes=[pltpu.SemaphoreType.DMA(())],
        input_output_aliases={2: 0},
        compiler_params=_params("arbitrary"), name="expert_scatter")(dest3, h1, xs0)


def _expert_kernel(be_ref, used_ref, x_ref, wg_ref, wu_ref, wd_ref, y_ref, wg_b, wu_b, wd_b):
    b = pl.program_id(0)

    @pl.when((b == 0) | (be_ref[b] != be_ref[jnp.maximum(b - 1, 0)]))
    def _():
        wg_b[...] = wg_ref[...].astype(BF16)
        wu_b[...] = wu_ref[...].astype(BF16)
        wd_b[...] = wd_ref[...].astype(BF16)

    @pl.when(b < used_ref[0])
    def _():
        x = x_ref[...].astype(BF16)
        gate = jnp.dot(x, wg_b[...], preferred_element_type=F32)
        up = jnp.dot(x, wu_b[...], preferred_element_type=F32)
        mid = (_silu(gate) * up).astype(BF16)
        y_ref[...] = jnp.dot(mid, wd_b[...], preferred_element_type=F32)

    @pl.when(b >= used_ref[0])
    def _():
        y_ref[...] = jnp.zeros(y_ref.shape, y_ref.dtype)


def _expert_mlp(block_expert, n_used, xs, w_gate, w_up, w_down, layer):
    n_rows, d = xs.shape
    n_blocks = n_rows // ROW_BLOCK
    ff = w_gate.shape[-1]
    return pl.pallas_call(
        _expert_kernel,
        grid_spec=pltpu.PrefetchScalarGridSpec(
            num_scalar_prefetch=2, grid=(n_blocks,),
            in_specs=[pl.BlockSpec((ROW_BLOCK, d), lambda b, be, nu: (b, 0)),
                      pl.BlockSpec((None, None, d, ff), lambda b, be, nu: (layer, be[b], 0, 0)),
                      pl.BlockSpec((None, None, d, ff), lambda b, be, nu: (layer, be[b], 0, 0)),
                      pl.BlockSpec((None, None, ff, d), lambda b, be, nu: (layer, be[b], 0, 0))],
            out_specs=pl.BlockSpec((ROW_BLOCK, d), lambda b, be, nu: (b, 0)),
            scratch_shapes=[pltpu.VMEM((d, ff), BF16), pltpu.VMEM((d, ff), BF16), pltpu.VMEM((ff, d), BF16)]),
        out_shape=jax.ShapeDtypeStruct((n_rows, d), F32),
        compiler_params=_params("arbitrary"), name="expert_mlp")(
            block_expert, n_used, xs, w_gate, w_up, w_down)


def _combine_kernel(dest_ref, y_ref, h_ref, gate_ref, g_ref, b_ref, o_ref, buf_ref, sem):
    tm = o_ref.shape[0]

    def start(t, carry):
        for k in range(TOP_K):
            _row_copy(y_ref, dest_ref[0, 0, TOP_K * t + k], buf_ref.at[k], t, sem).start()
        return carry

    lax.fori_loop(0, tm, start, 0, unroll=DMA_UNROLL)
    pltpu.make_async_copy(buf_ref, buf_ref, sem).wait()
    gate = gate_ref[...]
    ffn = gate[:, 0:1] * buf_ref[0] + gate[:, 1:2] * buf_ref[1]
    o_ref[...] = _layer_norm(DEEPNORM_ALPHA * h_ref[...] + ffn, g_ref[...], b_ref[...])


def _combine(dest3, ys, h1, gate, ln_g, ln_b):
    n_tok, d = h1.shape
    tm = TOKEN_TILE
    row = lambda n: pl.BlockSpec((tm, n), lambda i: (i, 0))
    vec = pl.BlockSpec((1, d), lambda i: (0, 0))
    return pl.pallas_call(
        _combine_kernel, grid=(n_tok // tm,),
        in_specs=[pl.BlockSpec((1, 1, dest3.shape[2]), lambda i: (i, 0, 0), memory_space=pltpu.SMEM),
                  pl.BlockSpec(memory_space=pl.ANY), row(d), row(LANES), vec, vec],
        out_specs=row(d),
        out_shape=jax.ShapeDtypeStruct((n_tok, d), F32),
        scratch_shapes=[pltpu.VMEM((TOP_K, tm, d), F32), pltpu.SemaphoreType.DMA(())],
        compiler_params=_params("arbitrary"), name="expert_combine")(dest3, ys, h1, gate, ln_g, ln_b)


def _pad_lanes(a, n):
    return jnp.pad(a, [(0, 0)] * (a.ndim - 1) + [(0, n - a.shape[-1])])


def _row(a, n=None):
    a = a.reshape(1, -1).astype(F32)
    return a if n is None else _pad_lanes(a, n)


def _mixer_layer(h, w_in, ssd_conv_w, ssd_conv_b, ssd_dt_bias, ssd_a_log, ssd_d, ssd_norm_g,
                 rwkv_mu, rwkv_w0, rwkv_w2, rwkv_a0, rwkv_a2, rwkv_g2, rwkv_k_k, rwkv_k_a, rwkv_r_k,
                 rwkv_ln_g, rwkv_ln_b, gla_w_a2, gla_b_a, gla_norm_g, bsz, seq):
    o = 0
    cols = {}
    for name, n in (("z", SSD_WIDTH), ("xbc", SSD_XBC), ("dt", SSD_HEADS), ("rwkv", RWKV_IN),
                    ("q", GLA_QK_WIDTH), ("k", GLA_QK_WIDTH), ("v", GLA_WIDTH), ("g", GLA_WIDTH),
                    ("ga", GLA_GATE_RANK)):
        cols[name] = w_in[:, o:o + n]
        o += n
    w_ssd = jnp.concatenate([cols["z"], cols["xbc"], _pad_lanes(cols["dt"], LANES)], 1).astype(BF16)
    w_rwkv = cols["rwkv"].astype(BF16)
    w_gla = jnp.concatenate([cols["q"], cols["k"], cols["v"], cols["g"], _pad_lanes(cols["ga"], LANES)],
                            1).astype(BF16)
    p_ssd, p_rwkv, p_gla = _in_projection(h, w_ssd, w_rwkv, w_gla)

    y_ssd = _ssd_mixer(
        p_ssd.reshape(bsz, seq, SSD_IN), ssd_conv_w.astype(F32), _row(ssd_conv_b), _row(ssd_dt_bias, LANES),
        _row(-jnp.exp(ssd_a_log.astype(F32)), LANES), _row(jnp.repeat(ssd_d.astype(F32), SSD_HEAD_DIM)),
        _row(ssd_norm_g))

    w_lr = jnp.zeros((RWKV_LR, 3 * RWKV_WIDTH), F32)
    w_lr = w_lr.at[0:RWKV_DECAY_RANK, 0:RWKV_WIDTH].set(rwkv_w2)
    w_lr = w_lr.at[RWKV_DECAY_RANK:RWKV_DECAY_RANK + RWKV_ICLR_RANK, RWKV_WIDTH:2 * RWKV_WIDTH].set(rwkv_a2)
    w_lr = w_lr.at[RWKV_DECAY_RANK + RWKV_ICLR_RANK:, 2 * RWKV_WIDTH:].set(rwkv_g2)
    y_rwkv = _rwkv_mixer(
        p_rwkv.reshape(bsz, seq, RWKV_IN), _row(rwkv_mu), w_lr, _row(rwkv_w0), _row(rwkv_a0), _row(rwkv_k_k),
        _row(rwkv_k_a), _row(rwkv_r_k), _row(rwkv_ln_g), _row(rwkv_ln_b))

    w_a2 = jnp.zeros((LANES, GLA_QK_WIDTH), F32).at[0:GLA_GATE_RANK].set(gla_w_a2)
    y_gla = _gla_mixer(p_gla.reshape(bsz, seq, GLA_IN), w_a2, _row(gla_b_a), _row(gla_norm_g))

    n_tok = bsz * seq
    return (y_ssd.reshape(n_tok, SSD_WIDTH), y_rwkv.reshape(n_tok, RWKV_WIDTH),
            y_gla.reshape(n_tok, GLA_WIDTH))


def _moe_layer(h1, eid, gate, layer, w_gate, w_up, w_down, ln_g, ln_b):
    n_tok = h1.shape[0]
    tm = TOKEN_TILE
    n_blocks = n_tok * TOP_K // ROW_BLOCK + N_EXPERTS
    dest, counts = _dispatch_plan(eid)
    blocks = jnp.ceil(counts[0, 0:N_EXPERTS] / ROW_BLOCK).astype(jnp.int32)
    block_end = jnp.cumsum(blocks)
    block_ids = jnp.arange(n_blocks, dtype=jnp.int32)
    block_expert = jnp.minimum(jnp.sum((block_end[None, :] <= block_ids[:, None]).astype(jnp.int32), axis=1),
                               N_EXPERTS - 1)
    n_used = block_end[N_EXPERTS - 1:N_EXPERTS]
    dest3 = dest[:, 0:TOP_K].reshape(n_tok // tm, 1, tm * TOP_K)
    xs = _scatter_rows(dest3, h1, n_blocks * ROW_BLOCK)
    ys = _expert_mlp(block_expert, n_used, xs, w_gate, w_up, w_down, layer)
    return _combine(dest3, ys, h1, gate, ln_g, ln_b)


def kernel(x, ln_in_g, ln_in_b, w_in, ssd_conv_w, ssd_conv_b, ssd_dt_bias, ssd_a_log, ssd_d, ssd_norm_g, rwkv_mu, rwkv_w0, rwkv_w2, rwkv_a0, rwkv_a2, rwkv_g2, rwkv_k_k, rwkv_k_a, rwkv_r_k, rwkv_ln_g, rwkv_ln_b, gla_w_a2, gla_b_a, gla_norm_g, w_out, ln1_g, ln1_b, moe_w_rg, moe_b_rg, moe_w_re, moe_b_re, moe_w_gate, moe_w_up, moe_w_down, ln2_g, ln2_b):
    bsz, seq, d = x.shape
    n_tok = bsz * seq
    h = _input_ln(x.reshape(n_tok, d), ln_in_g, ln_in_b)
    for i in range(w_in.shape[0]):
        y_ssd, y_rwkv, y_gla = _mixer_layer(
            h, w_in[i], ssd_conv_w[i], ssd_conv_b[i], ssd_dt_bias[i], ssd_a_log[i], ssd_d[i], ssd_norm_g[i],
            rwkv_mu[i], rwkv_w0[i], rwkv_w2[i], rwkv_a0[i], rwkv_a2[i], rwkv_g2[i], rwkv_k_k[i],
            rwkv_k_a[i], rwkv_r_k[i], rwkv_ln_g[i], rwkv_ln_b[i], gla_w_a2[i], gla_b_a[i], gla_norm_g[i],
            bsz, seq)
        w_r = _pad_lanes(jnp.concatenate([moe_w_rg[i], moe_w_re[i]], 1).astype(F32), LANES)
        b_r = _row(jnp.concatenate([moe_b_rg[i], moe_b_re[i]]), LANES)
        h1, eid, gate = _out_projection(
            y_ssd, y_rwkv, y_gla, h, w_out[i].astype(BF16), _row(ln1_g[i]), _row(ln1_b[i]), w_r, b_r)
        h = _moe_layer(h1, eid, gate, i, moe_w_gate, moe_w_up, moe_w_down, _row(ln2_g[i]), _row(ln2_b[i]))
    return h.reshape(bsz, seq, d)
```

```python
import math

import jax
import jax.numpy as jnp
from jax import lax
from jax.experimental import pallas as pl
from jax.experimental.pallas import tpu as pltpu

F32 = jnp.float32
BF16 = jnp.bfloat16

D_MODEL = 1024
DEPTH = 2

SSD_HEAD_DIM = 64
SSD_WIDTH = D_MODEL // 2
SSD_HEADS = SSD_WIDTH // SSD_HEAD_DIM
SSD_GROUPS = 2
SSD_STATE = 64
SSD_CONV = 4
SSD_BC = SSD_GROUPS * SSD_STATE
SSD_XBC = SSD_WIDTH + 2 * SSD_BC

RWKV_HEAD_DIM = 64
RWKV_WIDTH = D_MODEL // 4
RWKV_HEADS = RWKV_WIDTH // RWKV_HEAD_DIM
RWKV_DECAY_RANK = 32
RWKV_ICLR_RANK = 32
RWKV_GATE_RANK = 64
RWKV_LR = RWKV_DECAY_RANK + RWKV_ICLR_RANK + RWKV_GATE_RANK
RWKV_IN = 3 * RWKV_WIDTH + RWKV_LR
RWKV_GN_EPS = 64e-5

GLA_VALUE_DIM = 64
GLA_WIDTH = D_MODEL // 4
GLA_HEADS = GLA_WIDTH // GLA_VALUE_DIM
GLA_KEY_DIM = GLA_VALUE_DIM // 2
GLA_QK_WIDTH = GLA_HEADS * GLA_KEY_DIM
GLA_GATE_RANK = 16
GLA_GATE_TEMP = 16.0

MIX_WIDTH = SSD_WIDTH + RWKV_WIDTH + GLA_WIDTH

N_GROUPS = 4
EXPERTS_PER_GROUP = 8
N_EXPERTS = N_GROUPS * EXPERTS_PER_GROUP
TOP_K = 2
EXPERT_FF = 512

DEEPNORM_ALPHA = (2 * DEPTH) ** 0.25
LN_EPS = 1e-5
RMS_EPS = 1e-6

LANES = 128
SUBLANES = 8
SSD_IN = SSD_WIDTH + SSD_XBC + LANES
GLA_IN = 2 * GLA_QK_WIDTH + 2 * GLA_WIDTH + LANES

TOKEN_TILE = 512
SSD_BLOCK = 128
RWKV_BLOCK = 64
GLA_BLOCK = 64
CHUNKS_PER_STEP = 4
RWKV_CHUNKS_PER_STEP = 2
ROW_BLOCK = 512
DMA_UNROLL = 8
VMEM_LIMIT = 48 * 1024 * 1024


def _dot(a, b):
    return jnp.dot(a.astype(BF16), b.astype(BF16), preferred_element_type=F32)


def _split_bf16(x):
    hi = x.astype(BF16)
    return hi, (x - hi.astype(F32)).astype(BF16)


def _dot_rhs2(a_exact, b):
    b_hi, b_lo = _split_bf16(b)
    return (jnp.dot(a_exact, b_hi, preferred_element_type=F32)
            + jnp.dot(a_exact, b_lo, preferred_element_type=F32))


def _dot_lhs2(a, b_exact):
    a_hi, a_lo = _split_bf16(a)
    return (jnp.dot(a_hi, b_exact, preferred_element_type=F32)
            + jnp.dot(a_lo, b_exact, preferred_element_type=F32))


def _cumsum_rows(mask, x):
    m = mask.astype(BF16)
    x_hi, rest = _split_bf16(x)
    x_mid, x_lo = _split_bf16(x - x_hi.astype(F32))
    del rest
    return (jnp.dot(m, x_hi, preferred_element_type=F32)
            + (jnp.dot(m, x_mid, preferred_element_type=F32) + jnp.dot(m, x_lo, preferred_element_type=F32)))


def _dot_x3(a, b):
    a_hi, a_lo = _split_bf16(a)
    b_hi, b_lo = _split_bf16(b)
    return (jnp.dot(a_hi, b_hi, preferred_element_type=F32)
            + (jnp.dot(a_hi, b_lo, preferred_element_type=F32)
               + jnp.dot(a_lo, b_hi, preferred_element_type=F32)))


def _dot_nt(a, b, precision=None):
    return lax.dot_general(a, b, (((1,), (1,)), ((), ())), precision=precision,
                           preferred_element_type=F32)


def _dot_tn(a, b, precision=None):
    return lax.dot_general(a, b, (((0,), (0,)), ((), ())), precision=precision,
                           preferred_element_type=F32)


def _sigmoid(x):
    return 1.0 / (1.0 + jnp.exp(-x))


def _silu(x):
    return x * _sigmoid(x)


def _softplus(x):
    return jnp.maximum(x, 0.0) + jnp.log1p(jnp.exp(-jnp.abs(x)))


def _iota(shape, dim):
    return lax.broadcasted_iota(jnp.int32, shape, dim)


def _block_indicator(rows, cols, row_seg, col_seg):
    r = _iota((rows, cols), 0) // row_seg
    c = _iota((rows, cols), 1) // col_seg
    return (r == c).astype(F32)


def _layer_norm(x, g, b):
    mu = jnp.mean(x, -1, keepdims=True)
    xc = x - mu
    var = jnp.mean(xc * xc, -1, keepdims=True)
    return xc * lax.rsqrt(var + LN_EPS) * g + b


def _params(*sem):
    return pltpu.CompilerParams(dimension_semantics=sem, vmem_limit_bytes=VMEM_LIMIT)


def _ln_kernel(x_ref, g_ref, b_ref, o_ref):
    o_ref[...] = _layer_norm(x_ref[...], g_ref[...], b_ref[...])


def _input_ln(x, g, b):
    n_tok, d = x.shape
    tm = TOKEN_TILE
    row = pl.BlockSpec((tm, d), lambda i: (i, 0))
    vec = pl.BlockSpec((1, d), lambda i: (0, 0))
    return pl.pallas_call(
        _ln_kernel, grid=(n_tok // tm,), in_specs=[row, vec, vec], out_specs=row,
        out_shape=jax.ShapeDtypeStruct((n_tok, d), F32), compiler_params=_params("parallel"),
        name="input_ln")(x, g.reshape(1, d), b.reshape(1, d))


def _inproj_kernel(h_ref, w1_ref, w2_ref, w3_ref, o1_ref, o2_ref, o3_ref):
    hb = h_ref[...].astype(BF16)
    o1_ref[...] = jnp.dot(hb, w1_ref[...], preferred_element_type=F32)
    o2_ref[...] = jnp.dot(hb, w2_ref[...], preferred_element_type=F32)
    o3_ref[...] = jnp.dot(hb, w3_ref[...], preferred_element_type=F32)


def _in_projection(h, w_ssd, w_rwkv, w_gla):
    n_tok, d = h.shape
    tm = TOKEN_TILE
    ws = (w_ssd, w_rwkv, w_gla)
    return pl.pallas_call(
        _inproj_kernel, grid=(n_tok // tm,),
        in_specs=[pl.BlockSpec((tm, d), lambda i: (i, 0))]
        + [pl.BlockSpec(w.shape, lambda i: (0, 0)) for w in ws],
        out_specs=[pl.BlockSpec((tm, w.shape[1]), lambda i: (i, 0)) for w in ws],
        out_shape=[jax.ShapeDtypeStruct((n_tok, w.shape[1]), F32) for w in ws],
        compiler_params=_params("parallel"), name="in_projection")(h, *ws)


def _ssd_kernel(p_ref, cw_ref, cb_ref, dtb_ref, a_ref, d_ref, ng_ref, o_ref, xs_ref, st_ref):
    @pl.when(pl.program_id(1) == 0)
    def _():
        xs_ref[0:SUBLANES, :] = jnp.zeros((SUBLANES, SSD_XBC), F32)
        st_ref[...] = jnp.zeros(st_ref.shape, F32)

    for s in range(o_ref.shape[0] // SSD_BLOCK):
        rows = pl.ds(s * SSD_BLOCK, SSD_BLOCK)
        _ssd_chunk(p_ref.at[rows], cw_ref, cb_ref, dtb_ref, a_ref, d_ref, ng_ref, o_ref.at[rows], xs_ref, st_ref)


def _ssd_chunk(p_ref, cw_ref, cb_ref, dtb_ref, a_ref, d_ref, ng_ref, o_ref, xs_ref, st_ref):
    blk = o_ref.shape[0]
    tail = SUBLANES
    z = p_ref[:, 0:SSD_WIDTH]
    xs_ref[tail:tail + blk, :] = p_ref[:, SSD_WIDTH:SSD_WIDTH + SSD_XBC]
    dt_raw = p_ref[:, SSD_WIDTH + SSD_XBC:SSD_IN]

    acc = jnp.broadcast_to(cb_ref[...], (blk, SSD_XBC))
    for i in range(SSD_CONV):
        acc = acc + cw_ref[i:i + 1, :] * xs_ref[pl.ds(tail - (SSD_CONV - 1) + i, blk), :]
    xs_ref[0:tail, :] = xs_ref[blk:blk + tail, :]
    xbc = _silu(acc)
    xh = xbc[:, 0:SSD_WIDTH]
    bm = xbc[:, SSD_WIDTH:SSD_WIDTH + SSD_BC]
    cm = xbc[:, SSD_WIDTH + SSD_BC:SSD_XBC]

    dt = _softplus(dt_raw + dtb_ref[...])
    a_dt = dt * a_ref[...]
    row = _iota((blk, blk), 0)
    col = _iota((blk, blk), 1)
    causal = col <= row
    a_cs = _cumsum_rows(causal, a_dt)
    a_cs_t = a_cs.T
    a_last = a_cs[blk - 1:blk, :]

    expand = _block_indicator(LANES, SSD_WIDTH, 1, SSD_HEAD_DIM).astype(BF16)
    per_head = jnp.concatenate([dt, jnp.exp(a_cs), jnp.exp(a_last - a_cs),
                                jnp.broadcast_to(jnp.exp(a_last), (SUBLANES, LANES))], axis=0)
    per_chan = _dot_lhs2(per_head, expand)
    dt_e = per_chan[0:blk]
    dec_e = per_chan[blk:2 * blk]
    te_e = per_chan[2 * blk:3 * blk]
    cd_e = per_chan[3 * blk:3 * blk + 1]

    x_dt = xh * dt_e
    state = st_ref[...]
    y_off = _dot(cm, state) * dec_e

    lane = _iota((blk, LANES), 1)
    hpg = SSD_HEADS // SSD_GROUPS
    pairs = []
    for j in range(SSD_HEADS // 2):
        grp = (2 * j) // hpg
        in_grp = (lane // SSD_STATE) == grp
        cb = _dot_nt(jnp.where(in_grp, cm, 0.0).astype(BF16), bm.astype(BF16))
        xp = x_dt[:, j * LANES:(j + 1) * LANES]
        ys = []
        for h in (2 * j, 2 * j + 1):
            diff = jnp.broadcast_to(a_cs[:, h:h + 1], (blk, blk)) - a_cs_t[h:h + 1, :]
            seg = jnp.exp(jnp.where(causal, diff, -jnp.inf))
            ys.append(_dot(cb * seg, xp))
        pairs.append(jnp.where(lane < SSD_HEAD_DIM, ys[0], ys[1]))
    y = jnp.concatenate(pairs, axis=1) + y_off + d_ref[...] * xh

    new = _dot_tn(bm.astype(BF16), (x_dt * te_e).astype(BF16))
    keep = _block_indicator(SSD_BC, SSD_WIDTH, SSD_STATE, SSD_WIDTH // SSD_GROUPS)
    st_ref[...] = state * cd_e + keep * new

    y = y * _silu(z)
    gw = SSD_WIDTH // SSD_GROUPS
    for g in range(SSD_GROUPS):
        yg = y[:, g * gw:(g + 1) * gw]
        ms = jnp.mean(yg * yg, -1, keepdims=True)
        o_ref[:, g * gw:(g + 1) * gw] = (yg * lax.rsqrt(ms + RMS_EPS)
                                         * ng_ref[:, g * gw:(g + 1) * gw]).astype(o_ref.dtype)


def _ssd_mixer(p, conv_w, conv_b, dt_bias, a_neg, d_skip, norm_g):
    bsz, seq, _ = p.shape
    blk = SSD_BLOCK * CHUNKS_PER_STEP
    small = lambda a: pl.BlockSpec(a.shape, lambda b, c: (0, 0))
    args = (conv_w, conv_b, dt_bias, a_neg, d_skip, norm_g)
    return pl.pallas_call(
        _ssd_kernel, grid=(bsz, seq // blk),
        in_specs=[pl.BlockSpec((None, blk, SSD_IN), lambda b, c: (b, c, 0))] + [small(a) for a in args],
        out_specs=pl.BlockSpec((None, blk, SSD_WIDTH), lambda b, c: (b, c, 0)),
        out_shape=jax.ShapeDtypeStruct((bsz, seq, SSD_WIDTH), BF16),
        scratch_shapes=[pltpu.VMEM((SSD_BLOCK + 2 * SUBLANES, SSD_XBC), F32),
                        pltpu.VMEM((SSD_BC, SSD_WIDTH), F32)],
        compiler_params=_params("parallel", "arbitrary"), name="ssd_mixer")(p, *args)


def _rwkv_kernel(p_ref, mu_ref, wlr_ref, w0_ref, a0_ref, kk_ref, ka_ref, rk_ref, lng_ref, lnb_ref,
                 o_ref, xs_ref, st_ref):
    @pl.when(pl.program_id(0) == 0)
    def _():
        xs_ref[:, 0:SUBLANES, :] = jnp.zeros((xs_ref.shape[0], SUBLANES, RWKV_IN), F32)
        st_ref[...] = jnp.zeros(st_ref.shape, F32)

    for s in range(o_ref.shape[1] // RWKV_BLOCK):
        _rwkv_chunk(p_ref, mu_ref, wlr_ref, w0_ref, a0_ref, kk_ref, ka_ref, rk_ref, lng_ref, lnb_ref,
                    o_ref, xs_ref, st_ref, s * RWKV_BLOCK)


def _rwkv_chunk(p_ref, mu_ref, wlr_ref, w0_ref, a0_ref, kk_ref, ka_ref, rk_ref, lng_ref, lnb_ref,
                o_ref, xs_ref, st_ref, frame0):
    nb = o_ref.shape[0]
    blk = RWKV_BLOCK
    rows = nb * blk
    tail = SUBLANES
    width = RWKV_WIDTH

    ps, prevs = [], []
    for b in range(nb):
        ps.append(p_ref[b, frame0:frame0 + blk, :])
        xs_ref[b, tail:tail + blk, :] = ps[b]
        prevs.append(xs_ref[b, pl.ds(tail - 1, blk), :])
        xs_ref[b, 0:tail, :] = xs_ref[b, blk:blk + tail, :]
    p = jnp.concatenate(ps, axis=0)
    prev = jnp.concatenate(prevs, axis=0)
    pm = p + (prev - p) * mu_ref[...]
    r = pm[:, 0:width]
    k = pm[:, width:2 * width]
    v = pm[:, 2 * width:3 * width]
    lr = pm[:, 3 * width:RWKV_IN]

    lane = _iota((rows, RWKV_LR), 1)
    lr_act = jnp.where(lane < RWKV_DECAY_RANK, jnp.tanh(lr),
                       jnp.where(lane < RWKV_DECAY_RANK + RWKV_ICLR_RANK, lr, _sigmoid(lr)))
    proj = _dot(lr_act, wlr_ref[...])
    w = -_softplus(-(w0_ref[...] + proj[:, 0:width])) - 0.5
    log_decay = -jnp.exp(w)
    a = _sigmoid(a0_ref[...] + proj[:, width:2 * width])
    g = proj[:, 2 * width:3 * width]

    head_sum = _block_indicator(width, width, RWKV_HEAD_DIM, RWKV_HEAD_DIM)
    head_sum_b = head_sum.astype(BF16)
    kk = k * kk_ref[...]
    kk = kk * lax.rsqrt(_dot(kk * kk, head_sum_b) + 1e-12)
    k2 = k * (1.0 + (a - 1.0) * ka_ref[...])
    alpha = -kk
    beta = kk * a

    row = _iota((rows, rows), 0)
    col = _iota((rows, rows), 1)
    same_row = (row // blk) == (col // blk)
    incl = same_row & (col <= row)
    strict = same_row & (col < row)
    cs = _dot_rhs2(incl.astype(BF16), log_decay)
    e_neg = jnp.exp(-cs)
    a_t = alpha * jnp.exp(cs - log_decay)
    b_t = (beta * e_neg).astype(BF16)
    k_t = (k2 * e_neg).astype(BF16)
    r_t = r * jnp.exp(cs)
    v_b = v.astype(BF16)

    states = [st_ref[b] for b in range(nb)]
    reads = []
    for b in range(nb):
        sl = slice(b * blk, (b + 1) * blk)
        reads.append(_dot(jnp.concatenate([a_t[sl], r_t[sl]], axis=0), states[b]))
    a_s = jnp.concatenate([x[0:blk] for x in reads], axis=0)
    r_s = jnp.concatenate([x[blk:2 * blk] for x in reads], axis=0)

    wlane = _iota((rows, width), 1)
    in_head = [(wlane // RWKV_HEAD_DIM) == h for h in range(RWKV_HEADS)]
    parts = []
    for h in range(RWKV_HEADS):
        parts += [jnp.where(in_head[h], a_t, 0.0), jnp.where(in_head[h], r_t, 0.0)]
    lhs = jnp.concatenate(parts, axis=0).astype(BF16)
    pb_all = _dot_nt(lhs, b_t)
    pk_all = _dot_nt(lhs, k_t)

    heads = range(RWKV_HEADS)
    base = [2 * h * rows for h in heads]
    npow = [jnp.where(strict, pb_all[base[h]:base[h] + rows], 0.0).astype(BF16) for h in heads]
    u = [a_s + _dot(jnp.where(strict, pk_all[base[h]:base[h] + rows], 0.0), v_b) for h in heads]
    steps = int(math.log2(blk))
    for i in range(steps):
        u = [u[h] + _dot(npow[h], u[h]) for h in heads]
        if i + 1 < steps:
            npow = [_dot(npow[h], npow[h]).astype(BF16) for h in heads]
    y = [_dot(jnp.where(incl, pb_all[base[h] + rows:base[h] + 2 * rows], 0.0), u[h])
         + _dot(jnp.where(incl, pk_all[base[h] + rows:base[h] + 2 * rows], 0.0), v_b) for h in heads]
    u_all = u[0]
    y_all = y[0]
    for h in range(1, RWKV_HEADS):
        u_all = jnp.where(in_head[h], u[h], u_all)
        y_all = jnp.where(in_head[h], y[h], y_all)
    y_all = y_all + r_s

    for b in range(nb):
        sl = slice(b * blk, (b + 1) * blk)
        c_last = cs[(b + 1) * blk - 1:(b + 1) * blk, :]
        to_end = jnp.exp(c_last - cs[sl])
        new = (_dot_tn((beta[sl] * to_end).astype(BF16), u_all[sl].astype(BF16))
               + _dot_tn((k2[sl] * to_end).astype(BF16), v_b[sl]))
        chunk_decay = jnp.broadcast_to(jnp.exp(c_last), (SUBLANES, width)).T[:, 0:1]
        st_ref[b] = states[b] * chunk_decay + head_sum * new

    head_mean_b = (head_sum * (1.0 / RWKV_HEAD_DIM)).astype(BF16)
    mean = _dot_lhs2(y_all, head_mean_b)
    yc = y_all - mean
    var = _dot(yc * yc, head_mean_b)
    yn = yc * lax.rsqrt(var + RWKV_GN_EPS) * lng_ref[...] + lnb_ref[...]
    bonus = _dot(r * k2 * rk_ref[...], head_sum_b) * v
    out = ((yn + bonus) * g).astype(o_ref.dtype)
    for b in range(nb):
        o_ref[b, frame0:frame0 + blk, :] = out[b * blk:(b + 1) * blk]


def _rwkv_mixer(p, mu, wlr, w0, a0, k_k, k_a, r_k, ln_g, ln_b):
    bsz, seq, _ = p.shape
    blk = RWKV_BLOCK * RWKV_CHUNKS_PER_STEP
    small = lambda a: pl.BlockSpec(a.shape, lambda c: (0, 0))
    args = (mu, wlr, w0, a0, k_k, k_a, r_k, ln_g, ln_b)
    return pl.pallas_call(
        _rwkv_kernel, grid=(seq // blk,),
        in_specs=[pl.BlockSpec((bsz, blk, RWKV_IN), lambda c: (0, c, 0))] + [small(a) for a in args],
        out_specs=pl.BlockSpec((bsz, blk, RWKV_WIDTH), lambda c: (0, c, 0)),
        out_shape=jax.ShapeDtypeStruct((bsz, seq, RWKV_WIDTH), BF16),
        scratch_shapes=[pltpu.VMEM((bsz, RWKV_BLOCK + 2 * SUBLANES, RWKV_IN), F32),
                        pltpu.VMEM((bsz, RWKV_WIDTH, RWKV_WIDTH), F32)],
        compiler_params=_params("arbitrary"), name="rwkv7_mixer")(p, *args)


def _gla_kernel(p_ref, wa_ref, ba_ref, ng_ref, o_ref, st_ref, z_ref, acc_ref):
    @pl.when(pl.program_id(1) == 0)
    def _():
        st_ref[...] = jnp.zeros(st_ref.shape, F32)

    for s in range(o_ref.shape[0] // GLA_BLOCK):
        rows = pl.ds(s * GLA_BLOCK, GLA_BLOCK)
        _gla_chunk(p_ref.at[rows], wa_ref, ba_ref, ng_ref, o_ref.at[rows], st_ref, z_ref, acc_ref)


def _gla_chunk(p_ref, wa_ref, ba_ref, ng_ref, o_ref, st_ref, z_ref, acc_ref):
    blk = o_ref.shape[0]
    qk = GLA_QK_WIDTH
    vw = GLA_WIDTH
    q = p_ref[:, 0:qk] * (GLA_KEY_DIM ** -0.5)
    k = p_ref[:, qk:2 * qk]
    g = p_ref[:, 2 * qk + vw:2 * qk + 2 * vw]
    a_lr = p_ref[:, 2 * qk + 2 * vw:GLA_IN]
    v_off = 2 * qk

    logit = _dot_x3(a_lr, wa_ref[...]) + ba_ref[...]
    log_a = -_softplus(-logit) / GLA_GATE_TEMP
    row = _iota((blk, blk), 0)
    col = _iota((blk, blk), 1)
    cum = _cumsum_rows(col <= row, log_a)

    state = st_ref[...]
    acc_ref[...] = _dot_nt((q * jnp.exp(cum)).astype(BF16), state.astype(BF16))

    spread = _block_indicator(qk, vw, GLA_KEY_DIM, GLA_VALUE_DIM).astype(BF16)
    rows = _iota((blk, qk), 0)
    group = 2 * SUBLANES
    for jb in range(blk // group):
        r0 = jb * group
        n = blk - r0
        for jj in range(group):
            j = r0 + jj
            decay = jnp.exp(jnp.where(rows[r0:] >= j, cum[r0:] - cum[j:j + 1, :], -jnp.inf))
            z_ref[jj * n:(jj + 1) * n, :] = (q[r0:] * decay * k[j:j + 1, :]).astype(BF16)
        res = jnp.dot(z_ref[0:group * n, :], spread, preferred_element_type=F32)
        part = jnp.zeros((n, vw), F32)
        for jj in range(group):
            j = r0 + jj
            part = part + res[jj * n:(jj + 1) * n] * p_ref[j:j + 1, v_off:v_off + vw]
        acc_ref[r0:blk, :] = acc_ref[r0:blk, :] + part

    v = p_ref[:, v_off:v_off + vw]
    last = cum[blk - 1:blk, :]
    k_dec = k * jnp.exp(last - cum)
    keep = _block_indicator(vw, qk, GLA_VALUE_DIM, GLA_KEY_DIM)
    st_ref[...] = state * jnp.exp(last) + keep * _dot_tn(v.astype(BF16), k_dec.astype(BF16))

    o = acc_ref[...]
    head_mean = (_block_indicator(vw, vw, GLA_VALUE_DIM, GLA_VALUE_DIM) * (1.0 / GLA_VALUE_DIM)).astype(BF16)
    ms = _dot_lhs2(o * o, head_mean)
    o_ref[...] = (o * lax.rsqrt(ms + RMS_EPS) * ng_ref[...] * _silu(g)).astype(o_ref.dtype)


def _gla_mixer(p, w_a2, b_a, norm_g):
    bsz, seq, _ = p.shape
    blk = GLA_BLOCK * CHUNKS_PER_STEP
    small = lambda a: pl.BlockSpec(a.shape, lambda b, c: (0, 0))
    args = (w_a2, b_a, norm_g)
    return pl.pallas_call(
        _gla_kernel, grid=(bsz, seq // blk),
        in_specs=[pl.BlockSpec((None, blk, GLA_IN), lambda b, c: (b, c, 0))] + [small(a) for a in args],
        out_specs=pl.BlockSpec((None, blk, GLA_WIDTH), lambda b, c: (b, c, 0)),
        out_shape=jax.ShapeDtypeStruct((bsz, seq, GLA_WIDTH), BF16),
        scratch_shapes=[pltpu.VMEM((GLA_WIDTH, GLA_QK_WIDTH), F32),
                        pltpu.VMEM((2 * SUBLANES * GLA_BLOCK, GLA_QK_WIDTH), BF16),
                        pltpu.VMEM((GLA_BLOCK, GLA_WIDTH), F32)],
        compiler_params=_params("parallel", "arbitrary"), name="gla_mixer")(p, *args)


def _outproj_kernel(ys_ref, yr_ref, yg_ref, h_ref, w1_ref, w2_ref, w3_ref, g_ref, b_ref, wr_ref, br_ref,
                    h1_ref, eid_ref, gate_ref):
    mix = (jnp.dot(ys_ref[...], w1_ref[...], preferred_element_type=F32)
           + jnp.dot(yr_ref[...], w2_ref[...], preferred_element_type=F32)
           + jnp.dot(yg_ref[...], w3_ref[...], preferred_element_type=F32))
    h1 = _layer_norm(DEEPNORM_ALPHA * h_ref[...] + mix, g_ref[...], b_ref[...])
    h1_ref[...] = h1

    logits = _dot_x3(h1, wr_ref[...]) + br_ref[...]
    tm = logits.shape[0]
    lane = _iota((tm, LANES), 1)
    lane_f = lane.astype(F32)

    def masked_softmax(mask):
        m = jnp.max(jnp.where(mask, logits, -jnp.inf), -1, keepdims=True)
        e = jnp.where(mask, jnp.exp(logits - m), 0.0)
        return e / jnp.sum(e, -1, keepdims=True)

    def first_argmax(vals, mask):
        m = jnp.max(jnp.where(mask, vals, -jnp.inf), -1, keepdims=True)
        idx = jnp.min(jnp.where(mask & (vals == m), lane_f, float(LANES)), -1, keepdims=True)
        return m, idx.astype(jnp.int32)

    is_group = lane < N_GROUPS
    g_prob = masked_softmax(is_group)
    g_w, g_idx = first_argmax(g_prob, is_group)
    e_lo = N_GROUPS + g_idx * EXPERTS_PER_GROUP
    in_group = (lane >= e_lo) & (lane < e_lo + EXPERTS_PER_GROUP)
    e_prob = masked_softmax(in_group)
    p1, i1 = first_argmax(e_prob, in_group)
    p2, i2 = first_argmax(e_prob, in_group & (lane != i1))
    denom = p1 + p2
    eid_ref[...] = jnp.where(lane == 0, i1 - N_GROUPS, jnp.where(lane == 1, i2 - N_GROUPS, 0))
    gate_ref[...] = jnp.where(lane == 0, g_w * (p1 / denom), jnp.where(lane == 1, g_w * (p2 / denom), 0.0))


def _out_projection(y_ssd, y_rwkv, y_gla, h, w_out, ln_g, ln_b, w_r, b_r):
    n_tok, d = h.shape
    tm = TOKEN_TILE
    row = lambda n: pl.BlockSpec((tm, n), lambda i: (i, 0))
    full = lambda a: pl.BlockSpec(a.shape, lambda i: (0, 0))
    w1 = w_out[0:SSD_WIDTH]
    w2 = w_out[SSD_WIDTH:SSD_WIDTH + RWKV_WIDTH]
    w3 = w_out[SSD_WIDTH + RWKV_WIDTH:MIX_WIDTH]
    consts = (w1, w2, w3, ln_g, ln_b, w_r, b_r)
    return pl.pallas_call(
        _outproj_kernel, grid=(n_tok // tm,),
        in_specs=[row(SSD_WIDTH), row(RWKV_WIDTH), row(GLA_WIDTH), row(d)] + [full(a) for a in consts],
        out_specs=[row(d), row(LANES), row(LANES)],
        out_shape=[jax.ShapeDtypeStruct((n_tok, d), F32),
                   jax.ShapeDtypeStruct((n_tok, LANES), jnp.int32),
                   jax.ShapeDtypeStruct((n_tok, LANES), F32)],
        compiler_params=_params("parallel"), name="out_projection")(y_ssd, y_rwkv, y_gla, h, *consts)


def _dest_kernel(eid_ref, dest_ref, cnt_ref, run_ref, base_ref):
    phase = pl.program_id(0)
    step = pl.program_id(1)
    tm = eid_ref.shape[0]
    e = eid_ref[...]
    lane = _iota((tm, LANES), 1)
    oh0 = (lane == e[:, 0:1]).astype(F32)
    oh1 = (lane == e[:, 1:2]).astype(F32)
    oh = oh0 + oh1
    tile_count = jnp.sum(oh, 0, keepdims=True)

    @pl.when(step == 0)
    def _():
        run_ref[...] = jnp.zeros(run_ref.shape, F32)

    @pl.when((phase == 1) & (step == 0))
    def _():
        blocks = jnp.ceil(cnt_ref[...] * (1.0 / ROW_BLOCK))
        before = (_iota((LANES, LANES), 0) < _iota((LANES, LANES), 1)).astype(BF16)
        start = jnp.dot(jnp.broadcast_to(blocks, (SUBLANES, LANES)).astype(BF16), before,
                        preferred_element_type=F32)[0:1, :]
        base_ref[...] = start * ROW_BLOCK

    @pl.when(phase == 0)
    def _():
        total = run_ref[...] + tile_count
        run_ref[...] = total
        cnt_ref[...] = total

    @pl.when(phase == 1)
    def _():
        earlier = (_iota((tm, tm), 1) < _iota((tm, tm), 0)).astype(BF16)
        rank = jnp.dot(earlier, oh.astype(BF16), preferred_element_type=F32)
        pos = rank + run_ref[...] + base_ref[...]
        d0 = jnp.sum(oh0 * pos, -1, keepdims=True)
        d1 = jnp.sum(oh1 * pos, -1, keepdims=True)
        dest_ref[...] = jnp.where(lane == 0, d0, jnp.where(lane == 1, d1, 0.0)).astype(jnp.int32)
        run_ref[...] = run_ref[...] + tile_count


def _dispatch_plan(eid):
    n_tok = eid.shape[0]
    tm = TOKEN_TILE
    return pl.pallas_call(
        _dest_kernel, grid=(2, n_tok // tm),
        in_specs=[pl.BlockSpec((tm, LANES), lambda ph, i: (i, 0))],
        out_specs=[pl.BlockSpec((tm, LANES), lambda ph, i: (i * ph, 0)),
                   pl.BlockSpec((1, LANES), lambda ph, i: (0, 0))],
        out_shape=[jax.ShapeDtypeStruct((n_tok, LANES), jnp.int32),
                   jax.ShapeDtypeStruct((1, LANES), F32)],
        scratch_shapes=[pltpu.VMEM((1, LANES), F32), pltpu.VMEM((1, LANES), F32)],
        compiler_params=_params("arbitrary", "arbitrary"), name="dispatch_plan")(eid)


def _row_copy(src_ref, src_row, dst_ref, dst_row, sem):
    return pltpu.make_async_copy(src_ref.at[pl.ds(src_row, 1)], dst_ref.at[pl.ds(dst_row, 1)], sem)


def _scatter_kernel(dest_ref, h_ref, xs_in_ref, xs_ref, sem):
    del xs_in_ref
    tm = h_ref.shape[0]

    def start(t, carry):
        for k in range(TOP_K):
            _row_copy(h_ref, t, xs_ref, dest_ref[0, 0, TOP_K * t + k], sem).start()
        return carry

    lax.fori_loop(0, tm, start, 0, unroll=DMA_UNROLL)
    all_rows = xs_ref.at[pl.ds(0, TOP_K * tm)]
    pltpu.make_async_copy(all_rows, all_rows, sem).wait()


def _scatter_rows(dest3, h1, n_rows):
    n_tok, d = h1.shape
    n_tiles = dest3.shape[0]
    xs0 = jnp.zeros((n_rows, d), h1.dtype)
    return pl.pallas_call(
        _scatter_kernel, grid=(n_tiles,),
        in_specs=[pl.BlockSpec((1, 1, dest3.shape[2]), lambda i: (i, 0, 0), memory_space=pltpu.SMEM),
                  pl.BlockSpec((n_tok // n_tiles, d), lambda i: (i, 0)), pl.BlockSpec(memory_space=pl.ANY)],
        out_specs=pl.BlockSpec(memory_space=pl.ANY),
        out_shape=jax.ShapeDtypeStruct((n_rows, d), h1.dtype),
        scratch_shapes=[pltpu.SemaphoreType.DMA(())],
        input_output_aliases={2: 0},
        compiler_params=_params("arbitrary"), name="expert_scatter")(dest3, h1, xs0)


def _expert_kernel(be_ref, used_ref, x_ref, wg_ref, wu_ref, wd_ref, y_ref, wg_b, wu_b, wd_b):
    b = pl.program_id(0)

    @pl.when((b == 0) | (be_ref[b] != be_ref[jnp.maximum(b - 1, 0)]))
    def _():
        wg_b[...] = wg_ref[...].astype(BF16)
        wu_b[...] = wu_ref[...].astype(BF16)
        wd_b[...] = wd_ref[...].astype(BF16)

    @pl.when(b < used_ref[0])
    def _():
        x = x_ref[...].astype(BF16)
        gate = jnp.dot(x, wg_b[...], preferred_element_type=F32)
        up = jnp.dot(x, wu_b[...], preferred_element_type=F32)
        mid = (_silu(gate) * up).astype(BF16)
        y_ref[...] = jnp.dot(mid, wd_b[...], preferred_element_type=F32)

    @pl.when(b >= used_ref[0])
    def _():
        y_ref[...] = jnp.zeros(y_ref.shape, y_ref.dtype)


def _expert_mlp(block_expert, n_used, xs, w_gate, w_up, w_down, layer):
    n_rows, d = xs.shape
    n_blocks = n_rows // ROW_BLOCK
    ff = w_gate.shape[-1]
    return pl.pallas_call(
        _expert_kernel,
        grid_spec=pltpu.PrefetchScalarGridSpec(
            num_scalar_prefetch=2, grid=(n_blocks,),
            in_specs=[pl.BlockSpec((ROW_BLOCK, d), lambda b, be, nu: (b, 0)),
                      pl.BlockSpec((None, None, d, ff), lambda b, be, nu: (layer, be[b], 0, 0)),
                      pl.BlockSpec((None, None, d, ff), lambda b, be, nu: (layer, be[b], 0, 0)),
                      pl.BlockSpec((None, None, ff, d), lambda b, be, nu: (layer, be[b], 0, 0))],
            out_specs=pl.BlockSpec((ROW_BLOCK, d), lambda b, be, nu: (b, 0)),
            scratch_shapes=[pltpu.VMEM((d, ff), BF16), pltpu.VMEM((d, ff), BF16), pltpu.VMEM((ff, d), BF16)]),
        out_shape=jax.ShapeDtypeStruct((n_rows, d), F32),
        compiler_params=_params("arbitrary"), name="expert_mlp")(
            block_expert, n_used, xs, w_gate, w_up, w_down)


def _combine_kernel(dest_ref, y_ref, h_ref, gate_ref, g_ref, b_ref, o_ref, buf_ref, sem):
    tm = o_ref.shape[0]

    def start(t, carry):
        for k in range(TOP_K):
            _row_copy(y_ref, dest_ref[0, 0, TOP_K * t + k], buf_ref.at[k], t, sem).start()
        return carry

    lax.fori_loop(0, tm, start, 0, unroll=DMA_UNROLL)
    pltpu.make_async_copy(buf_ref, buf_ref, sem).wait()
    gate = gate_ref[...]
    ffn = gate[:, 0:1] * buf_ref[0] + gate[:, 1:2] * buf_ref[1]
    o_ref[...] = _layer_norm(DEEPNORM_ALPHA * h_ref[...] + ffn, g_ref[...], b_ref[...])


def _combine(dest3, ys, h1, gate, ln_g, ln_b):
    n_tok, d = h1.shape
    tm = TOKEN_TILE
    row = lambda n: pl.BlockSpec((tm, n), lambda i: (i, 0))
    vec = pl.BlockSpec((1, d), lambda i: (0, 0))
    return pl.pallas_call(
        _combine_kernel, grid=(n_tok // tm,),
        in_specs=[pl.BlockSpec((1, 1, dest3.shape[2]), lambda i: (i, 0, 0), memory_space=pltpu.SMEM),
                  pl.BlockSpec(memory_space=pl.ANY), row(d), row(LANES), vec, vec],
        out_specs=row(d),
        out_shape=jax.ShapeDtypeStruct((n_tok, d), F32),
        scratch_shapes=[pltpu.VMEM((TOP_K, tm, d), F32), pltpu.SemaphoreType.DMA(())],
        compiler_params=_params("arbitrary"), name="expert_combine")(dest3, ys, h1, gate, ln_g, ln_b)


def _pad_lanes(a, n):
    return jnp.pad(a, [(0, 0)] * (a.ndim - 1) + [(0, n - a.shape[-1])])


def _row(a, n=None):
    a = a.reshape(1, -1).astype(F32)
    return a if n is None else _pad_lanes(a, n)


def _mixer_layer(h, w_in, ssd_conv_w, ssd_conv_b, ssd_dt_bias, ssd_a_log, ssd_d, ssd_norm_g,
                 rwkv_mu, rwkv_w0, rwkv_w2, rwkv_a0, rwkv_a2, rwkv_g2, rwkv_k_k, rwkv_k_a, rwkv_r_k,
                 rwkv_ln_g, rwkv_ln_b, gla_w_a2, gla_b_a, gla_norm_g, bsz, seq):
    o = 0
    cols = {}
    for name, n in (("z", SSD_WIDTH), ("xbc", SSD_XBC), ("dt", SSD_HEADS), ("rwkv", RWKV_IN),
                    ("q", GLA_QK_WIDTH), ("k", GLA_QK_WIDTH), ("v", GLA_WIDTH), ("g", GLA_WIDTH),
                    ("ga", GLA_GATE_RANK)):
        cols[name] = w_in[:, o:o + n]
        o += n
    w_ssd = jnp.concatenate([cols["z"], cols["xbc"], _pad_lanes(cols["dt"], LANES)], 1).astype(BF16)
    w_rwkv = cols["rwkv"].astype(BF16)
    w_gla = jnp.concatenate([cols["q"], cols["k"], cols["v"], cols["g"], _pad_lanes(cols["ga"], LANES)],
                            1).astype(BF16)
    p_ssd, p_rwkv, p_gla = _in_projection(h, w_ssd, w_rwkv, w_gla)

    y_ssd = _ssd_mixer(
        p_ssd.reshape(bsz, seq, SSD_IN), ssd_conv_w.astype(F32), _row(ssd_conv_b), _row(ssd_dt_bias, LANES),
        _row(-jnp.exp(ssd_a_log.astype(F32)), LANES), _row(jnp.repeat(ssd_d.astype(F32), SSD_HEAD_DIM)),
        _row(ssd_norm_g))

    w_lr = jnp.zeros((RWKV_LR, 3 * RWKV_WIDTH), F32)
    w_lr = w_lr.at[0:RWKV_DECAY_RANK, 0:RWKV_WIDTH].set(rwkv_w2)
    w_lr = w_lr.at[RWKV_DECAY_RANK:RWKV_DECAY_RANK + RWKV_ICLR_RANK, RWKV_WIDTH:2 * RWKV_WIDTH].set(rwkv_a2)
    w_lr = w_lr.at[RWKV_DECAY_RANK + RWKV_ICLR_RANK:, 2 * RWKV_WIDTH:].set(rwkv_g2)
    y_rwkv = _rwkv_mixer(
        p_rwkv.reshape(bsz, seq, RWKV_IN), _row(rwkv_mu), w_lr, _row(rwkv_w0), _row(rwkv_a0), _row(rwkv_k_k),
        _row(rwkv_k_a), _row(rwkv_r_k), _row(rwkv_ln_g), _row(rwkv_ln_b))

    w_a2 = jnp.zeros((LANES, GLA_QK_WIDTH), F32).at[0:GLA_GATE_RANK].set(gla_w_a2)
    y_gla = _gla_mixer(p_gla.reshape(bsz, seq, GLA_IN), w_a2, _row(gla_b_a), _row(gla_norm_g))

    n_tok = bsz * seq
    return (y_ssd.reshape(n_tok, SSD_WIDTH), y_rwkv.reshape(n_tok, RWKV_WIDTH),
            y_gla.reshape(n_tok, GLA_WIDTH))


def _moe_layer(h1, eid, gate, layer, w_gate, w_up, w_down, ln_g, ln_b):
    n_tok = h1.shape[0]
    tm = TOKEN_TILE
    n_blocks = n_tok * TOP_K // ROW_BLOCK + N_EXPERTS
    dest, counts = _dispatch_plan(eid)
    blocks = jnp.ceil(counts[0, 0:N_EXPERTS] / ROW_BLOCK).astype(jnp.int32)
    block_end = jnp.cumsum(blocks)
    block_ids = jnp.arange(n_blocks, dtype=jnp.int32)
    block_expert = jnp.minimum(jnp.sum((block_end[None, :] <= block_ids[:, None]).astype(jnp.int32), axis=1),
                               N_EXPERTS - 1)
    n_used = block_end[N_EXPERTS - 1:N_EXPERTS]
    dest3 = dest[:, 0:TOP_K].reshape(n_tok // tm, 1, tm * TOP_K)
    xs = _scatter_rows(dest3, h1, n_blocks * ROW_BLOCK)
    ys = _expert_mlp(block_expert, n_used, xs, w_gate, w_up, w_down, layer)
    return _combine(dest3, ys, h1, gate, ln_g, ln_b)


def kernel(x, ln_in_g, ln_in_b, w_in, ssd_conv_w, ssd_conv_b, ssd_dt_bias, ssd_a_log, ssd_d, ssd_norm_g, rwkv_mu, rwkv_w0, rwkv_w2, rwkv_a0, rwkv_a2, rwkv_g2, rwkv_k_k, rwkv_k_a, rwkv_r_k, rwkv_ln_g, rwkv_ln_b, gla_w_a2, gla_b_a, gla_norm_g, w_out, ln1_g, ln1_b, moe_w_rg, moe_b_rg, moe_w_re, moe_b_re, moe_w_gate, moe_w_up, moe_w_down, ln2_g, ln2_b):
    bsz, seq, d = x.shape
    n_tok = bsz * seq
    h = _input_ln(x.reshape(n_tok, d), ln_in_g, ln_in_b)
    for i in range(w_in.shape[0]):
        y_ssd, y_rwkv, y_gla = _mixer_layer(
            h, w_in[i], ssd_conv_w[i], ssd_conv_b[i], ssd_dt_bias[i], ssd_a_log[i], ssd_d[i], ssd_norm_g[i],
            rwkv_mu[i], rwkv_w0[i], rwkv_w2[i], rwkv_a0[i], rwkv_a2[i], rwkv_g2[i], rwkv_k_k[i],
            rwkv_k_a[i], rwkv_r_k[i], rwkv_ln_g[i], rwkv_ln_b[i], gla_w_a2[i], gla_b_a[i], gla_norm_g[i],
            bsz, seq)
        w_r = _pad_lanes(jnp.concatenate([moe_w_rg[i], moe_w_re[i]], 1).astype(F32), LANES)
        b_r = _row(jnp.concatenate([moe_b_rg[i], moe_b_re[i]]), LANES)
        h1, eid, gate = _out_projection(
            y_ssd, y_rwkv, y_gla, h, w_out[i].astype(BF16), _row(ln1_g[i]), _row(ln1_b[i]), w_r, b_r)
        h = _moe_layer(h1, eid, gate, i, moe_w_gate, moe_w_up, moe_w_down, _row(ln2_g[i]), _row(ln2_b[i]))
    return h.reshape(bsz, seq, d)
```

```python
import math

import jax
import jax.numpy as jnp
from jax import lax
from jax.experimental import pallas as pl
from jax.experimental.pallas import tpu as pltpu

F32 = jnp.float32
BF16 = jnp.bfloat16

D_MODEL = 1024
DEPTH = 2

SSD_HEAD_DIM = 64
SSD_WIDTH = D_MODEL // 2
SSD_HEADS = SSD_WIDTH // SSD_HEAD_DIM
SSD_GROUPS = 2
SSD_STATE = 64
SSD_CONV = 4
SSD_BC = SSD_GROUPS * SSD_STATE
SSD_XBC = SSD_WIDTH + 2 * SSD_BC

RWKV_HEAD_DIM = 64
RWKV_WIDTH = D_MODEL // 4
RWKV_HEADS = RWKV_WIDTH // RWKV_HEAD_DIM
RWKV_DECAY_RANK = 32
RWKV_ICLR_RANK = 32
RWKV_GATE_RANK = 64
RWKV_LR = RWKV_DECAY_RANK + RWKV_ICLR_RANK + RWKV_GATE_RANK
RWKV_IN = 3 * RWKV_WIDTH + RWKV_LR
RWKV_GN_EPS = 64e-5

GLA_VALUE_DIM = 64
GLA_WIDTH = D_MODEL // 4
GLA_HEADS = GLA_WIDTH // GLA_VALUE_DIM
GLA_KEY_DIM = GLA_VALUE_DIM // 2
GLA_QK_WIDTH = GLA_HEADS * GLA_KEY_DIM
GLA_GATE_RANK = 16
GLA_GATE_TEMP = 16.0

MIX_WIDTH = SSD_WIDTH + RWKV_WIDTH + GLA_WIDTH

N_GROUPS = 4
EXPERTS_PER_GROUP = 8
N_EXPERTS = N_GROUPS * EXPERTS_PER_GROUP
TOP_K = 2
EXPERT_FF = 512

DEEPNORM_ALPHA = (2 * DEPTH) ** 0.25
LN_EPS = 1e-5
RMS_EPS = 1e-6

LANES = 128
SUBLANES = 8
SSD_IN = SSD_WIDTH + SSD_XBC + LANES
GLA_IN = 2 * GLA_QK_WIDTH + 2 * GLA_WIDTH + LANES

TOKEN_TILE = 512
SSD_BLOCK = 128
RWKV_BLOCK = 64
GLA_BLOCK = 64
CHUNKS_PER_STEP = 4
RWKV_CHUNKS_PER_STEP = 2
ROW_BLOCK = 512
DMA_UNROLL = 8
VMEM_LIMIT = 48 * 1024 * 1024


def _dot(a, b):
    return jnp.dot(a.astype(BF16), b.astype(BF16), preferred_element_type=F32)


def _split_bf16(x):
    hi = x.astype(BF16)
    return hi, (x - hi.astype(F32)).astype(BF16)


def _dot_rhs2(a_exact, b):
    b_hi, b_lo = _split_bf16(b)
    return (jnp.dot(a_exact, b_hi, preferred_element_type=F32)
            + jnp.dot(a_exact, b_lo, preferred_element_type=F32))


def _dot_lhs2(a, b_exact):
    a_hi, a_lo = _split_bf16(a)
    return (jnp.dot(a_hi, b_exact, preferred_element_type=F32)
            + jnp.dot(a_lo, b_exact, preferred_element_type=F32))


def _cumsum_rows(mask, x):
    m = mask.astype(BF16)
    x_hi, rest = _split_bf16(x)
    x_mid, x_lo = _split_bf16(x - x_hi.astype(F32))
    del rest
    return (jnp.dot(m, x_hi, preferred_element_type=F32)
            + (jnp.dot(m, x_mid, preferred_element_type=F32) + jnp.dot(m, x_lo, preferred_element_type=F32)))


def _dot_x3(a, b):
    a_hi, a_lo = _split_bf16(a)
    b_hi, b_lo = _split_bf16(b)
    return (jnp.dot(a_hi, b_hi, preferred_element_type=F32)
            + (jnp.dot(a_hi, b_lo, preferred_element_type=F32)
               + jnp.dot(a_lo, b_hi, preferred_element_type=F32)))


def _pack_bf16_pair(x):
    m = x.shape[1] // 2
    lo = lax.bitcast_convert_type(x[:, 0:m].astype(BF16).astype(F32), jnp.uint32)
    hi = lax.bitcast_convert_type(x[:, m:2 * m].astype(BF16).astype(F32), jnp.uint32)
    return (lo >> 16) | (hi & jnp.uint32(0xFFFF0000))


def _unpack_bf16_pair(p):
    lo = lax.bitcast_convert_type(p << 16, F32)
    hi = lax.bitcast_convert_type(p & jnp.uint32(0xFFFF0000), F32)
    return lo, hi


def _dot_nt(a, b, precision=None):
    return lax.dot_general(a, b, (((1,), (1,)), ((), ())), precision=precision,
                           preferred_element_type=F32)


def _dot_tn(a, b, precision=None):
    return lax.dot_general(a, b, (((0,), (0,)), ((), ())), precision=precision,
                           preferred_element_type=F32)


def _sigmoid(x):
    return 1.0 / (1.0 + jnp.exp(-x))


def _silu(x):
    return x * _sigmoid(x)


def _softplus(x):
    return jnp.maximum(x, 0.0) + jnp.log1p(jnp.exp(-jnp.abs(x)))


def _iota(shape, dim):
    return lax.broadcasted_iota(jnp.int32, shape, dim)


def _block_indicator(rows, cols, row_seg, col_seg):
    r = _iota((rows, cols), 0) // row_seg
    c = _iota((rows, cols), 1) // col_seg
    return (r == c).astype(F32)


def _layer_norm(x, g, b):
    mu = jnp.mean(x, -1, keepdims=True)
    xc = x - mu
    var = jnp.mean(xc * xc, -1, keepdims=True)
    return xc * lax.rsqrt(var + LN_EPS) * g + b


def _params(*sem):
    return pltpu.CompilerParams(dimension_semantics=sem, vmem_limit_bytes=VMEM_LIMIT)


def _ln_kernel(x_ref, g_ref, b_ref, o_ref):
    o_ref[...] = _layer_norm(x_ref[...], g_ref[...], b_ref[...])


def _input_ln(x, g, b):
    n_tok, d = x.shape
    tm = TOKEN_TILE
    row = pl.BlockSpec((tm, d), lambda i: (i, 0))
    vec = pl.BlockSpec((1, d), lambda i: (0, 0))
    return pl.pallas_call(
        _ln_kernel, grid=(n_tok // tm,), in_specs=[row, vec, vec], out_specs=row,
        out_shape=jax.ShapeDtypeStruct((n_tok, d), F32), compiler_params=_params("parallel"),
        name="input_ln")(x, g.reshape(1, d), b.reshape(1, d))


def _inproj_kernel(h_ref, w1_ref, w2_ref, w3_ref, o1_ref, o2_ref, o3_ref):
    hb = h_ref[...].astype(BF16)
    o1_ref[...] = jnp.dot(hb, w1_ref[...], preferred_element_type=F32)
    o2_ref[...] = jnp.dot(hb, w2_ref[...], preferred_element_type=F32)
    o3_ref[...] = jnp.dot(hb, w3_ref[...], preferred_element_type=F32)


def _in_projection(h, w_ssd, w_rwkv, w_gla):
    n_tok, d = h.shape
    tm = TOKEN_TILE
    ws = (w_ssd, w_rwkv, w_gla)
    return pl.pallas_call(
        _inproj_kernel, grid=(n_tok // tm,),
        in_specs=[pl.BlockSpec((tm, d), lambda i: (i, 0))]
        + [pl.BlockSpec(w.shape, lambda i: (0, 0)) for w in ws],
        out_specs=[pl.BlockSpec((tm, w.shape[1]), lambda i: (i, 0)) for w in ws],
        out_shape=[jax.ShapeDtypeStruct((n_tok, w.shape[1]), F32) for w in ws],
        compiler_params=_params("parallel"), name="in_projection")(h, *ws)


def _ssd_kernel(p_ref, cw_ref, cb_ref, dtb_ref, a_ref, d_ref, ng_ref, o_ref, xs_ref, st_ref):
    @pl.when(pl.program_id(1) == 0)
    def _():
        xs_ref[0:SUBLANES, :] = jnp.zeros((SUBLANES, SSD_XBC), F32)
        st_ref[...] = jnp.zeros(st_ref.shape, F32)

    for s in range(o_ref.shape[0] // SSD_BLOCK):
        rows = pl.ds(s * SSD_BLOCK, SSD_BLOCK)
        _ssd_chunk(p_ref.at[rows], cw_ref, cb_ref, dtb_ref, a_ref, d_ref, ng_ref, o_ref.at[rows], xs_ref, st_ref)


def _ssd_chunk(p_ref, cw_ref, cb_ref, dtb_ref, a_ref, d_ref, ng_ref, o_ref, xs_ref, st_ref):
    blk = o_ref.shape[0]
    tail = SUBLANES
    z = p_ref[:, 0:SSD_WIDTH]
    xs_ref[tail:tail + blk, :] = p_ref[:, SSD_WIDTH:SSD_WIDTH + SSD_XBC]
    dt_raw = p_ref[:, SSD_WIDTH + SSD_XBC:SSD_IN]

    acc = jnp.broadcast_to(cb_ref[...], (blk, SSD_XBC))
    for i in range(SSD_CONV):
        acc = acc + cw_ref[i:i + 1, :] * xs_ref[pl.ds(tail - (SSD_CONV - 1) + i, blk), :]
    xs_ref[0:tail, :] = xs_ref[blk:blk + tail, :]
    xbc = _silu(acc)
    xh = xbc[:, 0:SSD_WIDTH]
    bm = xbc[:, SSD_WIDTH:SSD_WIDTH + SSD_BC]
    cm = xbc[:, SSD_WIDTH + SSD_BC:SSD_XBC]

    dt = _softplus(dt_raw + dtb_ref[...])
    a_dt = dt * a_ref[...]
    row = _iota((blk, blk), 0)
    col = _iota((blk, blk), 1)
    causal = col <= row
    a_cs = _cumsum_rows(causal, a_dt)
    a_cs_t = a_cs.T
    a_last = a_cs[blk - 1:blk, :]

    expand = _block_indicator(LANES, SSD_WIDTH, 1, SSD_HEAD_DIM).astype(BF16)
    per_head = jnp.concatenate([dt, jnp.exp(a_cs), jnp.exp(a_last - a_cs),
                                jnp.broadcast_to(jnp.exp(a_last), (SUBLANES, LANES))], axis=0)
    per_chan = _dot_lhs2(per_head, expand)
    dt_e = per_chan[0:blk]
    dec_e = per_chan[blk:2 * blk]
    te_e = per_chan[2 * blk:3 * blk]
    cd_e = per_chan[3 * blk:3 * blk + 1]

    x_dt = xh * dt_e
    state = st_ref[...]
    y_off = _dot(cm, state) * dec_e

    lane = _iota((blk, LANES), 1)
    hpg = SSD_HEADS // SSD_GROUPS
    pairs = []
    for j in range(SSD_HEADS // 2):
        grp = (2 * j) // hpg
        in_grp = (lane // SSD_STATE) == grp
        cb = _dot_nt(jnp.where(in_grp, cm, 0.0).astype(BF16), bm.astype(BF16))
        xp = x_dt[:, j * LANES:(j + 1) * LANES]
        ys = []
        for h in (2 * j, 2 * j + 1):
            diff = jnp.broadcast_to(a_cs[:, h:h + 1], (blk, blk)) - a_cs_t[h:h + 1, :]
            seg = jnp.exp(jnp.where(causal, diff, -jnp.inf))
            ys.append(_dot(cb * seg, xp))
        pairs.append(jnp.where(lane < SSD_HEAD_DIM, ys[0], ys[1]))
    y = jnp.concatenate(pairs, axis=1) + y_off + d_ref[...] * xh

    new = _dot_tn(bm.astype(BF16), (x_dt * te_e).astype(BF16))
    keep = _block_indicator(SSD_BC, SSD_WIDTH, SSD_STATE, SSD_WIDTH // SSD_GROUPS)
    st_ref[...] = state * cd_e + keep * new

    y = y * _silu(z)
    gw = SSD_WIDTH // SSD_GROUPS
    for g in range(SSD_GROUPS):
        yg = y[:, g * gw:(g + 1) * gw]
        ms = jnp.mean(yg * yg, -1, keepdims=True)
        o_ref[:, g * gw:(g + 1) * gw] = (yg * lax.rsqrt(ms + RMS_EPS)
                                         * ng_ref[:, g * gw:(g + 1) * gw]).astype(o_ref.dtype)


def _ssd_mixer(p, conv_w, conv_b, dt_bias, a_neg, d_skip, norm_g):
    bsz, seq, _ = p.shape
    blk = SSD_BLOCK * CHUNKS_PER_STEP
    small = lambda a: pl.BlockSpec(a.shape, lambda b, c: (0, 0))
    args = (conv_w, conv_b, dt_bias, a_neg, d_skip, norm_g)
    return pl.pallas_call(
        _ssd_kernel, grid=(bsz, seq // blk),
        in_specs=[pl.BlockSpec((None, blk, SSD_IN), lambda b, c: (b, c, 0))] + [small(a) for a in args],
        out_specs=pl.BlockSpec((None, blk, SSD_WIDTH), lambda b, c: (b, c, 0)),
        out_shape=jax.ShapeDtypeStruct((bsz, seq, SSD_WIDTH), BF16),
        scratch_shapes=[pltpu.VMEM((SSD_BLOCK + 2 * SUBLANES, SSD_XBC), F32),
                        pltpu.VMEM((SSD_BC, SSD_WIDTH), F32)],
        compiler_params=_params("parallel", "arbitrary"), name="ssd_mixer")(p, *args)


def _rwkv_kernel(p_ref, mu_ref, wlr_ref, w0_ref, a0_ref, kk_ref, ka_ref, rk_ref, lng_ref, lnb_ref,
                 o_ref, xs_ref, st_ref):
    @pl.when(pl.program_id(0) == 0)
    def _():
        xs_ref[:, 0:SUBLANES, :] = jnp.zeros((xs_ref.shape[0], SUBLANES, RWKV_IN), F32)
        st_ref[...] = jnp.zeros(st_ref.shape, F32)

    for s in range(o_ref.shape[1] // RWKV_BLOCK):
        _rwkv_chunk(p_ref, mu_ref, wlr_ref, w0_ref, a0_ref, kk_ref, ka_ref, rk_ref, lng_ref, lnb_ref,
                    o_ref, xs_ref, st_ref, s * RWKV_BLOCK)


def _rwkv_chunk(p_ref, mu_ref, wlr_ref, w0_ref, a0_ref, kk_ref, ka_ref, rk_ref, lng_ref, lnb_ref,
                o_ref, xs_ref, st_ref, frame0):
    nb = o_ref.shape[0]
    blk = RWKV_BLOCK
    rows = nb * blk
    tail = SUBLANES
    width = RWKV_WIDTH

    ps, prevs = [], []
    for b in range(nb):
        ps.append(p_ref[b, frame0:frame0 + blk, :])
        xs_ref[b, tail:tail + blk, :] = ps[b]
        prevs.append(xs_ref[b, pl.ds(tail - 1, blk), :])
        xs_ref[b, 0:tail, :] = xs_ref[b, blk:blk + tail, :]
    p = jnp.concatenate(ps, axis=0)
    prev = jnp.concatenate(prevs, axis=0)
    pm = p + (prev - p) * mu_ref[...]
    r = pm[:, 0:width]
    k = pm[:, width:2 * width]
    v = pm[:, 2 * width:3 * width]
    lr = pm[:, 3 * width:RWKV_IN]

    lane = _iota((rows, RWKV_LR), 1)
    lr_act = jnp.where(lane < RWKV_DECAY_RANK, jnp.tanh(lr),
                       jnp.where(lane < RWKV_DECAY_RANK + RWKV_ICLR_RANK, lr, _sigmoid(lr)))
    proj = _dot(lr_act, wlr_ref[...])
    w = -_softplus(-(w0_ref[...] + proj[:, 0:width])) - 0.5
    log_decay = -jnp.exp(w)
    a = _sigmoid(a0_ref[...] + proj[:, width:2 * width])
    g = proj[:, 2 * width:3 * width]

    head_sum = _block_indicator(width, width, RWKV_HEAD_DIM, RWKV_HEAD_DIM)
    head_sum_b = head_sum.astype(BF16)
    kk = k * kk_ref[...]
    kk = kk * lax.rsqrt(_dot(kk * kk, head_sum_b) + 1e-12)
    k2 = k * (1.0 + (a - 1.0) * ka_ref[...])
    alpha = -kk
    beta = kk * a

    row = _iota((rows, rows), 0)
    col = _iota((rows, rows), 1)
    same_row = (row // blk) == (col // blk)
    incl = same_row & (col <= row)
    strict = same_row & (col < row)
    cs = _dot_rhs2(incl.astype(BF16), log_decay)
    e_neg = jnp.exp(-cs)
    a_t = alpha * jnp.exp(cs - log_decay)
    b_t = (beta * e_neg).astype(BF16)
    k_t = (k2 * e_neg).astype(BF16)
    r_t = r * jnp.exp(cs)
    v_b = v.astype(BF16)

    states = [st_ref[b] for b in range(nb)]
    reads = []
    for b in range(nb):
        sl = slice(b * blk, (b + 1) * blk)
        reads.append(_dot(jnp.concatenate([a_t[sl], r_t[sl]], axis=0), states[b]))
    a_s = jnp.concatenate([x[0:blk] for x in reads], axis=0)
    r_s = jnp.concatenate([x[blk:2 * blk] for x in reads], axis=0)

    wlane = _iota((rows, width), 1)
    in_head = [(wlane // RWKV_HEAD_DIM) == h for h in range(RWKV_HEADS)]
    parts = []
    for h in range(RWKV_HEADS):
        parts += [jnp.where(in_head[h], a_t, 0.0), jnp.where(in_head[h], r_t, 0.0)]
    lhs = jnp.concatenate(parts, axis=0).astype(BF16)
    pb_all = _dot_nt(lhs, b_t)
    pk_all = _dot_nt(lhs, k_t)

    heads = range(RWKV_HEADS)
    base = [2 * h * rows for h in heads]
    npow = [jnp.where(strict, pb_all[base[h]:base[h] + rows], 0.0).astype(BF16) for h in heads]
    u = [a_s + _dot(jnp.where(strict, pk_all[base[h]:base[h] + rows], 0.0), v_b) for h in heads]
    steps = int(math.log2(blk))
    for i in range(steps):
        u = [u[h] + _dot(npow[h], u[h]) for h in heads]
        if i + 1 < steps:
            npow = [_dot(npow[h], npow[h]).astype(BF16) for h in heads]
    y = [_dot(jnp.where(incl, pb_all[base[h] + rows:base[h] + 2 * rows], 0.0), u[h])
         + _dot(jnp.where(incl, pk_all[base[h] + rows:base[h] + 2 * rows], 0.0), v_b) for h in heads]
    u_all = u[0]
    y_all = y[0]
    for h in range(1, RWKV_HEADS):
        u_all = jnp.where(in_head[h], u[h], u_all)
        y_all = jnp.where(in_head[h], y[h], y_all)
    y_all = y_all + r_s

    for b in range(nb):
        sl = slice(b * blk, (b + 1) * blk)
        c_last = cs[(b + 1) * blk - 1:(b + 1) * blk, :]
        to_end = jnp.exp(c_last - cs[sl])
        new = (_dot_tn((beta[sl] * to_end).astype(BF16), u_all[sl].astype(BF16))
               + _dot_tn((k2[sl] * to_end).astype(BF16), v_b[sl]))
        chunk_decay = jnp.broadcast_to(jnp.exp(c_last), (SUBLANES, width)).T[:, 0:1]
        st_ref[b] = states[b] * chunk_decay + head_sum * new

    head_mean_b = (head_sum * (1.0 / RWKV_HEAD_DIM)).astype(BF16)
    mean = _dot_lhs2(y_all, head_mean_b)
    yc = y_all - mean
    var = _dot(yc * yc, head_mean_b)
    yn = yc * lax.rsqrt(var + RWKV_GN_EPS) * lng_ref[...] + lnb_ref[...]
    bonus = _dot(r * k2 * rk_ref[...], head_sum_b) * v
    out = ((yn + bonus) * g).astype(o_ref.dtype)
    for b in range(nb):
        o_ref[b, frame0:frame0 + blk, :] = out[b * blk:(b + 1) * blk]


def _rwkv_mixer(p, mu, wlr, w0, a0, k_k, k_a, r_k, ln_g, ln_b):
    bsz, seq, _ = p.shape
    blk = RWKV_BLOCK * RWKV_CHUNKS_PER_STEP
    small = lambda a: pl.BlockSpec(a.shape, lambda c: (0, 0))
    args = (mu, wlr, w0, a0, k_k, k_a, r_k, ln_g, ln_b)
    return pl.pallas_call(
        _rwkv_kernel, grid=(seq // blk,),
        in_specs=[pl.BlockSpec((bsz, blk, RWKV_IN), lambda c: (0, c, 0))] + [small(a) for a in args],
        out_specs=pl.BlockSpec((bsz, blk, RWKV_WIDTH), lambda c: (0, c, 0)),
        out_shape=jax.ShapeDtypeStruct((bsz, seq, RWKV_WIDTH), BF16),
        scratch_shapes=[pltpu.VMEM((bsz, RWKV_BLOCK + 2 * SUBLANES, RWKV_IN), F32),
                        pltpu.VMEM((bsz, RWKV_WIDTH, RWKV_WIDTH), F32)],
        compiler_params=_params("arbitrary"), name="rwkv7_mixer")(p, *args)


def _gla_kernel(p_ref, wa_ref, ba_ref, ng_ref, o_ref, st_ref, z_ref, acc_ref):
    @pl.when(pl.program_id(1) == 0)
    def _():
        st_ref[...] = jnp.zeros(st_ref.shape, F32)

    for s in range(o_ref.shape[0] // GLA_BLOCK):
        rows = pl.ds(s * GLA_BLOCK, GLA_BLOCK)
        _gla_chunk(p_ref.at[rows], wa_ref, ba_ref, ng_ref, o_ref.at[rows], st_ref, z_ref, acc_ref)


def _gla_chunk(p_ref, wa_ref, ba_ref, ng_ref, o_ref, st_ref, z_ref, acc_ref):
    blk = o_ref.shape[0]
    qk = GLA_QK_WIDTH
    vw = GLA_WIDTH
    q = p_ref[:, 0:qk] * (GLA_KEY_DIM ** -0.5)
    k = p_ref[:, qk:2 * qk]
    g = p_ref[:, 2 * qk + vw:2 * qk + 2 * vw]
    a_lr = p_ref[:, 2 * qk + 2 * vw:GLA_IN]
    v_off = 2 * qk

    logit = _dot_x3(a_lr, wa_ref[...]) + ba_ref[...]
    log_a = -_softplus(-logit) / GLA_GATE_TEMP
    row = _iota((blk, blk), 0)
    col = _iota((blk, blk), 1)
    cum = _cumsum_rows(col <= row, log_a)

    state = st_ref[...]
    acc_ref[...] = _dot_nt((q * jnp.exp(cum)).astype(BF16), state.astype(BF16))

    spread = _block_indicator(qk, vw, GLA_KEY_DIM, GLA_VALUE_DIM).astype(BF16)
    rows = _iota((blk, qk), 0)
    group = 2 * SUBLANES
    for jb in range(blk // group):
        r0 = jb * group
        n = blk - r0
        for jj in range(group):
            j = r0 + jj
            decay = jnp.exp(jnp.where(rows[r0:] >= j, cum[r0:] - cum[j:j + 1, :], -jnp.inf))
            z_ref[jj * n:(jj + 1) * n, :] = (q[r0:] * decay * k[j:j + 1, :]).astype(BF16)
        res = jnp.dot(z_ref[0:group * n, :], spread, preferred_element_type=F32)
        part = jnp.zeros((n, vw), F32)
        for jj in range(group):
            j = r0 + jj
            part = part + res[jj * n:(jj + 1) * n] * p_ref[j:j + 1, v_off:v_off + vw]
        acc_ref[r0:blk, :] = acc_ref[r0:blk, :] + part

    v = p_ref[:, v_off:v_off + vw]
    last = cum[blk - 1:blk, :]
    k_dec = k * jnp.exp(last - cum)
    keep = _block_indicator(vw, qk, GLA_VALUE_DIM, GLA_KEY_DIM)
    st_ref[...] = state * jnp.exp(last) + keep * _dot_tn(v.astype(BF16), k_dec.astype(BF16))

    o = acc_ref[...]
    head_mean = (_block_indicator(vw, vw, GLA_VALUE_DIM, GLA_VALUE_DIM) * (1.0 / GLA_VALUE_DIM)).astype(BF16)
    ms = _dot_lhs2(o * o, head_mean)
    o_ref[...] = (o * lax.rsqrt(ms + RMS_EPS) * ng_ref[...] * _silu(g)).astype(o_ref.dtype)


def _gla_mixer(p, w_a2, b_a, norm_g):
    bsz, seq, _ = p.shape
    blk = GLA_BLOCK * CHUNKS_PER_STEP
    small = lambda a: pl.BlockSpec(a.shape, lambda b, c: (0, 0))
    args = (w_a2, b_a, norm_g)
    return pl.pallas_call(
        _gla_kernel, grid=(bsz, seq // blk),
        in_specs=[pl.BlockSpec((None, blk, GLA_IN), lambda b, c: (b, c, 0))] + [small(a) for a in args],
        out_specs=pl.BlockSpec((None, blk, GLA_WIDTH), lambda b, c: (b, c, 0)),
        out_shape=jax.ShapeDtypeStruct((bsz, seq, GLA_WIDTH), BF16),
        scratch_shapes=[pltpu.VMEM((GLA_WIDTH, GLA_QK_WIDTH), F32),
                        pltpu.VMEM((2 * SUBLANES * GLA_BLOCK, GLA_QK_WIDTH), BF16),
                        pltpu.VMEM((GLA_BLOCK, GLA_WIDTH), F32)],
        compiler_params=_params("parallel", "arbitrary"), name="gla_mixer")(p, *args)


def _outproj_kernel(ys_ref, yr_ref, yg_ref, h_ref, w1_ref, w2_ref, w3_ref, g_ref, b_ref, wr_ref, br_ref,
                    h1_ref, h1p_ref, eid_ref, gate_ref):
    mix = (jnp.dot(ys_ref[...], w1_ref[...], preferred_element_type=F32)
           + jnp.dot(yr_ref[...], w2_ref[...], preferred_element_type=F32)
           + jnp.dot(yg_ref[...], w3_ref[...], preferred_element_type=F32))
    h1 = _layer_norm(DEEPNORM_ALPHA * h_ref[...] + mix, g_ref[...], b_ref[...])
    h1_ref[...] = h1
    h1p_ref[...] = _pack_bf16_pair(h1)

    logits = _dot_x3(h1, wr_ref[...]) + br_ref[...]
    tm = logits.shape[0]
    lane = _iota((tm, LANES), 1)
    lane_f = lane.astype(F32)

    def masked_softmax(mask):
        m = jnp.max(jnp.where(mask, logits, -jnp.inf), -1, keepdims=True)
        e = jnp.where(mask, jnp.exp(logits - m), 0.0)
        return e / jnp.sum(e, -1, keepdims=True)

    def first_argmax(vals, mask):
        m = jnp.max(jnp.where(mask, vals, -jnp.inf), -1, keepdims=True)
        idx = jnp.min(jnp.where(mask & (vals == m), lane_f, float(LANES)), -1, keepdims=True)
        return m, idx.astype(jnp.int32)

    is_group = lane < N_GROUPS
    g_prob = masked_softmax(is_group)
    g_w, g_idx = first_argmax(g_prob, is_group)
    e_lo = N_GROUPS + g_idx * EXPERTS_PER_GROUP
    in_group = (lane >= e_lo) & (lane < e_lo + EXPERTS_PER_GROUP)
    e_prob = masked_softmax(in_group)
    p1, i1 = first_argmax(e_prob, in_group)
    p2, i2 = first_argmax(e_prob, in_group & (lane != i1))
    denom = p1 + p2
    eid_ref[...] = jnp.where(lane == 0, i1 - N_GROUPS, jnp.where(lane == 1, i2 - N_GROUPS, 0))
    gate_ref[...] = jnp.where(lane == 0, g_w * (p1 / denom), jnp.where(lane == 1, g_w * (p2 / denom), 0.0))


def _out_projection(y_ssd, y_rwkv, y_gla, h, w_out, ln_g, ln_b, w_r, b_r):
    n_tok, d = h.shape
    tm = TOKEN_TILE
    row = lambda n: pl.BlockSpec((tm, n), lambda i: (i, 0))
    full = lambda a: pl.BlockSpec(a.shape, lambda i: (0, 0))
    w1 = w_out[0:SSD_WIDTH]
    w2 = w_out[SSD_WIDTH:SSD_WIDTH + RWKV_WIDTH]
    w3 = w_out[SSD_WIDTH + RWKV_WIDTH:MIX_WIDTH]
    consts = (w1, w2, w3, ln_g, ln_b, w_r, b_r)
    return pl.pallas_call(
        _outproj_kernel, grid=(n_tok // tm,),
        in_specs=[row(SSD_WIDTH), row(RWKV_WIDTH), row(GLA_WIDTH), row(d)] + [full(a) for a in consts],
        out_specs=[row(d), row(d // 2), row(LANES), row(LANES)],
        out_shape=[jax.ShapeDtypeStruct((n_tok, d), F32),
                   jax.ShapeDtypeStruct((n_tok, d // 2), jnp.uint32),
                   jax.ShapeDtypeStruct((n_tok, LANES), jnp.int32),
                   jax.ShapeDtypeStruct((n_tok, LANES), F32)],
        compiler_params=_params("parallel"), name="out_projection")(y_ssd, y_rwkv, y_gla, h, *consts)


def _dest_kernel(eid_ref, dest_ref, cnt_ref, run_ref, base_ref):
    phase = pl.program_id(0)
    step = pl.program_id(1)
    tm = eid_ref.shape[0]
    e = eid_ref[...]
    lane = _iota((tm, LANES), 1)
    oh0 = (lane == e[:, 0:1]).astype(F32)
    oh1 = (lane == e[:, 1:2]).astype(F32)
    oh = oh0 + oh1
    tile_count = jnp.sum(oh, 0, keepdims=True)

    @pl.when(step == 0)
    def _():
        run_ref[...] = jnp.zeros(run_ref.shape, F32)

    @pl.when((phase == 1) & (step == 0))
    def _():
        blocks = jnp.ceil(cnt_ref[...] * (1.0 / ROW_BLOCK))
        before = (_iota((LANES, LANES), 0) < _iota((LANES, LANES), 1)).astype(BF16)
        start = jnp.dot(jnp.broadcast_to(blocks, (SUBLANES, LANES)).astype(BF16), before,
                        preferred_element_type=F32)[0:1, :]
        base_ref[...] = start * ROW_BLOCK

    @pl.when(phase == 0)
    def _():
        total = run_ref[...] + tile_count
        run_ref[...] = total
        cnt_ref[...] = total

    @pl.when(phase == 1)
    def _():
        earlier = (_iota((tm, tm), 1) < _iota((tm, tm), 0)).astype(BF16)
        rank = jnp.dot(earlier, oh.astype(BF16), preferred_element_type=F32)
        pos = rank + run_ref[...] + base_ref[...]
        d0 = jnp.sum(oh0 * pos, -1, keepdims=True)
        d1 = jnp.sum(oh1 * pos, -1, keepdims=True)
        dest_ref[...] = jnp.where(lane == 0, d0, jnp.where(lane == 1, d1, 0.0)).astype(jnp.int32)
        run_ref[...] = run_ref[...] + tile_count


def _dispatch_plan(eid):
    n_tok = eid.shape[0]
    tm = TOKEN_TILE
    return pl.pallas_call(
        _dest_kernel, grid=(2, n_tok // tm),
        in_specs=[pl.BlockSpec((tm, LANES), lambda ph, i: (i, 0))],
        out_specs=[pl.BlockSpec((tm, LANES), lambda ph, i: (i * ph, 0)),
                   pl.BlockSpec((1, LANES), lambda ph, i: (0, 0))],
        out_shape=[jax.ShapeDtypeStruct((n_tok, LANES), jnp.int32),
                   jax.ShapeDtypeStruct((1, LANES), F32)],
        scratch_shapes=[pltpu.VMEM((1, LANES), F32), pltpu.VMEM((1, LANES), F32)],
        compiler_params=_params("arbitrary", "arbitrary"), name="dispatch_plan")(eid)


def _row_copy(src_ref, src_row, dst_ref, dst_row, sem):
    return pltpu.make_async_copy(src_ref.at[pl.ds(src_row, 1)], dst_ref.at[pl.ds(dst_row, 1)], sem)


def _scatter_kernel(dest_ref, h_ref, xs_in_ref, xs_ref, sem):
    del xs_in_ref
    tm = h_ref.shape[0]

    def start(t, carry):
        for k in range(TOP_K):
            _row_copy(h_ref, t, xs_ref, dest_ref[0, 0, TOP_K * t + k], sem).start()
        return carry

    lax.fori_loop(0, tm, start, 0, unroll=DMA_UNROLL)
    all_rows = xs_ref.at[pl.ds(0, TOP_K * tm)]
    pltpu.make_async_copy(all_rows, all_rows, sem).wait()


def _scatter_rows(dest3, h1, n_rows):
    n_tok, d = h1.shape
    n_tiles = dest3.shape[0]
    xs0 = jnp.zeros((n_rows, d), h1.dtype)
    return pl.pallas_call(
        _scatter_kernel, grid=(n_tiles,),
        in_specs=[pl.BlockSpec((1, 1, dest3.shape[2]), lambda i: (i, 0, 0), memory_space=pltpu.SMEM),
                  pl.BlockSpec((n_tok // n_tiles, d), lambda i: (i, 0)), pl.BlockSpec(memory_space=pl.ANY)],
        out_specs=pl.BlockSpec(memory_space=pl.ANY),
        out_shape=jax.ShapeDtypeStruct((n_rows, d), h1.dtype),
        scratch_shapes=[pltpu.SemaphoreType.DMA(())],
        input_output_aliases={2: 0},
        compiler_params=_params("arbitrary"), name="expert_scatter")(dest3, h1, xs0)


def _expert_kernel(be_ref, used_ref, x_ref, wg_ref, wu_ref, wd_ref, y_ref, wg_b, wu_b, wd_b):
    b = pl.program_id(0)

    @pl.when((b == 0) | (be_ref[b] != be_ref[jnp.maximum(b - 1, 0)]))
    def _():
        wg_b[...] = wg_ref[...].astype(BF16)
        wu_b[...] = wu_ref[...].astype(BF16)
        wd_b[...] = wd_ref[...].astype(BF16)

    @pl.when(b < used_ref[0])
    def _():
        x_lo, x_hi = _unpack_bf16_pair(x_ref[...])
        x = jnp.concatenate([x_lo.astype(BF16), x_hi.astype(BF16)], axis=1)
        gate = jnp.dot(x, wg_b[...], preferred_element_type=F32)
        up = jnp.dot(x, wu_b[...], preferred_element_type=F32)
        mid = (_silu(gate) * up).astype(BF16)
        y_ref[...] = _pack_bf16_pair(jnp.dot(mid, wd_b[...], preferred_element_type=F32))

    @pl.when(b >= used_ref[0])
    def _():
        y_ref[...] = jnp.zeros(y_ref.shape, y_ref.dtype)


def _expert_mlp(block_expert, n_used, xs, w_gate, w_up, w_down, layer):
    n_rows, dp = xs.shape
    n_blocks = n_rows // ROW_BLOCK
    d, ff = w_gate.shape[-2:]
    return pl.pallas_call(
        _expert_kernel,
        grid_spec=pltpu.PrefetchScalarGridSpec(
            num_scalar_prefetch=2, grid=(n_blocks,),
            in_specs=[pl.BlockSpec((ROW_BLOCK, dp), lambda b, be, nu: (b, 0)),
                      pl.BlockSpec((None, None, d, ff), lambda b, be, nu: (layer, be[b], 0, 0)),
                      pl.BlockSpec((None, None, d, ff), lambda b, be, nu: (layer, be[b], 0, 0)),
                      pl.BlockSpec((None, None, ff, d), lambda b, be, nu: (layer, be[b], 0, 0))],
            out_specs=pl.BlockSpec((ROW_BLOCK, dp), lambda b, be, nu: (b, 0)),
            scratch_shapes=[pltpu.VMEM((d, ff), BF16), pltpu.VMEM((d, ff), BF16), pltpu.VMEM((ff, d), BF16)]),
        out_shape=jax.ShapeDtypeStruct((n_rows, dp), jnp.uint32),
        compiler_params=_params("arbitrary"), name="expert_mlp")(
            block_expert, n_used, xs, w_gate, w_up, w_down)


def _combine_kernel(dest_ref, y_ref, h_ref, gate_ref, g_ref, b_ref, o_ref, buf_ref, sem):
    tm = o_ref.shape[0]

    def start(t, carry):
        for k in range(TOP_K):
            _row_copy(y_ref, dest_ref[0, 0, TOP_K * t + k], buf_ref.at[k], t, sem).start()
        return carry

    lax.fori_loop(0, tm, start, 0, unroll=DMA_UNROLL)
    pltpu.make_async_copy(buf_ref, buf_ref, sem).wait()
    gate = gate_ref[...]
    lo0, hi0 = _unpack_bf16_pair(buf_ref[0])
    lo1, hi1 = _unpack_bf16_pair(buf_ref[1])
    g0 = gate[:, 0:1]
    g1 = gate[:, 1:2]
    ffn = jnp.concatenate([g0 * lo0 + g1 * lo1, g0 * hi0 + g1 * hi1], axis=1)
    o_ref[...] = _layer_norm(DEEPNORM_ALPHA * h_ref[...] + ffn, g_ref[...], b_ref[...])


def _combine(dest3, ys, h1, gate, ln_g, ln_b):
    n_tok, d = h1.shape
    tm = TOKEN_TILE
    row = lambda n: pl.BlockSpec((tm, n), lambda i: (i, 0))
    vec = pl.BlockSpec((1, d), lambda i: (0, 0))
    return pl.pallas_call(
        _combine_kernel, grid=(n_tok // tm,),
        in_specs=[pl.BlockSpec((1, 1, dest3.shape[2]), lambda i: (i, 0, 0), memory_space=pltpu.SMEM),
                  pl.BlockSpec(memory_space=pl.ANY), row(d), row(LANES), vec, vec],
        out_specs=row(d),
        out_shape=jax.ShapeDtypeStruct((n_tok, d), F32),
        scratch_shapes=[pltpu.VMEM((TOP_K, tm, ys.shape[1]), ys.dtype), pltpu.SemaphoreType.DMA(())],
        compiler_params=_params("arbitrary"), name="expert_combine")(dest3, ys, h1, gate, ln_g, ln_b)


def _pad_lanes(a, n):
    return jnp.pad(a, [(0, 0)] * (a.ndim - 1) + [(0, n - a.shape[-1])])


def _row(a, n=None):
    a = a.reshape(1, -1).astype(F32)
    return a if n is None else _pad_lanes(a, n)


def _mixer_layer(h, w_in, ssd_conv_w, ssd_conv_b, ssd_dt_bias, ssd_a_log, ssd_d, ssd_norm_g,
                 rwkv_mu, rwkv_w0, rwkv_w2, rwkv_a0, rwkv_a2, rwkv_g2, rwkv_k_k, rwkv_k_a, rwkv_r_k,
                 rwkv_ln_g, rwkv_ln_b, gla_w_a2, gla_b_a, gla_norm_g, bsz, seq):
    o = 0
    cols = {}
    for name, n in (("z", SSD_WIDTH), ("xbc", SSD_XBC), ("dt", SSD_HEADS), ("rwkv", RWKV_IN),
                    ("q", GLA_QK_WIDTH), ("k", GLA_QK_WIDTH), ("v", GLA_WIDTH), ("g", GLA_WIDTH),
                    ("ga", GLA_GATE_RANK)):
        cols[name] = w_in[:, o:o + n]
        o += n
    w_ssd = jnp.concatenate([cols["z"], cols["xbc"], _pad_lanes(cols["dt"], LANES)], 1).astype(BF16)
    w_rwkv = cols["rwkv"].astype(BF16)
    w_gla = jnp.concatenate([cols["q"], cols["k"], cols["v"], cols["g"], _pad_lanes(cols["ga"], LANES)],
                            1).astype(BF16)
    p_ssd, p_rwkv, p_gla = _in_projection(h, w_ssd, w_rwkv, w_gla)

    y_ssd = _ssd_mixer(
        p_ssd.reshape(bsz, seq, SSD_IN), ssd_conv_w.astype(F32), _row(ssd_conv_b), _row(ssd_dt_bias, LANES),
        _row(-jnp.exp(ssd_a_log.astype(F32)), LANES), _row(jnp.repeat(ssd_d.astype(F32), SSD_HEAD_DIM)),
        _row(ssd_norm_g))

    w_lr = jnp.zeros((RWKV_LR, 3 * RWKV_WIDTH), F32)
    w_lr = w_lr.at[0:RWKV_DECAY_RANK, 0:RWKV_WIDTH].set(rwkv_w2)
    w_lr = w_lr.at[RWKV_DECAY_RANK:RWKV_DECAY_RANK + RWKV_ICLR_RANK, RWKV_WIDTH:2 * RWKV_WIDTH].set(rwkv_a2)
    w_lr = w_lr.at[RWKV_DECAY_RANK + RWKV_ICLR_RANK:, 2 * RWKV_WIDTH:].set(rwkv_g2)
    y_rwkv = _rwkv_mixer(
        p_rwkv.reshape(bsz, seq, RWKV_IN), _row(rwkv_mu), w_lr, _row(rwkv_w0), _row(rwkv_a0), _row(rwkv_k_k),
        _row(rwkv_k_a), _row(rwkv_r_k), _row(rwkv_ln_g), _row(rwkv_ln_b))

    w_a2 = jnp.zeros((LANES, GLA_QK_WIDTH), F32).at[0:GLA_GATE_RANK].set(gla_w_a2)
    y_gla = _gla_mixer(p_gla.reshape(bsz, seq, GLA_IN), w_a2, _row(gla_b_a), _row(gla_norm_g))

    n_tok = bsz * seq
    return (y_ssd.reshape(n_tok, SSD_WIDTH), y_rwkv.reshape(n_tok, RWKV_WIDTH),
            y_gla.reshape(n_tok, GLA_WIDTH))


def _moe_layer(h1, h1p, eid, gate, layer, w_gate, w_up, w_down, ln_g, ln_b):
    n_tok = h1.shape[0]
    tm = TOKEN_TILE
    n_blocks = n_tok * TOP_K // ROW_BLOCK + N_EXPERTS
    dest, counts = _dispatch_plan(eid)
    blocks = jnp.ceil(counts[0, 0:N_EXPERTS] / ROW_BLOCK).astype(jnp.int32)
    block_end = jnp.cumsum(blocks)
    block_ids = jnp.arange(n_blocks, dtype=jnp.int32)
    block_expert = jnp.minimum(jnp.sum((block_end[None, :] <= block_ids[:, None]).astype(jnp.int32), axis=1),
                               N_EXPERTS - 1)
    n_used = block_end[N_EXPERTS - 1:N_EXPERTS]
    dest3 = dest[:, 0:TOP_K].reshape(n_tok // tm, 1, tm * TOP_K)
    xs = _scatter_rows(dest3, h1p, n_blocks * ROW_BLOCK)
    ys = _expert_mlp(block_expert, n_used, xs, w_gate, w_up, w_down, layer)
    return _combine(dest3, ys, h1, gate, ln_g, ln_b)


def kernel(x, ln_in_g, ln_in_b, w_in, ssd_conv_w, ssd_conv_b, ssd_dt_bias, ssd_a_log, ssd_d, ssd_norm_g, rwkv_mu, rwkv_w0, rwkv_w2, rwkv_a0, rwkv_a2, rwkv_g2, rwkv_k_k, rwkv_k_a, rwkv_r_k, rwkv_ln_g, rwkv_ln_b, gla_w_a2, gla_b_a, gla_norm_g, w_out, ln1_g, ln1_b, moe_w_rg, moe_b_rg, moe_w_re, moe_b_re, moe_w_gate, moe_w_up, moe_w_down, ln2_g, ln2_b):
    bsz, seq, d = x.shape
    n_tok = bsz * seq
    h = _input_ln(x.reshape(n_tok, d), ln_in_g, ln_in_b)
    for i in range(w_in.shape[0]):
        y_ssd, y_rwkv, y_gla = _mixer_layer(
            h, w_in[i], ssd_conv_w[i], ssd_conv_b[i], ssd_dt_bias[i], ssd_a_log[i], ssd_d[i], ssd_norm_g[i],
            rwkv_mu[i], rwkv_w0[i], rwkv_w2[i], rwkv_a0[i], rwkv_a2[i], rwkv_g2[i], rwkv_k_k[i],
            rwkv_k_a[i], rwkv_r_k[i], rwkv_ln_g[i], rwkv_ln_b[i], gla_w_a2[i], gla_b_a[i], gla_norm_g[i],
            bsz, seq)
        w_r = _pad_lanes(jnp.concatenate([moe_w_rg[i], moe_w_re[i]], 1).astype(F32), LANES)
        b_r = _row(jnp.concatenate([moe_b_rg[i], moe_b_re[i]]), LANES)
        h1, h1p, eid, gate = _out_projection(
            y_ssd, y_rwkv, y_gla, h, w_out[i].astype(BF16), _row(ln1_g[i]), _row(ln1_b[i]), w_r, b_r)
        h = _moe_layer(h1, h1p, eid, gate, i, moe_w_gate, moe_w_up, moe_w_down, _row(ln2_g[i]), _row(ln2_b[i]))
    return h.reshape(bsz, seq, d)
```

```python
import functools
import math

import jax
import jax.numpy as jnp
from jax import lax
from jax.experimental import pallas as pl
from jax.experimental.pallas import tpu as pltpu

F32 = jnp.float32
BF16 = jnp.bfloat16

D_MODEL = 1024
DEPTH = 2

SSD_HEAD_DIM = 64
SSD_WIDTH = D_MODEL // 2
SSD_HEADS = SSD_WIDTH // SSD_HEAD_DIM
SSD_GROUPS = 2
SSD_STATE = 64
SSD_CONV = 4
SSD_BC = SSD_GROUPS * SSD_STATE
SSD_XBC = SSD_WIDTH + 2 * SSD_BC

RWKV_HEAD_DIM = 64
RWKV_WIDTH = D_MODEL // 4
RWKV_HEADS = RWKV_WIDTH // RWKV_HEAD_DIM
RWKV_DECAY_RANK = 32
RWKV_ICLR_RANK = 32
RWKV_GATE_RANK = 64
RWKV_LR = RWKV_DECAY_RANK + RWKV_ICLR_RANK + RWKV_GATE_RANK
RWKV_IN = 3 * RWKV_WIDTH + RWKV_LR
RWKV_GN_EPS = 64e-5

GLA_VALUE_DIM = 64
GLA_WIDTH = D_MODEL // 4
GLA_HEADS = GLA_WIDTH // GLA_VALUE_DIM
GLA_KEY_DIM = GLA_VALUE_DIM // 2
GLA_QK_WIDTH = GLA_HEADS * GLA_KEY_DIM
GLA_GATE_RANK = 16
GLA_GATE_TEMP = 16.0

MIX_WIDTH = SSD_WIDTH + RWKV_WIDTH + GLA_WIDTH

N_GROUPS = 4
EXPERTS_PER_GROUP = 8
N_EXPERTS = N_GROUPS * EXPERTS_PER_GROUP
TOP_K = 2
EXPERT_FF = 512

DEEPNORM_ALPHA = (2 * DEPTH) ** 0.25
LN_EPS = 1e-5
RMS_EPS = 1e-6

LANES = 128
SUBLANES = 8
SSD_IN = SSD_WIDTH + SSD_XBC + LANES
GLA_IN = 2 * GLA_QK_WIDTH + 2 * GLA_WIDTH + LANES

TOKEN_TILE = 512
SSD_BLOCK = 128
RWKV_BLOCK = 64
GLA_BLOCK = 64
CHUNKS_PER_STEP = 4
RWKV_CHUNKS_PER_STEP = 2
ROW_BLOCK = 512
DMA_UNROLL = 8
VMEM_LIMIT = 48 * 1024 * 1024


def _dot(a, b):
    return jnp.dot(a.astype(BF16), b.astype(BF16), preferred_element_type=F32)


def _split_bf16(x):
    hi = x.astype(BF16)
    return hi, (x - hi.astype(F32)).astype(BF16)


def _dot_rhs2(a_exact, b):
    b_hi, b_lo = _split_bf16(b)
    return (jnp.dot(a_exact, b_hi, preferred_element_type=F32)
            + jnp.dot(a_exact, b_lo, preferred_element_type=F32))


def _dot_lhs2(a, b_exact):
    a_hi, a_lo = _split_bf16(a)
    return (jnp.dot(a_hi, b_exact, preferred_element_type=F32)
            + jnp.dot(a_lo, b_exact, preferred_element_type=F32))


def _cumsum_rows(mask, x):
    m = mask.astype(BF16)
    x_hi, rest = _split_bf16(x)
    x_mid, x_lo = _split_bf16(x - x_hi.astype(F32))
    del rest
    return (jnp.dot(m, x_hi, preferred_element_type=F32)
            + (jnp.dot(m, x_mid, preferred_element_type=F32) + jnp.dot(m, x_lo, preferred_element_type=F32)))


def _dot_x3(a, b):
    a_hi, a_lo = _split_bf16(a)
    b_hi, b_lo = _split_bf16(b)
    return (jnp.dot(a_hi, b_hi, preferred_element_type=F32)
            + (jnp.dot(a_hi, b_lo, preferred_element_type=F32)
               + jnp.dot(a_lo, b_hi, preferred_element_type=F32)))


def _pack_bf16_pair(x):
    m = x.shape[1] // 2
    lo = lax.bitcast_convert_type(x[:, 0:m].astype(BF16).astype(F32), jnp.uint32)
    hi = lax.bitcast_convert_type(x[:, m:2 * m].astype(BF16).astype(F32), jnp.uint32)
    return (lo >> 16) | (hi & jnp.uint32(0xFFFF0000))


def _unpack_bf16_pair(p):
    lo = lax.bitcast_convert_type(p << 16, F32)
    hi = lax.bitcast_convert_type(p & jnp.uint32(0xFFFF0000), F32)
    return lo, hi


def _dot_nt(a, b, precision=None):
    return lax.dot_general(a, b, (((1,), (1,)), ((), ())), precision=precision,
                           preferred_element_type=F32)


def _dot_tn(a, b, precision=None):
    return lax.dot_general(a, b, (((0,), (0,)), ((), ())), precision=precision,
                           preferred_element_type=F32)


def _sigmoid(x):
    return 1.0 / (1.0 + jnp.exp(-x))


def _silu(x):
    return x * _sigmoid(x)


def _softplus(x):
    return jnp.maximum(x, 0.0) + jnp.log1p(jnp.exp(-jnp.abs(x)))


def _iota(shape, dim):
    return lax.broadcasted_iota(jnp.int32, shape, dim)


def _block_indicator(rows, cols, row_seg, col_seg):
    r = _iota((rows, cols), 0) // row_seg
    c = _iota((rows, cols), 1) // col_seg
    return (r == c).astype(F32)


def _layer_norm(x, g, b):
    mu = jnp.mean(x, -1, keepdims=True)
    xc = x - mu
    var = jnp.mean(xc * xc, -1, keepdims=True)
    return xc * lax.rsqrt(var + LN_EPS) * g + b


def _params(*sem):
    return pltpu.CompilerParams(dimension_semantics=sem, vmem_limit_bytes=VMEM_LIMIT)


def _inproj_kernel(h_ref, g_ref, b_ref, w1_ref, w2_ref, w3_ref, o1_ref, o2_ref, o3_ref, *, normalize_input):
    h = h_ref[...]
    if normalize_input:
        h = _layer_norm(h, g_ref[...], b_ref[...])
    hb = h.astype(BF16)
    o1_ref[...] = jnp.dot(hb, w1_ref[...], preferred_element_type=F32)
    o2_ref[...] = jnp.dot(hb, w2_ref[...], preferred_element_type=F32)
    o3_ref[...] = jnp.dot(hb, w3_ref[...], preferred_element_type=F32)


def _in_projection(h, h_g, h_b, w_ssd, w_rwkv, w_gla, normalize_input):
    n_tok, d = h.shape
    tm = TOKEN_TILE
    consts = (h_g, h_b, w_ssd, w_rwkv, w_gla)
    ws = (w_ssd, w_rwkv, w_gla)
    return pl.pallas_call(
        functools.partial(_inproj_kernel, normalize_input=normalize_input), grid=(n_tok // tm,),
        in_specs=[pl.BlockSpec((tm, d), lambda i: (i, 0))]
        + [pl.BlockSpec(a.shape, lambda i: (0, 0)) for a in consts],
        out_specs=[pl.BlockSpec((tm, w.shape[1]), lambda i: (i, 0)) for w in ws],
        out_shape=[jax.ShapeDtypeStruct((n_tok, w.shape[1]), F32) for w in ws],
        compiler_params=_params("parallel"), name="in_projection")(h, *consts)


def _ssd_kernel(p_ref, cw_ref, cb_ref, dtb_ref, a_ref, d_ref, ng_ref, o_ref, xs_ref, st_ref):
    @pl.when(pl.program_id(1) == 0)
    def _():
        xs_ref[0:SUBLANES, :] = jnp.zeros((SUBLANES, SSD_XBC), F32)
        st_ref[...] = jnp.zeros(st_ref.shape, F32)

    for s in range(o_ref.shape[0] // SSD_BLOCK):
        rows = pl.ds(s * SSD_BLOCK, SSD_BLOCK)
        _ssd_chunk(p_ref.at[rows], cw_ref, cb_ref, dtb_ref, a_ref, d_ref, ng_ref, o_ref.at[rows], xs_ref, st_ref)


def _ssd_chunk(p_ref, cw_ref, cb_ref, dtb_ref, a_ref, d_ref, ng_ref, o_ref, xs_ref, st_ref):
    blk = o_ref.shape[0]
    tail = SUBLANES
    z = p_ref[:, 0:SSD_WIDTH]
    xs_ref[tail:tail + blk, :] = p_ref[:, SSD_WIDTH:SSD_WIDTH + SSD_XBC]
    dt_raw = p_ref[:, SSD_WIDTH + SSD_XBC:SSD_IN]

    acc = jnp.broadcast_to(cb_ref[...], (blk, SSD_XBC))
    for i in range(SSD_CONV):
        acc = acc + cw_ref[i:i + 1, :] * xs_ref[pl.ds(tail - (SSD_CONV - 1) + i, blk), :]
    xs_ref[0:tail, :] = xs_ref[blk:blk + tail, :]
    xbc = _silu(acc)
    xh = xbc[:, 0:SSD_WIDTH]
    bm = xbc[:, SSD_WIDTH:SSD_WIDTH + SSD_BC]
    cm = xbc[:, SSD_WIDTH + SSD_BC:SSD_XBC]

    dt = _softplus(dt_raw + dtb_ref[...])
    a_dt = dt * a_ref[...]
    row = _iota((blk, blk), 0)
    col = _iota((blk, blk), 1)
    causal = col <= row
    a_cs = _cumsum_rows(causal, a_dt)
    a_cs_t = a_cs.T
    a_last = a_cs[blk - 1:blk, :]

    expand = _block_indicator(LANES, SSD_WIDTH, 1, SSD_HEAD_DIM).astype(BF16)
    per_head = jnp.concatenate([dt, jnp.exp(a_cs), jnp.exp(a_last - a_cs),
                                jnp.broadcast_to(jnp.exp(a_last), (SUBLANES, LANES))], axis=0)
    per_chan = _dot_lhs2(per_head, expand)
    dt_e = per_chan[0:blk]
    dec_e = per_chan[blk:2 * blk]
    te_e = per_chan[2 * blk:3 * blk]
    cd_e = per_chan[3 * blk:3 * blk + 1]

    x_dt = xh * dt_e
    state = st_ref[...]
    y_off = _dot(cm, state) * dec_e

    lane = _iota((blk, LANES), 1)
    hpg = SSD_HEADS // SSD_GROUPS
    pairs = []
    for j in range(SSD_HEADS // 2):
        grp = (2 * j) // hpg
        in_grp = (lane // SSD_STATE) == grp
        cb = _dot_nt(jnp.where(in_grp, cm, 0.0).astype(BF16), bm.astype(BF16))
        xp = x_dt[:, j * LANES:(j + 1) * LANES]
        ys = []
        for h in (2 * j, 2 * j + 1):
            diff = jnp.broadcast_to(a_cs[:, h:h + 1], (blk, blk)) - a_cs_t[h:h + 1, :]
            seg = jnp.exp(jnp.where(causal, diff, -jnp.inf))
            ys.append(_dot(cb * seg, xp))
        pairs.append(jnp.where(lane < SSD_HEAD_DIM, ys[0], ys[1]))
    y = jnp.concatenate(pairs, axis=1) + y_off + d_ref[...] * xh

    new = _dot_tn(bm.astype(BF16), (x_dt * te_e).astype(BF16))
    keep = _block_indicator(SSD_BC, SSD_WIDTH, SSD_STATE, SSD_WIDTH // SSD_GROUPS)
    st_ref[...] = state * cd_e + keep * new

    y = y * _silu(z)
    gw = SSD_WIDTH // SSD_GROUPS
    for g in range(SSD_GROUPS):
        yg = y[:, g * gw:(g + 1) * gw]
        ms = jnp.mean(yg * yg, -1, keepdims=True)
        o_ref[:, g * gw:(g + 1) * gw] = (yg * lax.rsqrt(ms + RMS_EPS)
                                         * ng_ref[:, g * gw:(g + 1) * gw]).astype(o_ref.dtype)


def _ssd_mixer(p, conv_w, conv_b, dt_bias, a_neg, d_skip, norm_g):
    bsz, seq, _ = p.shape
    blk = SSD_BLOCK * CHUNKS_PER_STEP
    small = lambda a: pl.BlockSpec(a.shape, lambda b, c: (0, 0))
    args = (conv_w, conv_b, dt_bias, a_neg, d_skip, norm_g)
    return pl.pallas_call(
        _ssd_kernel, grid=(bsz, seq // blk),
        in_specs=[pl.BlockSpec((None, blk, SSD_IN), lambda b, c: (b, c, 0))] + [small(a) for a in args],
        out_specs=pl.BlockSpec((None, blk, SSD_WIDTH), lambda b, c: (b, c, 0)),
        out_shape=jax.ShapeDtypeStruct((bsz, seq, SSD_WIDTH), BF16),
        scratch_shapes=[pltpu.VMEM((SSD_BLOCK + 2 * SUBLANES, SSD_XBC), F32),
                        pltpu.VMEM((SSD_BC, SSD_WIDTH), F32)],
        compiler_params=_params("parallel", "arbitrary"), name="ssd_mixer")(p, *args)


def _rwkv_kernel(p_ref, mu_ref, wlr_ref, w0_ref, a0_ref, kk_ref, ka_ref, rk_ref, lng_ref, lnb_ref,
                 o_ref, xs_ref, st_ref):
    @pl.when(pl.program_id(0) == 0)
    def _():
        xs_ref[:, 0:SUBLANES, :] = jnp.zeros((xs_ref.shape[0], SUBLANES, RWKV_IN), F32)
        st_ref[...] = jnp.zeros(st_ref.shape, F32)

    for s in range(o_ref.shape[1] // RWKV_BLOCK):
        _rwkv_chunk(p_ref, mu_ref, wlr_ref, w0_ref, a0_ref, kk_ref, ka_ref, rk_ref, lng_ref, lnb_ref,
                    o_ref, xs_ref, st_ref, s * RWKV_BLOCK)


def _rwkv_chunk(p_ref, mu_ref, wlr_ref, w0_ref, a0_ref, kk_ref, ka_ref, rk_ref, lng_ref, lnb_ref,
                o_ref, xs_ref, st_ref, frame0):
    nb = o_ref.shape[0]
    blk = RWKV_BLOCK
    rows = nb * blk
    tail = SUBLANES
    width = RWKV_WIDTH

    ps, prevs = [], []
    for b in range(nb):
        ps.append(p_ref[b, frame0:frame0 + blk, :])
        xs_ref[b, tail:tail + blk, :] = ps[b]
        prevs.append(xs_ref[b, pl.ds(tail - 1, blk), :])
        xs_ref[b, 0:tail, :] = xs_ref[b, blk:blk + tail, :]
    p = jnp.concatenate(ps, axis=0)
    prev = jnp.concatenate(prevs, axis=0)
    pm = p + (prev - p) * mu_ref[...]
    r = pm[:, 0:width]
    k = pm[:, width:2 * width]
    v = pm[:, 2 * width:3 * width]
    lr = pm[:, 3 * width:RWKV_IN]

    lane = _iota((rows, RWKV_LR), 1)
    lr_act = jnp.where(lane < RWKV_DECAY_RANK, jnp.tanh(lr),
                       jnp.where(lane < RWKV_DECAY_RANK + RWKV_ICLR_RANK, lr, _sigmoid(lr)))
    proj = _dot(lr_act, wlr_ref[...])
    w = -_softplus(-(w0_ref[...] + proj[:, 0:width])) - 0.5
    log_decay = -jnp.exp(w)
    a = _sigmoid(a0_ref[...] + proj[:, width:2 * width])
    g = proj[:, 2 * width:3 * width]

    head_sum = _block_indicator(width, width, RWKV_HEAD_DIM, RWKV_HEAD_DIM)
    head_sum_b = head_sum.astype(BF16)
    kk = k * kk_ref[...]
    kk = kk * lax.rsqrt(_dot(kk * kk, head_sum_b) + 1e-12)
    k2 = k * (1.0 + (a - 1.0) * ka_ref[...])
    alpha = -kk
    beta = kk * a

    row = _iota((rows, rows), 0)
    col = _iota((rows, rows), 1)
    same_row = (row // blk) == (col // blk)
    incl = same_row & (col <= row)
    strict = same_row & (col < row)
    cs = _dot_rhs2(incl.astype(BF16), log_decay)
    e_neg = jnp.exp(-cs)
    a_t = alpha * jnp.exp(cs - log_decay)
    b_t = (beta * e_neg).astype(BF16)
    k_t = (k2 * e_neg).astype(BF16)
    r_t = r * jnp.exp(cs)
    v_b = v.astype(BF16)

    states = [st_ref[b] for b in range(nb)]
    reads = []
    for b in range(nb):
        sl = slice(b * blk, (b + 1) * blk)
        reads.append(_dot(jnp.concatenate([a_t[sl], r_t[sl]], axis=0), states[b]))
    a_s = jnp.concatenate([x[0:blk] for x in reads], axis=0)
    r_s = jnp.concatenate([x[blk:2 * blk] for x in reads], axis=0)

    wlane = _iota((rows, width), 1)
    in_head = [(wlane // RWKV_HEAD_DIM) == h for h in range(RWKV_HEADS)]
    parts = []
    for h in range(RWKV_HEADS):
        parts += [jnp.where(in_head[h], a_t, 0.0), jnp.where(in_head[h], r_t, 0.0)]
    lhs = jnp.concatenate(parts, axis=0).astype(BF16)
    pb_all = _dot_nt(lhs, b_t)
    pk_all = _dot_nt(lhs, k_t)

    heads = range(RWKV_HEADS)
    base = [2 * h * rows for h in heads]
    npow = [jnp.where(strict, pb_all[base[h]:base[h] + rows], 0.0).astype(BF16) for h in heads]
    u = [a_s + _dot(jnp.where(strict, pk_all[base[h]:base[h] + rows], 0.0), v_b) for h in heads]
    steps = int(math.log2(blk))
    for i in range(steps):
        u = [u[h] + _dot(npow[h], u[h]) for h in heads]
        if i + 1 < steps:
            npow = [_dot(npow[h], npow[h]).astype(BF16) for h in heads]
    y = [_dot(jnp.where(incl, pb_all[base[h] + rows:base[h] + 2 * rows], 0.0), u[h])
         + _dot(jnp.where(incl, pk_all[base[h] + rows:base[h] + 2 * rows], 0.0), v_b) for h in heads]
    u_all = u[0]
    y_all = y[0]
    for h in range(1, RWKV_HEADS):
        u_all = jnp.where(in_head[h], u[h], u_all)
        y_all = jnp.where(in_head[h], y[h], y_all)
    y_all = y_all + r_s

    for b in range(nb):
        sl = slice(b * blk, (b + 1) * blk)
        c_last = cs[(b + 1) * blk - 1:(b + 1) * blk, :]
        to_end = jnp.exp(c_last - cs[sl])
        new = (_dot_tn((beta[sl] * to_end).astype(BF16), u_all[sl].astype(BF16))
               + _dot_tn((k2[sl] * to_end).astype(BF16), v_b[sl]))
        chunk_decay = jnp.broadcast_to(jnp.exp(c_last), (SUBLANES, width)).T[:, 0:1]
        st_ref[b] = states[b] * chunk_decay + head_sum * new

    head_mean_b = (head_sum * (1.0 / RWKV_HEAD_DIM)).astype(BF16)
    mean = _dot_lhs2(y_all, head_mean_b)
    yc = y_all - mean
    var = _dot(yc * yc, head_mean_b)
    yn = yc * lax.rsqrt(var + RWKV_GN_EPS) * lng_ref[...] + lnb_ref[...]
    bonus = _dot(r * k2 * rk_ref[...], head_sum_b) * v
    out = ((yn + bonus) * g).astype(o_ref.dtype)
    for b in range(nb):
        o_ref[b, frame0:frame0 + blk, :] = out[b * blk:(b + 1) * blk]


def _rwkv_mixer(p, mu, wlr, w0, a0, k_k, k_a, r_k, ln_g, ln_b):
    bsz, seq, _ = p.shape
    blk = RWKV_BLOCK * RWKV_CHUNKS_PER_STEP
    small = lambda a: pl.BlockSpec(a.shape, lambda c: (0, 0))
    args = (mu, wlr, w0, a0, k_k, k_a, r_k, ln_g, ln_b)
    return pl.pallas_call(
        _rwkv_kernel, grid=(seq // blk,),
        in_specs=[pl.BlockSpec((bsz, blk, RWKV_IN), lambda c: (0, c, 0))] + [small(a) for a in args],
        out_specs=pl.BlockSpec((bsz, blk, RWKV_WIDTH), lambda c: (0, c, 0)),
        out_shape=jax.ShapeDtypeStruct((bsz, seq, RWKV_WIDTH), BF16),
        scratch_shapes=[pltpu.VMEM((bsz, RWKV_BLOCK + 2 * SUBLANES, RWKV_IN), F32),
                        pltpu.VMEM((bsz, RWKV_WIDTH, RWKV_WIDTH), F32)],
        compiler_params=_params("arbitrary"), name="rwkv7_mixer")(p, *args)


def _gla_kernel(p_ref, wa_ref, ba_ref, ng_ref, o_ref, st_ref, z_ref, acc_ref):
    @pl.when(pl.program_id(1) == 0)
    def _():
        st_ref[...] = jnp.zeros(st_ref.shape, F32)

    for s in range(o_ref.shape[0] // GLA_BLOCK):
        rows = pl.ds(s * GLA_BLOCK, GLA_BLOCK)
        _gla_chunk(p_ref.at[rows], wa_ref, ba_ref, ng_ref, o_ref.at[rows], st_ref, z_ref, acc_ref)


def _gla_chunk(p_ref, wa_ref, ba_ref, ng_ref, o_ref, st_ref, z_ref, acc_ref):
    blk = o_ref.shape[0]
    qk = GLA_QK_WIDTH
    vw = GLA_WIDTH
    q = p_ref[:, 0:qk] * (GLA_KEY_DIM ** -0.5)
    k = p_ref[:, qk:2 * qk]
    g = p_ref[:, 2 * qk + vw:2 * qk + 2 * vw]
    a_lr = p_ref[:, 2 * qk + 2 * vw:GLA_IN]
    v_off = 2 * qk

    logit = _dot_x3(a_lr, wa_ref[...]) + ba_ref[...]
    log_a = -_softplus(-logit) / GLA_GATE_TEMP
    row = _iota((blk, blk), 0)
    col = _iota((blk, blk), 1)
    cum = _cumsum_rows(col <= row, log_a)

    state = st_ref[...]
    acc_ref[...] = _dot_nt((q * jnp.exp(cum)).astype(BF16), state.astype(BF16))

    spread = _block_indicator(qk, vw, GLA_KEY_DIM, GLA_VALUE_DIM).astype(BF16)
    rows = _iota((blk, qk), 0)
    group = 2 * SUBLANES
    for jb in range(blk // group):
        r0 = jb * group
        n = blk - r0
        for jj in range(group):
            j = r0 + jj
            decay = jnp.exp(jnp.where(rows[r0:] >= j, cum[r0:] - cum[j:j + 1, :], -jnp.inf))
            z_ref[jj * n:(jj + 1) * n, :] = (q[r0:] * decay * k[j:j + 1, :]).astype(BF16)
        res = jnp.dot(z_ref[0:group * n, :], spread, preferred_element_type=F32)
        part = jnp.zeros((n, vw), F32)
        for jj in range(group):
            j = r0 + jj
            part = part + res[jj * n:(jj + 1) * n] * p_ref[j:j + 1, v_off:v_off + vw]
        acc_ref[r0:blk, :] = acc_ref[r0:blk, :] + part

    v = p_ref[:, v_off:v_off + vw]
    last = cum[blk - 1:blk, :]
    k_dec = k * jnp.exp(last - cum)
    keep = _block_indicator(vw, qk, GLA_VALUE_DIM, GLA_KEY_DIM)
    st_ref[...] = state * jnp.exp(last) + keep * _dot_tn(v.astype(BF16), k_dec.astype(BF16))

    o = acc_ref[...]
    head_mean = (_block_indicator(vw, vw, GLA_VALUE_DIM, GLA_VALUE_DIM) * (1.0 / GLA_VALUE_DIM)).astype(BF16)
    ms = _dot_lhs2(o * o, head_mean)
    o_ref[...] = (o * lax.rsqrt(ms + RMS_EPS) * ng_ref[...] * _silu(g)).astype(o_ref.dtype)


def _gla_mixer(p, w_a2, b_a, norm_g):
    bsz, seq, _ = p.shape
    blk = GLA_BLOCK * CHUNKS_PER_STEP
    small = lambda a: pl.BlockSpec(a.shape, lambda b, c: (0, 0))
    args = (w_a2, b_a, norm_g)
    return pl.pallas_call(
        _gla_kernel, grid=(bsz, seq // blk),
        in_specs=[pl.BlockSpec((None, blk, GLA_IN), lambda b, c: (b, c, 0))] + [small(a) for a in args],
        out_specs=pl.BlockSpec((None, blk, GLA_WIDTH), lambda b, c: (b, c, 0)),
        out_shape=jax.ShapeDtypeStruct((bsz, seq, GLA_WIDTH), BF16),
        scratch_shapes=[pltpu.VMEM((GLA_WIDTH, GLA_QK_WIDTH), F32),
                        pltpu.VMEM((2 * SUBLANES * GLA_BLOCK, GLA_QK_WIDTH), BF16),
                        pltpu.VMEM((GLA_BLOCK, GLA_WIDTH), F32)],
        compiler_params=_params("parallel", "arbitrary"), name="gla_mixer")(p, *args)


def _outproj_kernel(ys_ref, yr_ref, yg_ref, h_ref, w1_ref, w2_ref, w3_ref, g_ref, b_ref, wr_ref, br_ref,
                    hg_ref, hb_ref, h1_ref, h1p_ref, eid_ref, gate_ref, cnt_ref, *, normalize_input):
    mix = (jnp.dot(ys_ref[...], w1_ref[...], preferred_element_type=F32)
           + jnp.dot(yr_ref[...], w2_ref[...], preferred_element_type=F32)
           + jnp.dot(yg_ref[...], w3_ref[...], preferred_element_type=F32))
    h = h_ref[...]
    if normalize_input:
        h = _layer_norm(h, hg_ref[...], hb_ref[...])
    h1 = _layer_norm(DEEPNORM_ALPHA * h + mix, g_ref[...], b_ref[...])
    h1_ref[...] = h1
    h1p_ref[...] = _pack_bf16_pair(h1)

    logits = _dot_x3(h1, wr_ref[...]) + br_ref[...]
    tm = logits.shape[0]
    lane = _iota((tm, LANES), 1)
    lane_f = lane.astype(F32)

    def masked_softmax(mask):
        m = jnp.max(jnp.where(mask, logits, -jnp.inf), -1, keepdims=True)
        e = jnp.where(mask, jnp.exp(logits - m), 0.0)
        return e / jnp.sum(e, -1, keepdims=True)

    def first_argmax(vals, mask):
        m = jnp.max(jnp.where(mask, vals, -jnp.inf), -1, keepdims=True)
        idx = jnp.min(jnp.where(mask & (vals == m), lane_f, float(LANES)), -1, keepdims=True)
        return m, idx.astype(jnp.int32)

    is_group = lane < N_GROUPS
    g_prob = masked_softmax(is_group)
    g_w, g_idx = first_argmax(g_prob, is_group)
    e_lo = N_GROUPS + g_idx * EXPERTS_PER_GROUP
    in_group = (lane >= e_lo) & (lane < e_lo + EXPERTS_PER_GROUP)
    e_prob = masked_softmax(in_group)
    p1, i1 = first_argmax(e_prob, in_group)
    p2, i2 = first_argmax(e_prob, in_group & (lane != i1))
    denom = p1 + p2
    eid_ref[...] = jnp.where(lane == 0, i1 - N_GROUPS, jnp.where(lane == 1, i2 - N_GROUPS, 0))
    gate_ref[...] = jnp.where(lane == 0, g_w * (p1 / denom), jnp.where(lane == 1, g_w * (p2 / denom), 0.0))

    chosen = ((lane == i1 - N_GROUPS) | (lane == i2 - N_GROUPS)).astype(F32)

    @pl.when(pl.program_id(0) == 0)
    def _():
        cnt_ref[...] = jnp.zeros(cnt_ref.shape, F32)

    cnt_ref[...] = cnt_ref[...] + jnp.sum(chosen, 0, keepdims=True)


def _out_projection(y_ssd, y_rwkv, y_gla, h, w_out, ln_g, ln_b, w_r, b_r, h_g, h_b, normalize_input):
    n_tok, d = h.shape
    tm = TOKEN_TILE
    row = lambda n: pl.BlockSpec((tm, n), lambda i: (i, 0))
    full = lambda a: pl.BlockSpec(a.shape, lambda i: (0, 0))
    w1 = w_out[0:SSD_WIDTH]
    w2 = w_out[SSD_WIDTH:SSD_WIDTH + RWKV_WIDTH]
    w3 = w_out[SSD_WIDTH + RWKV_WIDTH:MIX_WIDTH]
    consts = (w1, w2, w3, ln_g, ln_b, w_r, b_r, h_g, h_b)
    return pl.pallas_call(
        functools.partial(_outproj_kernel, normalize_input=normalize_input), grid=(n_tok // tm,),
        in_specs=[row(SSD_WIDTH), row(RWKV_WIDTH), row(GLA_WIDTH), row(d)] + [full(a) for a in consts],
        out_specs=[row(d), row(d // 2), row(LANES), row(LANES), pl.BlockSpec((1, LANES), lambda i: (0, 0))],
        out_shape=[jax.ShapeDtypeStruct((n_tok, d), F32),
                   jax.ShapeDtypeStruct((n_tok, d // 2), jnp.uint32),
                   jax.ShapeDtypeStruct((n_tok, LANES), jnp.int32),
                   jax.ShapeDtypeStruct((n_tok, LANES), F32),
                   jax.ShapeDtypeStruct((1, LANES), F32)],
        compiler_params=_params("arbitrary"), name="out_projection")(y_ssd, y_rwkv, y_gla, h, *consts)


def _dest_kernel(eid_ref, cnt_ref, dest_ref, run_ref, base_ref):
    tm = eid_ref.shape[0]
    e = eid_ref[...]
    lane = _iota((tm, LANES), 1)
    oh0 = (lane == e[:, 0:1]).astype(F32)
    oh1 = (lane == e[:, 1:2]).astype(F32)
    oh = oh0 + oh1

    @pl.when(pl.program_id(0) == 0)
    def _():
        run_ref[...] = jnp.zeros(run_ref.shape, F32)
        blocks = jnp.ceil(cnt_ref[...] * (1.0 / ROW_BLOCK))
        before = (_iota((LANES, LANES), 0) < _iota((LANES, LANES), 1)).astype(BF16)
        start = jnp.dot(jnp.broadcast_to(blocks, (SUBLANES, LANES)).astype(BF16), before,
                        preferred_element_type=F32)[0:1, :]
        base_ref[...] = start * ROW_BLOCK

    earlier = (_iota((tm, tm), 1) < _iota((tm, tm), 0)).astype(BF16)
    rank = jnp.dot(earlier, oh.astype(BF16), preferred_element_type=F32)
    pos = rank + run_ref[...] + base_ref[...]
    d0 = jnp.sum(oh0 * pos, -1, keepdims=True)
    d1 = jnp.sum(oh1 * pos, -1, keepdims=True)
    dest_ref[...] = jnp.where(lane == 0, d0, jnp.where(lane == 1, d1, 0.0)).astype(jnp.int32)
    run_ref[...] = run_ref[...] + jnp.sum(oh, 0, keepdims=True)


def _dispatch_plan(eid, counts):
    n_tok = eid.shape[0]
    tm = TOKEN_TILE
    return pl.pallas_call(
        _dest_kernel, grid=(n_tok // tm,),
        in_specs=[pl.BlockSpec((tm, LANES), lambda i: (i, 0)), pl.BlockSpec((1, LANES), lambda i: (0, 0))],
        out_specs=pl.BlockSpec((tm, LANES), lambda i: (i, 0)),
        out_shape=jax.ShapeDtypeStruct((n_tok, LANES), jnp.int32),
        scratch_shapes=[pltpu.VMEM((1, LANES), F32), pltpu.VMEM((1, LANES), F32)],
        compiler_params=_params("arbitrary"), name="dispatch_plan")(eid, counts)


def _row_copy(src_ref, src_row, dst_ref, dst_row, sem):
    return pltpu.make_async_copy(src_ref.at[pl.ds(src_row, 1)], dst_ref.at[pl.ds(dst_row, 1)], sem)


def _scatter_kernel(dest_ref, h_ref, xs_in_ref, xs_ref, sem):
    del xs_in_ref
    tm = h_ref.shape[0]

    def start(t, carry):
        for k in range(TOP_K):
            _row_copy(h_ref, t, xs_ref, dest_ref[0, 0, TOP_K * t + k], sem).start()
        return carry

    lax.fori_loop(0, tm, start, 0, unroll=DMA_UNROLL)
    all_rows = xs_ref.at[pl.ds(0, TOP_K * tm)]
    pltpu.make_async_copy(all_rows, all_rows, sem).wait()


def _scatter_rows(dest3, h1, n_rows):
    n_tok, d = h1.shape
    n_tiles = dest3.shape[0]
    xs0 = jnp.zeros((n_rows, d), h1.dtype)
    return pl.pallas_call(
        _scatter_kernel, grid=(n_tiles,),
        in_specs=[pl.BlockSpec((1, 1, dest3.shape[2]), lambda i: (i, 0, 0), memory_space=pltpu.SMEM),
                  pl.BlockSpec((n_tok // n_tiles, d), lambda i: (i, 0)), pl.BlockSpec(memory_space=pl.ANY)],
        out_specs=pl.BlockSpec(memory_space=pl.ANY),
        out_shape=jax.ShapeDtypeStruct((n_rows, d), h1.dtype),
        scratch_shapes=[pltpu.SemaphoreType.DMA(())],
        input_output_aliases={2: 0},
        compiler_params=_params("arbitrary"), name="expert_scatter")(dest3, h1, xs0)


def _expert_kernel(be_ref, used_ref, x_ref, wg_ref, wu_ref, wd_ref, y_ref, wg_b, wu_b, wd_b):
    b = pl.program_id(0)

    @pl.when((b == 0) | (be_ref[b] != be_ref[jnp.maximum(b - 1, 0)]))
    def _():
        wg_b[...] = wg_ref[...].astype(BF16)
        wu_b[...] = wu_ref[...].astype(BF16)
        wd_b[...] = wd_ref[...].astype(BF16)

    @pl.when(b < used_ref[0])
    def _():
        x_lo, x_hi = _unpack_bf16_pair(x_ref[...])
        x = jnp.concatenate([x_lo.astype(BF16), x_hi.astype(BF16)], axis=1)
        gate = jnp.dot(x, wg_b[...], preferred_element_type=F32)
        up = jnp.dot(x, wu_b[...], preferred_element_type=F32)
        mid = (_silu(gate) * up).astype(BF16)
        y_ref[...] = _pack_bf16_pair(jnp.dot(mid, wd_b[...], preferred_element_type=F32))

    @pl.when(b >= used_ref[0])
    def _():
        y_ref[...] = jnp.zeros(y_ref.shape, y_ref.dtype)


def _expert_mlp(block_expert, n_used, xs, w_gate, w_up, w_down, layer):
    n_rows, dp = xs.shape
    n_blocks = n_rows // ROW_BLOCK
    d, ff = w_gate.shape[-2:]
    return pl.pallas_call(
        _expert_kernel,
        grid_spec=pltpu.PrefetchScalarGridSpec(
            num_scalar_prefetch=2, grid=(n_blocks,),
            in_specs=[pl.BlockSpec((ROW_BLOCK, dp), lambda b, be, nu: (b, 0)),
                      pl.BlockSpec((None, None, d, ff), lambda b, be, nu: (layer, be[b], 0, 0)),
                      pl.BlockSpec((None, None, d, ff), lambda b, be, nu: (layer, be[b], 0, 0)),
                      pl.BlockSpec((None, None, ff, d), lambda b, be, nu: (layer, be[b], 0, 0))],
            out_specs=pl.BlockSpec((ROW_BLOCK, dp), lambda b, be, nu: (b, 0)),
            scratch_shapes=[pltpu.VMEM((d, ff), BF16), pltpu.VMEM((d, ff), BF16), pltpu.VMEM((ff, d), BF16)]),
        out_shape=jax.ShapeDtypeStruct((n_rows, dp), jnp.uint32),
        compiler_params=_params("arbitrary"), name="expert_mlp")(
            block_expert, n_used, xs, w_gate, w_up, w_down)


def _combine_kernel(dest_ref, y_ref, h_ref, gate_ref, g_ref, b_ref, o_ref, buf_ref, sem):
    tm = o_ref.shape[0]

    def start(t, carry):
        for k in range(TOP_K):
            _row_copy(y_ref, dest_ref[0, 0, TOP_K * t + k], buf_ref.at[k], t, sem).start()
        return carry

    lax.fori_loop(0, tm, start, 0, unroll=DMA_UNROLL)
    pltpu.make_async_copy(buf_ref, buf_ref, sem).wait()
    gate = gate_ref[...]
    lo0, hi0 = _unpack_bf16_pair(buf_ref[0])
    lo1, hi1 = _unpack_bf16_pair(buf_ref[1])
    g0 = gate[:, 0:1]
    g1 = gate[:, 1:2]
    ffn = jnp.concatenate([g0 * lo0 + g1 * lo1, g0 * hi0 + g1 * hi1], axis=1)
    o_ref[...] = _layer_norm(DEEPNORM_ALPHA * h_ref[...] + ffn, g_ref[...], b_ref[...])


def _combine(dest3, ys, h1, gate, ln_g, ln_b):
    n_tok, d = h1.shape
    tm = TOKEN_TILE
    row = lambda n: pl.BlockSpec((tm, n), lambda i: (i, 0))
    vec = pl.BlockSpec((1, d), lambda i: (0, 0))
    return pl.pallas_call(
        _combine_kernel, grid=(n_tok // tm,),
        in_specs=[pl.BlockSpec((1, 1, dest3.shape[2]), lambda i: (i, 0, 0), memory_space=pltpu.SMEM),
                  pl.BlockSpec(memory_space=pl.ANY), row(d), row(LANES), vec, vec],
        out_specs=row(d),
        out_shape=jax.ShapeDtypeStruct((n_tok, d), F32),
        scratch_shapes=[pltpu.VMEM((TOP_K, tm, ys.shape[1]), ys.dtype), pltpu.SemaphoreType.DMA(())],
        compiler_params=_params("arbitrary"), name="expert_combine")(dest3, ys, h1, gate, ln_g, ln_b)


def _pad_lanes(a, n):
    return jnp.pad(a, [(0, 0)] * (a.ndim - 1) + [(0, n - a.shape[-1])])


def _row(a, n=None):
    a = a.reshape(1, -1).astype(F32)
    return a if n is None else _pad_lanes(a, n)


def _mixer_layer(h, h_g, h_b, normalize_input, w_in, ssd_conv_w, ssd_conv_b, ssd_dt_bias, ssd_a_log, ssd_d, ssd_norm_g,
                 rwkv_mu, rwkv_w0, rwkv_w2, rwkv_a0, rwkv_a2, rwkv_g2, rwkv_k_k, rwkv_k_a, rwkv_r_k,
                 rwkv_ln_g, rwkv_ln_b, gla_w_a2, gla_b_a, gla_norm_g, bsz, seq):
    o = 0
    cols = {}
    for name, n in (("z", SSD_WIDTH), ("xbc", SSD_XBC), ("dt", SSD_HEADS), ("rwkv", RWKV_IN),
                    ("q", GLA_QK_WIDTH), ("k", GLA_QK_WIDTH), ("v", GLA_WIDTH), ("g", GLA_WIDTH),
                    ("ga", GLA_GATE_RANK)):
        cols[name] = w_in[:, o:o + n]
        o += n
    w_ssd = jnp.concatenate([cols["z"], cols["xbc"], _pad_lanes(cols["dt"], LANES)], 1).astype(BF16)
    w_rwkv = cols["rwkv"].astype(BF16)
    w_gla = jnp.concatenate([cols["q"], cols["k"], cols["v"], cols["g"], _pad_lanes(cols["ga"], LANES)],
                            1).astype(BF16)
    p_ssd, p_rwkv, p_gla = _in_projection(h, h_g, h_b, w_ssd, w_rwkv, w_gla, normalize_input)

    y_ssd = _ssd_mixer(
        p_ssd.reshape(bsz, seq, SSD_IN), ssd_conv_w.astype(F32), _row(ssd_conv_b), _row(ssd_dt_bias, LANES),
        _row(-jnp.exp(ssd_a_log.astype(F32)), LANES), _row(jnp.repeat(ssd_d.astype(F32), SSD_HEAD_DIM)),
        _row(ssd_norm_g))

    w_lr = jnp.zeros((RWKV_LR, 3 * RWKV_WIDTH), F32)
    w_lr = w_lr.at[0:RWKV_DECAY_RANK, 0:RWKV_WIDTH].set(rwkv_w2)
    w_lr = w_lr.at[RWKV_DECAY_RANK:RWKV_DECAY_RANK + RWKV_ICLR_RANK, RWKV_WIDTH:2 * RWKV_WIDTH].set(rwkv_a2)
    w_lr = w_lr.at[RWKV_DECAY_RANK + RWKV_ICLR_RANK:, 2 * RWKV_WIDTH:].set(rwkv_g2)
    y_rwkv = _rwkv_mixer(
        p_rwkv.reshape(bsz, seq, RWKV_IN), _row(rwkv_mu), w_lr, _row(rwkv_w0), _row(rwkv_a0), _row(rwkv_k_k),
        _row(rwkv_k_a), _row(rwkv_r_k), _row(rwkv_ln_g), _row(rwkv_ln_b))

    w_a2 = jnp.zeros((LANES, GLA_QK_WIDTH), F32).at[0:GLA_GATE_RANK].set(gla_w_a2)
    y_gla = _gla_mixer(p_gla.reshape(bsz, seq, GLA_IN), w_a2, _row(gla_b_a), _row(gla_norm_g))

    n_tok = bsz * seq
    return (y_ssd.reshape(n_tok, SSD_WIDTH), y_rwkv.reshape(n_tok, RWKV_WIDTH),
            y_gla.reshape(n_tok, GLA_WIDTH))


def _moe_layer(h1, h1p, eid, gate, counts, layer, w_gate, w_up, w_down, ln_g, ln_b):
    n_tok = h1.shape[0]
    tm = TOKEN_TILE
    n_blocks = n_tok * TOP_K // ROW_BLOCK + N_EXPERTS
    dest = _dispatch_plan(eid, counts)
    blocks = jnp.ceil(counts[0, 0:N_EXPERTS] / ROW_BLOCK).astype(jnp.int32)
    block_end = jnp.cumsum(blocks)
    block_ids = jnp.arange(n_blocks, dtype=jnp.int32)
    block_expert = jnp.minimum(jnp.sum((block_end[None, :] <= block_ids[:, None]).astype(jnp.int32), axis=1),
                               N_EXPERTS - 1)
    n_used = block_end[N_EXPERTS - 1:N_EXPERTS]
    dest3 = dest[:, 0:TOP_K].reshape(n_tok // tm, 1, tm * TOP_K)
    xs = _scatter_rows(dest3, h1p, n_blocks * ROW_BLOCK)
    ys = _expert_mlp(block_expert, n_used, xs, w_gate, w_up, w_down, layer)
    return _combine(dest3, ys, h1, gate, ln_g, ln_b)


def kernel(x, ln_in_g, ln_in_b, w_in, ssd_conv_w, ssd_conv_b, ssd_dt_bias, ssd_a_log, ssd_d, ssd_norm_g, rwkv_mu, rwkv_w0, rwkv_w2, rwkv_a0, rwkv_a2, rwkv_g2, rwkv_k_k, rwkv_k_a, rwkv_r_k, rwkv_ln_g, rwkv_ln_b, gla_w_a2, gla_b_a, gla_norm_g, w_out, ln1_g, ln1_b, moe_w_rg, moe_b_rg, moe_w_re, moe_b_re, moe_w_gate, moe_w_up, moe_w_down, ln2_g, ln2_b):
    bsz, seq, d = x.shape
    n_tok = bsz * seq
    h = x.reshape(n_tok, d)
    h_g = _row(ln_in_g)
    h_b = _row(ln_in_b)
    for i in range(w_in.shape[0]):
        first = i == 0
        y_ssd, y_rwkv, y_gla = _mixer_layer(
            h, h_g, h_b, first, w_in[i], ssd_conv_w[i], ssd_conv_b[i], ssd_dt_bias[i], ssd_a_log[i], ssd_d[i], ssd_norm_g[i],
            rwkv_mu[i], rwkv_w0[i], rwkv_w2[i], rwkv_a0[i], rwkv_a2[i], rwkv_g2[i], rwkv_k_k[i],
            rwkv_k_a[i], rwkv_r_k[i], rwkv_ln_g[i], rwkv_ln_b[i], gla_w_a2[i], gla_b_a[i], gla_norm_g[i],
            bsz, seq)
        w_r = _pad_lanes(jnp.concatenate([moe_w_rg[i], moe_w_re[i]], 1).astype(F32), LANES)
        b_r = _row(jnp.concatenate([moe_b_rg[i], moe_b_re[i]]), LANES)
        h1, h1p, eid, gate, counts = _out_projection(
            y_ssd, y_rwkv, y_gla, h, w_out[i].astype(BF16), _row(ln1_g[i]), _row(ln1_b[i]), w_r, b_r,
            h_g, h_b, first)
        h = _moe_layer(h1, h1p, eid, gate, counts, i, moe_w_gate, moe_w_up, moe_w_down, _row(ln2_g[i]), _row(ln2_b[i]))
    return h.reshape(bsz, seq, d)
```

```python
import functools
import math

import jax
import jax.numpy as jnp
from jax import lax
from jax.experimental import pallas as pl
from jax.experimental.pallas import tpu as pltpu

F32 = jnp.float32
BF16 = jnp.bfloat16

D_MODEL = 1024
DEPTH = 2

SSD_HEAD_DIM = 64
SSD_WIDTH = D_MODEL // 2
SSD_HEADS = SSD_WIDTH // SSD_HEAD_DIM
SSD_GROUPS = 2
SSD_STATE = 64
SSD_CONV = 4
SSD_BC = SSD_GROUPS * SSD_STATE
SSD_XBC = SSD_WIDTH + 2 * SSD_BC

RWKV_HEAD_DIM = 64
RWKV_WIDTH = D_MODEL // 4
RWKV_HEADS = RWKV_WIDTH // RWKV_HEAD_DIM
RWKV_DECAY_RANK = 32
RWKV_ICLR_RANK = 32
RWKV_GATE_RANK = 64
RWKV_LR = RWKV_DECAY_RANK + RWKV_ICLR_RANK + RWKV_GATE_RANK
RWKV_IN = 3 * RWKV_WIDTH + RWKV_LR
RWKV_GN_EPS = 64e-5

GLA_VALUE_DIM = 64
GLA_WIDTH = D_MODEL // 4
GLA_HEADS = GLA_WIDTH // GLA_VALUE_DIM
GLA_KEY_DIM = GLA_VALUE_DIM // 2
GLA_QK_WIDTH = GLA_HEADS * GLA_KEY_DIM
GLA_GATE_RANK = 16
GLA_GATE_TEMP = 16.0

MIX_WIDTH = SSD_WIDTH + RWKV_WIDTH + GLA_WIDTH

N_GROUPS = 4
EXPERTS_PER_GROUP = 8
N_EXPERTS = N_GROUPS * EXPERTS_PER_GROUP
TOP_K = 2
EXPERT_FF = 512

DEEPNORM_ALPHA = (2 * DEPTH) ** 0.25
LN_EPS = 1e-5
RMS_EPS = 1e-6

LANES = 128
SUBLANES = 8
SSD_IN = SSD_WIDTH + SSD_XBC + LANES
GLA_IN = 2 * GLA_QK_WIDTH + 2 * GLA_WIDTH + LANES

TOKEN_TILE = 512
SSD_BLOCK = 128
RWKV_BLOCK = 64
GLA_BLOCK = 64
CHUNKS_PER_STEP = 8
RWKV_CHUNKS_PER_STEP = 2
ROW_BLOCK = 512
DMA_UNROLL = 8
VMEM_LIMIT = 48 * 1024 * 1024


def _dot(a, b):
    return jnp.dot(a.astype(BF16), b.astype(BF16), preferred_element_type=F32)


def _split_bf16(x):
    hi = x.astype(BF16)
    return hi, (x - hi.astype(F32)).astype(BF16)


def _dot_rhs2(a_exact, b):
    b_hi, b_lo = _split_bf16(b)
    return (jnp.dot(a_exact, b_hi, preferred_element_type=F32)
            + jnp.dot(a_exact, b_lo, preferred_element_type=F32))


def _dot_lhs2(a, b_exact):
    a_hi, a_lo = _split_bf16(a)
    return (jnp.dot(a_hi, b_exact, preferred_element_type=F32)
            + jnp.dot(a_lo, b_exact, preferred_element_type=F32))


def _cumsum_rows(mask, x):
    m = mask.astype(BF16)
    x_hi, rest = _split_bf16(x)
    x_mid, x_lo = _split_bf16(x - x_hi.astype(F32))
    del rest
    return (jnp.dot(m, x_hi, preferred_element_type=F32)
            + (jnp.dot(m, x_mid, preferred_element_type=F32) + jnp.dot(m, x_lo, preferred_element_type=F32)))


def _dot_x3(a, b):
    a_hi, a_lo = _split_bf16(a)
    b_hi, b_lo = _split_bf16(b)
    return (jnp.dot(a_hi, b_hi, preferred_element_type=F32)
            + (jnp.dot(a_hi, b_lo, preferred_element_type=F32)
               + jnp.dot(a_lo, b_hi, preferred_element_type=F32)))


def _pack_bf16_pair(x):
    m = x.shape[1] // 2
    lo = lax.bitcast_convert_type(x[:, 0:m].astype(BF16).astype(F32), jnp.uint32)
    hi = lax.bitcast_convert_type(x[:, m:2 * m].astype(BF16).astype(F32), jnp.uint32)
    return (lo >> 16) | (hi & jnp.uint32(0xFFFF0000))


def _unpack_bf16_pair(p):
    lo = lax.bitcast_convert_type(p << 16, F32)
    hi = lax.bitcast_convert_type(p & jnp.uint32(0xFFFF0000), F32)
    return lo, hi


def _dot_nt(a, b, precision=None):
    return lax.dot_general(a, b, (((1,), (1,)), ((), ())), precision=precision,
                           preferred_element_type=F32)


def _dot_tn(a, b, precision=None):
    return lax.dot_general(a, b, (((0,), (0,)), ((), ())), precision=precision,
                           preferred_element_type=F32)


def _sigmoid(x):
    return 1.0 / (1.0 + jnp.exp(-x))


def _silu(x):
    return x * _sigmoid(x)


def _softplus(x):
    return jnp.maximum(x, 0.0) + jnp.log1p(jnp.exp(-jnp.abs(x)))


def _iota(shape, dim):
    return lax.broadcasted_iota(jnp.int32, shape, dim)


def _block_indicator(rows, cols, row_seg, col_seg):
    r = _iota((rows, cols), 0) // row_seg
    c = _iota((rows, cols), 1) // col_seg
    return (r == c).astype(F32)


def _layer_norm(x, g, b):
    mu = jnp.mean(x, -1, keepdims=True)
    xc = x - mu
    var = jnp.mean(xc * xc, -1, keepdims=True)
    return xc * lax.rsqrt(var + LN_EPS) * g + b


def _params(*sem):
    return pltpu.CompilerParams(dimension_semantics=sem, vmem_limit_bytes=VMEM_LIMIT)


def _inproj_kernel(h_ref, g_ref, b_ref, w1_ref, w2_ref, w3_ref, o1_ref, o2_ref, o3_ref, *, normalize_input):
    h = h_ref[...]
    if normalize_input:
        h = _layer_norm(h, g_ref[...], b_ref[...])
    hb = h.astype(BF16)
    o1_ref[...] = jnp.dot(hb, w1_ref[...], preferred_element_type=F32)
    o2_ref[...] = jnp.dot(hb, w2_ref[...], preferred_element_type=F32)
    o3_ref[...] = jnp.dot(hb, w3_ref[...], preferred_element_type=F32)


def _in_projection(h, h_g, h_b, w_ssd, w_rwkv, w_gla, normalize_input):
    n_tok, d = h.shape
    tm = TOKEN_TILE
    consts = (h_g, h_b, w_ssd, w_rwkv, w_gla)
    ws = (w_ssd, w_rwkv, w_gla)
    return pl.pallas_call(
        functools.partial(_inproj_kernel, normalize_input=normalize_input), grid=(n_tok // tm,),
        in_specs=[pl.BlockSpec((tm, d), lambda i: (i, 0))]
        + [pl.BlockSpec(a.shape, lambda i: (0, 0)) for a in consts],
        out_specs=[pl.BlockSpec((tm, w.shape[1]), lambda i: (i, 0)) for w in ws],
        out_shape=[jax.ShapeDtypeStruct((n_tok, w.shape[1]), F32) for w in ws],
        compiler_params=_params("parallel"), name="in_projection")(h, *consts)


def _ssd_kernel(p_ref, cw_ref, cb_ref, dtb_ref, a_ref, d_ref, ng_ref, o_ref, xs_ref, st_ref):
    @pl.when(pl.program_id(1) == 0)
    def _():
        xs_ref[0:SUBLANES, :] = jnp.zeros((SUBLANES, SSD_XBC), F32)
        st_ref[...] = jnp.zeros(st_ref.shape, F32)

    for s in range(o_ref.shape[0] // SSD_BLOCK):
        rows = pl.ds(s * SSD_BLOCK, SSD_BLOCK)
        _ssd_chunk(p_ref.at[rows], cw_ref, cb_ref, dtb_ref, a_ref, d_ref, ng_ref, o_ref.at[rows], xs_ref, st_ref)


def _ssd_chunk(p_ref, cw_ref, cb_ref, dtb_ref, a_ref, d_ref, ng_ref, o_ref, xs_ref, st_ref):
    blk = o_ref.shape[0]
    tail = SUBLANES
    z = p_ref[:, 0:SSD_WIDTH]
    xs_ref[tail:tail + blk, :] = p_ref[:, SSD_WIDTH:SSD_WIDTH + SSD_XBC]
    dt_raw = p_ref[:, SSD_WIDTH + SSD_XBC:SSD_IN]

    acc = jnp.broadcast_to(cb_ref[...], (blk, SSD_XBC))
    for i in range(SSD_CONV):
        acc = acc + cw_ref[i:i + 1, :] * xs_ref[pl.ds(tail - (SSD_CONV - 1) + i, blk), :]
    xs_ref[0:tail, :] = xs_ref[blk:blk + tail, :]
    xbc = _silu(acc)
    xh = xbc[:, 0:SSD_WIDTH]
    bm = xbc[:, SSD_WIDTH:SSD_WIDTH + SSD_BC]
    cm = xbc[:, SSD_WIDTH + SSD_BC:SSD_XBC]

    dt = _softplus(dt_raw + dtb_ref[...])
    a_dt = dt * a_ref[...]
    row = _iota((blk, blk), 0)
    col = _iota((blk, blk), 1)
    causal = col <= row
    a_cs = _cumsum_rows(causal, a_dt)
    a_cs_t = a_cs.T
    a_last = a_cs[blk - 1:blk, :]

    expand = _block_indicator(LANES, SSD_WIDTH, 1, SSD_HEAD_DIM).astype(BF16)
    per_head = jnp.concatenate([dt, jnp.exp(a_cs), jnp.exp(a_last - a_cs),
                                jnp.broadcast_to(jnp.exp(a_last), (SUBLANES, LANES))], axis=0)
    per_chan = _dot_lhs2(per_head, expand)
    dt_e = per_chan[0:blk]
    dec_e = per_chan[blk:2 * blk]
    te_e = per_chan[2 * blk:3 * blk]
    cd_e = per_chan[3 * blk:3 * blk + 1]

    x_dt = xh * dt_e
    state = st_ref[...]
    y_off = _dot(cm, state) * dec_e

    lane = _iota((blk, LANES), 1)
    hpg = SSD_HEADS // SSD_GROUPS
    pairs = []
    for j in range(SSD_HEADS // 2):
        grp = (2 * j) // hpg
        in_grp = (lane // SSD_STATE) == grp
        cb = _dot_nt(jnp.where(in_grp, cm, 0.0).astype(BF16), bm.astype(BF16))
        xp = x_dt[:, j * LANES:(j + 1) * LANES]
        ys = []
        for h in (2 * j, 2 * j + 1):
            diff = jnp.broadcast_to(a_cs[:, h:h + 1], (blk, blk)) - a_cs_t[h:h + 1, :]
            seg = jnp.exp(jnp.where(causal, diff, -jnp.inf))
            ys.append(_dot(cb * seg, xp))
        pairs.append(jnp.where(lane < SSD_HEAD_DIM, ys[0], ys[1]))
    y = jnp.concatenate(pairs, axis=1) + y_off + d_ref[...] * xh

    new = _dot_tn(bm.astype(BF16), (x_dt * te_e).astype(BF16))
    keep = _block_indicator(SSD_BC, SSD_WIDTH, SSD_STATE, SSD_WIDTH // SSD_GROUPS)
    st_ref[...] = state * cd_e + keep * new

    y = y * _silu(z)
    gw = SSD_WIDTH // SSD_GROUPS
    for g in range(SSD_GROUPS):
        yg = y[:, g * gw:(g + 1) * gw]
        ms = jnp.mean(yg * yg, -1, keepdims=True)
        o_ref[:, g * gw:(g + 1) * gw] = (yg * lax.rsqrt(ms + RMS_EPS)
                                         * ng_ref[:, g * gw:(g + 1) * gw]).astype(o_ref.dtype)


def _ssd_mixer(p, conv_w, conv_b, dt_bias, a_neg, d_skip, norm_g):
    bsz, seq, _ = p.shape
    blk = SSD_BLOCK * CHUNKS_PER_STEP
    small = lambda a: pl.BlockSpec(a.shape, lambda b, c: (0, 0))
    args = (conv_w, conv_b, dt_bias, a_neg, d_skip, norm_g)
    return pl.pallas_call(
        _ssd_kernel, grid=(bsz, seq // blk),
        in_specs=[pl.BlockSpec((None, blk, SSD_IN), lambda b, c: (b, c, 0))] + [small(a) for a in args],
        out_specs=pl.BlockSpec((None, blk, SSD_WIDTH), lambda b, c: (b, c, 0)),
        out_shape=jax.ShapeDtypeStruct((bsz, seq, SSD_WIDTH), BF16),
        scratch_shapes=[pltpu.VMEM((SSD_BLOCK + 2 * SUBLANES, SSD_XBC), F32),
                        pltpu.VMEM((SSD_BC, SSD_WIDTH), F32)],
        compiler_params=_params("parallel", "arbitrary"), name="ssd_mixer")(p, *args)


def _rwkv_kernel(p_ref, mu_ref, wlr_ref, w0_ref, a0_ref, kk_ref, ka_ref, rk_ref, lng_ref, lnb_ref,
                 o_ref, xs_ref, st_ref):
    @pl.when(pl.program_id(0) == 0)
    def _():
        xs_ref[:, 0:SUBLANES, :] = jnp.zeros((xs_ref.shape[0], SUBLANES, RWKV_IN), F32)
        st_ref[...] = jnp.zeros(st_ref.shape, F32)

    for s in range(o_ref.shape[1] // RWKV_BLOCK):
        _rwkv_chunk(p_ref, mu_ref, wlr_ref, w0_ref, a0_ref, kk_ref, ka_ref, rk_ref, lng_ref, lnb_ref,
                    o_ref, xs_ref, st_ref, s * RWKV_BLOCK)


def _rwkv_chunk(p_ref, mu_ref, wlr_ref, w0_ref, a0_ref, kk_ref, ka_ref, rk_ref, lng_ref, lnb_ref,
                o_ref, xs_ref, st_ref, frame0):
    nb = o_ref.shape[0]
    blk = RWKV_BLOCK
    rows = nb * blk
    tail = SUBLANES
    width = RWKV_WIDTH

    ps, prevs = [], []
    for b in range(nb):
        ps.append(p_ref[b, frame0:frame0 + blk, :])
        xs_ref[b, tail:tail + blk, :] = ps[b]
        prevs.append(xs_ref[b, pl.ds(tail - 1, blk), :])
        xs_ref[b, 0:tail, :] = xs_ref[b, blk:blk + tail, :]
    p = jnp.concatenate(ps, axis=0)
    prev = jnp.concatenate(prevs, axis=0)
    pm = p + (prev - p) * mu_ref[...]
    r = pm[:, 0:width]
    k = pm[:, width:2 * width]
    v = pm[:, 2 * width:3 * width]
    lr = pm[:, 3 * width:RWKV_IN]

    lane = _iota((rows, RWKV_LR), 1)
    lr_act = jnp.where(lane < RWKV_DECAY_RANK, jnp.tanh(lr),
                       jnp.where(lane < RWKV_DECAY_RANK + RWKV_ICLR_RANK, lr, _sigmoid(lr)))
    proj = _dot(lr_act, wlr_ref[...])
    w = -_softplus(-(w0_ref[...] + proj[:, 0:width])) - 0.5
    log_decay = -jnp.exp(w)
    a = _sigmoid(a0_ref[...] + proj[:, width:2 * width])
    g = proj[:, 2 * width:3 * width]

    head_sum = _block_indicator(width, width, RWKV_HEAD_DIM, RWKV_HEAD_DIM)
    head_sum_b = head_sum.astype(BF16)
    kk = k * kk_ref[...]
    kk = kk * lax.rsqrt(_dot(kk * kk, head_sum_b) + 1e-12)
    k2 = k * (1.0 + (a - 1.0) * ka_ref[...])
    alpha = -kk
    beta = kk * a

    row = _iota((rows, rows), 0)
    col = _iota((rows, rows), 1)
    same_row = (row // blk) == (col // blk)
    incl = same_row & (col <= row)
    strict = same_row & (col < row)
    cs = _dot_rhs2(incl.astype(BF16), log_decay)
    e_neg = jnp.exp(-cs)
    a_t = alpha * jnp.exp(cs - log_decay)
    b_t = (beta * e_neg).astype(BF16)
    k_t = (k2 * e_neg).astype(BF16)
    r_t = r * jnp.exp(cs)
    v_b = v.astype(BF16)

    states = [st_ref[b] for b in range(nb)]
    reads = []
    for b in range(nb):
        sl = slice(b * blk, (b + 1) * blk)
        reads.append(_dot(jnp.concatenate([a_t[sl], r_t[sl]], axis=0), states[b]))
    a_s = jnp.concatenate([x[0:blk] for x in reads], axis=0)
    r_s = jnp.concatenate([x[blk:2 * blk] for x in reads], axis=0)

    wlane = _iota((rows, width), 1)
    in_head = [(wlane // RWKV_HEAD_DIM) == h for h in range(RWKV_HEADS)]
    parts = []
    for h in range(RWKV_HEADS):
        parts += [jnp.where(in_head[h], a_t, 0.0), jnp.where(in_head[h], r_t, 0.0)]
    lhs = jnp.concatenate(parts, axis=0).astype(BF16)
    pb_all = _dot_nt(lhs, b_t)
    pk_all = _dot_nt(lhs, k_t)

    heads = range(RWKV_HEADS)
    base = [2 * h * rows for h in heads]
    npow = [jnp.where(strict, pb_all[base[h]:base[h] + rows], 0.0).astype(BF16) for h in heads]
    u = [a_s + _dot(jnp.where(strict, pk_all[base[h]:base[h] + rows], 0.0), v_b) for h in heads]
    steps = int(math.log2(blk))
    for i in range(steps):
        u = [u[h] + _dot(npow[h], u[h]) for h in heads]
        if i + 1 < steps:
            npow = [_dot(npow[h], npow[h]).astype(BF16) for h in heads]
    y = [_dot(jnp.where(incl, pb_all[base[h] + rows:base[h] + 2 * rows], 0.0), u[h])
         + _dot(jnp.where(incl, pk_all[base[h] + rows:base[h] + 2 * rows], 0.0), v_b) for h in heads]
    u_all = u[0]
    y_all = y[0]
    for h in range(1, RWKV_HEADS):
        u_all = jnp.where(in_head[h], u[h], u_all)
        y_all = jnp.where(in_head[h], y[h], y_all)
    y_all = y_all + r_s

    for b in range(nb):
        sl = slice(b * blk, (b + 1) * blk)
        c_last = cs[(b + 1) * blk - 1:(b + 1) * blk, :]
        to_end = jnp.exp(c_last - cs[sl])
        new = (_dot_tn((beta[sl] * to_end).astype(BF16), u_all[sl].astype(BF16))
               + _dot_tn((k2[sl] * to_end).astype(BF16), v_b[sl]))
        chunk_decay = jnp.broadcast_to(jnp.exp(c_last), (SUBLANES, width)).T[:, 0:1]
        st_ref[b] = states[b] * chunk_decay + head_sum * new

    head_mean_b = (head_sum * (1.0 / RWKV_HEAD_DIM)).astype(BF16)
    mean = _dot_lhs2(y_all, head_mean_b)
    yc = y_all - mean
    var = _dot(yc * yc, head_mean_b)
    yn = yc * lax.rsqrt(var + RWKV_GN_EPS) * lng_ref[...] + lnb_ref[...]
    bonus = _dot(r * k2 * rk_ref[...], head_sum_b) * v
    out = ((yn + bonus) * g).astype(o_ref.dtype)
    for b in range(nb):
        o_ref[b, frame0:frame0 + blk, :] = out[b * blk:(b + 1) * blk]


def _rwkv_mixer(p, mu, wlr, w0, a0, k_k, k_a, r_k, ln_g, ln_b):
    bsz, seq, _ = p.shape
    blk = RWKV_BLOCK * RWKV_CHUNKS_PER_STEP
    small = lambda a: pl.BlockSpec(a.shape, lambda c: (0, 0))
    args = (mu, wlr, w0, a0, k_k, k_a, r_k, ln_g, ln_b)
    return pl.pallas_call(
        _rwkv_kernel, grid=(seq // blk,),
        in_specs=[pl.BlockSpec((bsz, blk, RWKV_IN), lambda c: (0, c, 0))] + [small(a) for a in args],
        out_specs=pl.BlockSpec((bsz, blk, RWKV_WIDTH), lambda c: (0, c, 0)),
        out_shape=jax.ShapeDtypeStruct((bsz, seq, RWKV_WIDTH), BF16),
        scratch_shapes=[pltpu.VMEM((bsz, RWKV_BLOCK + 2 * SUBLANES, RWKV_IN), F32),
                        pltpu.VMEM((bsz, RWKV_WIDTH, RWKV_WIDTH), F32)],
        compiler_params=_params("arbitrary"), name="rwkv7_mixer")(p, *args)


def _gla_kernel(p_ref, wa_ref, ba_ref, ng_ref, o_ref, st_ref, z_ref, acc_ref):
    @pl.when(pl.program_id(1) == 0)
    def _():
        st_ref[...] = jnp.zeros(st_ref.shape, F32)

    for s in range(o_ref.shape[0] // GLA_BLOCK):
        rows = pl.ds(s * GLA_BLOCK, GLA_BLOCK)
        _gla_chunk(p_ref.at[rows], wa_ref, ba_ref, ng_ref, o_ref.at[rows], st_ref, z_ref, acc_ref)


def _gla_chunk(p_ref, wa_ref, ba_ref, ng_ref, o_ref, st_ref, z_ref, acc_ref):
    blk = o_ref.shape[0]
    qk = GLA_QK_WIDTH
    vw = GLA_WIDTH
    q = p_ref[:, 0:qk] * (GLA_KEY_DIM ** -0.5)
    k = p_ref[:, qk:2 * qk]
    g = p_ref[:, 2 * qk + vw:2 * qk + 2 * vw]
    a_lr = p_ref[:, 2 * qk + 2 * vw:GLA_IN]
    v_off = 2 * qk

    logit = _dot_x3(a_lr, wa_ref[...]) + ba_ref[...]
    log_a = -_softplus(-logit) / GLA_GATE_TEMP
    row = _iota((blk, blk), 0)
    col = _iota((blk, blk), 1)
    cum = _cumsum_rows(col <= row, log_a)

    state = st_ref[...]
    acc_ref[...] = _dot_nt((q * jnp.exp(cum)).astype(BF16), state.astype(BF16))

    spread = _block_indicator(qk, vw, GLA_KEY_DIM, GLA_VALUE_DIM).astype(BF16)
    rows = _iota((blk, qk), 0)
    group = 2 * SUBLANES
    for jb in range(blk // group):
        r0 = jb * group
        n = blk - r0
        for jj in range(group):
            j = r0 + jj
            decay = jnp.exp(jnp.where(rows[r0:] >= j, cum[r0:] - cum[j:j + 1, :], -jnp.inf))
            z_ref[jj * n:(jj + 1) * n, :] = (q[r0:] * decay * k[j:j + 1, :]).astype(BF16)
        res = jnp.dot(z_ref[0:group * n, :], spread, preferred_element_type=F32)
        part = jnp.zeros((n, vw), F32)
        for jj in range(group):
            j = r0 + jj
            part = part + res[jj * n:(jj + 1) * n] * p_ref[j:j + 1, v_off:v_off + vw]
        acc_ref[r0:blk, :] = acc_ref[r0:blk, :] + part

    v = p_ref[:, v_off:v_off + vw]
    last = cum[blk - 1:blk, :]
    k_dec = k * jnp.exp(last - cum)
    keep = _block_indicator(vw, qk, GLA_VALUE_DIM, GLA_KEY_DIM)
    st_ref[...] = state * jnp.exp(last) + keep * _dot_tn(v.astype(BF16), k_dec.astype(BF16))

    o = acc_ref[...]
    head_mean = (_block_indicator(vw, vw, GLA_VALUE_DIM, GLA_VALUE_DIM) * (1.0 / GLA_VALUE_DIM)).astype(BF16)
    ms = _dot_lhs2(o * o, head_mean)
    o_ref[...] = (o * lax.rsqrt(ms + RMS_EPS) * ng_ref[...] * _silu(g)).astype(o_ref.dtype)


def _gla_mixer(p, w_a2, b_a, norm_g):
    bsz, seq, _ = p.shape
    blk = GLA_BLOCK * CHUNKS_PER_STEP
    small = lambda a: pl.BlockSpec(a.shape, lambda b, c: (0, 0))
    args = (w_a2, b_a, norm_g)
    return pl.pallas_call(
        _gla_kernel, grid=(bsz, seq // blk),
        in_specs=[pl.BlockSpec((None, blk, GLA_IN), lambda b, c: (b, c, 0))] + [small(a) for a in args],
        out_specs=pl.BlockSpec((None, blk, GLA_WIDTH), lambda b, c: (b, c, 0)),
        out_shape=jax.ShapeDtypeStruct((bsz, seq, GLA_WIDTH), BF16),
        scratch_shapes=[pltpu.VMEM((GLA_WIDTH, GLA_QK_WIDTH), F32),
                        pltpu.VMEM((2 * SUBLANES * GLA_BLOCK, GLA_QK_WIDTH), BF16),
                        pltpu.VMEM((GLA_BLOCK, GLA_WIDTH), F32)],
        compiler_params=_params("parallel", "arbitrary"), name="gla_mixer")(p, *args)


def _outproj_kernel(ys_ref, yr_ref, yg_ref, h_ref, w1_ref, w2_ref, w3_ref, g_ref, b_ref, wr_ref, br_ref,
                    hg_ref, hb_ref, h1_ref, h1p_ref, eid_ref, gate_ref, cnt_ref, *, normalize_input):
    mix = (jnp.dot(ys_ref[...], w1_ref[...], preferred_element_type=F32)
           + jnp.dot(yr_ref[...], w2_ref[...], preferred_element_type=F32)
           + jnp.dot(yg_ref[...], w3_ref[...], preferred_element_type=F32))
    h = h_ref[...]
    if normalize_input:
        h = _layer_norm(h, hg_ref[...], hb_ref[...])
    h1 = _layer_norm(DEEPNORM_ALPHA * h + mix, g_ref[...], b_ref[...])
    h1_ref[...] = h1
    h1p_ref[...] = _pack_bf16_pair(h1)

    logits = _dot_x3(h1, wr_ref[...]) + br_ref[...]
    tm = logits.shape[0]
    lane = _iota((tm, LANES), 1)
    lane_f = lane.astype(F32)

    def masked_softmax(mask):
        m = jnp.max(jnp.where(mask, logits, -jnp.inf), -1, keepdims=True)
        e = jnp.where(mask, jnp.exp(logits - m), 0.0)
        return e / jnp.sum(e, -1, keepdims=True)

    def first_argmax(vals, mask):
        m = jnp.max(jnp.where(mask, vals, -jnp.inf), -1, keepdims=True)
        idx = jnp.min(jnp.where(mask & (vals == m), lane_f, float(LANES)), -1, keepdims=True)
        return m, idx.astype(jnp.int32)

    is_group = lane < N_GROUPS
    g_prob = masked_softmax(is_group)
    g_w, g_idx = first_argmax(g_prob, is_group)
    e_lo = N_GROUPS + g_idx * EXPERTS_PER_GROUP
    in_group = (lane >= e_lo) & (lane < e_lo + EXPERTS_PER_GROUP)
    e_prob = masked_softmax(in_group)
    p1, i1 = first_argmax(e_prob, in_group)
    p2, i2 = first_argmax(e_prob, in_group & (lane != i1))
    denom = p1 + p2
    eid_ref[...] = jnp.where(lane == 0, i1 - N_GROUPS, jnp.where(lane == 1, i2 - N_GROUPS, 0))
    gate_ref[...] = jnp.where(lane == 0, g_w * (p1 / denom), jnp.where(lane == 1, g_w * (p2 / denom), 0.0))

    chosen = ((lane == i1 - N_GROUPS) | (lane == i2 - N_GROUPS)).astype(F32)

    @pl.when(pl.program_id(0) == 0)
    def _():
        cnt_ref[...] = jnp.zeros(cnt_ref.shape, F32)

    cnt_ref[...] = cnt_ref[...] + jnp.sum(chosen, 0, keepdims=True)


def _out_projection(y_ssd, y_rwkv, y_gla, h, w_out, ln_g, ln_b, w_r, b_r, h_g, h_b, normalize_input):
    n_tok, d = h.shape
    tm = TOKEN_TILE
    row = lambda n: pl.BlockSpec((tm, n), lambda i: (i, 0))
    full = lambda a: pl.BlockSpec(a.shape, lambda i: (0, 0))
    w1 = w_out[0:SSD_WIDTH]
    w2 = w_out[SSD_WIDTH:SSD_WIDTH + RWKV_WIDTH]
    w3 = w_out[SSD_WIDTH + RWKV_WIDTH:MIX_WIDTH]
    consts = (w1, w2, w3, ln_g, ln_b, w_r, b_r, h_g, h_b)
    return pl.pallas_call(
        functools.partial(_outproj_kernel, normalize_input=normalize_input), grid=(n_tok // tm,),
        in_specs=[row(SSD_WIDTH), row(RWKV_WIDTH), row(GLA_WIDTH), row(d)] + [full(a) for a in consts],
        out_specs=[row(d), row(d // 2), row(LANES), row(LANES), pl.BlockSpec((1, LANES), lambda i: (0, 0))],
        out_shape=[jax.ShapeDtypeStruct((n_tok, d), F32),
                   jax.ShapeDtypeStruct((n_tok, d // 2), jnp.uint32),
                   jax.ShapeDtypeStruct((n_tok, LANES), jnp.int32),
                   jax.ShapeDtypeStruct((n_tok, LANES), F32),
                   jax.ShapeDtypeStruct((1, LANES), F32)],
        compiler_params=_params("arbitrary"), name="out_projection")(y_ssd, y_rwkv, y_gla, h, *consts)


def _dest_kernel(eid_ref, cnt_ref, dest_ref, run_ref, base_ref):
    tm = eid_ref.shape[0]
    e = eid_ref[...]
    lane = _iota((tm, LANES), 1)
    oh0 = (lane == e[:, 0:1]).astype(F32)
    oh1 = (lane == e[:, 1:2]).astype(F32)
    oh = oh0 + oh1

    @pl.when(pl.program_id(0) == 0)
    def _():
        run_ref[...] = jnp.zeros(run_ref.shape, F32)
        blocks = jnp.ceil(cnt_ref[...] * (1.0 / ROW_BLOCK))
        before = (_iota((LANES, LANES), 0) < _iota((LANES, LANES), 1)).astype(BF16)
        start = jnp.dot(jnp.broadcast_to(blocks, (SUBLANES, LANES)).astype(BF16), before,
                        preferred_element_type=F32)[0:1, :]
        base_ref[...] = start * ROW_BLOCK

    earlier = (_iota((tm, tm), 1) < _iota((tm, tm), 0)).astype(BF16)
    rank = jnp.dot(earlier, oh.astype(BF16), preferred_element_type=F32)
    pos = rank + run_ref[...] + base_ref[...]
    d0 = jnp.sum(oh0 * pos, -1, keepdims=True)
    d1 = jnp.sum(oh1 * pos, -1, keepdims=True)
    dest_ref[...] = jnp.where(lane == 0, d0, jnp.where(lane == 1, d1, 0.0)).astype(jnp.int32)
    run_ref[...] = run_ref[...] + jnp.sum(oh, 0, keepdims=True)


def _dispatch_plan(eid, counts):
    n_tok = eid.shape[0]
    tm = TOKEN_TILE
    return pl.pallas_call(
        _dest_kernel, grid=(n_tok // tm,),
        in_specs=[pl.BlockSpec((tm, LANES), lambda i: (i, 0)), pl.BlockSpec((1, LANES), lambda i: (0, 0))],
        out_specs=pl.BlockSpec((tm, LANES), lambda i: (i, 0)),
        out_shape=jax.ShapeDtypeStruct((n_tok, LANES), jnp.int32),
        scratch_shapes=[pltpu.VMEM((1, LANES), F32), pltpu.VMEM((1, LANES), F32)],
        compiler_params=_params("arbitrary"), name="dispatch_plan")(eid, counts)


def _row_copy(src_ref, src_row, dst_ref, dst_row, sem):
    return pltpu.make_async_copy(src_ref.at[pl.ds(src_row, 1)], dst_ref.at[pl.ds(dst_row, 1)], sem)


def _scatter_kernel(dest_ref, h_ref, xs_in_ref, xs_ref, sem):
    del xs_in_ref
    tm = h_ref.shape[0]

    def start(i, carry):
        base = pl.multiple_of(i * SUBLANES, SUBLANES)
        tile = h_ref.at[pl.ds(base, SUBLANES)]
        for u in range(SUBLANES):
            for k in range(TOP_K):
                _row_copy(tile, u, xs_ref, dest_ref[0, 0, TOP_K * (base + u) + k], sem).start()
        return carry

    lax.fori_loop(0, tm // SUBLANES, start, 0)
    all_rows = xs_ref.at[pl.ds(0, TOP_K * tm)]
    pltpu.make_async_copy(all_rows, all_rows, sem).wait()


def _scatter_rows(dest3, h1, n_rows):
    n_tok, d = h1.shape
    n_tiles = dest3.shape[0]
    xs0 = jnp.zeros((n_rows, d), h1.dtype)
    return pl.pallas_call(
        _scatter_kernel, grid=(n_tiles,),
        in_specs=[pl.BlockSpec((1, 1, dest3.shape[2]), lambda i: (i, 0, 0), memory_space=pltpu.SMEM),
                  pl.BlockSpec((n_tok // n_tiles, d), lambda i: (i, 0)), pl.BlockSpec(memory_space=pl.ANY)],
        out_specs=pl.BlockSpec(memory_space=pl.ANY),
        out_shape=jax.ShapeDtypeStruct((n_rows, d), h1.dtype),
        scratch_shapes=[pltpu.SemaphoreType.DMA(())],
        input_output_aliases={2: 0},
        compiler_params=_params("arbitrary"), name="expert_scatter")(dest3, h1, xs0)


def _expert_kernel(be_ref, used_ref, x_ref, wg_ref, wu_ref, wd_ref, y_ref, wg_b, wu_b, wd_b):
    b = pl.program_id(0)

    @pl.when((b == 0) | (be_ref[b] != be_ref[jnp.maximum(b - 1, 0)]))
    def _():
        wg_b[...] = wg_ref[...].astype(BF16)
        wu_b[...] = wu_ref[...].astype(BF16)
        wd_b[...] = wd_ref[...].astype(BF16)

    @pl.when(b < used_ref[0])
    def _():
        x_lo, x_hi = _unpack_bf16_pair(x_ref[...])
        x = jnp.concatenate([x_lo.astype(BF16), x_hi.astype(BF16)], axis=1)
        gate = jnp.dot(x, wg_b[...], preferred_element_type=F32)
        up = jnp.dot(x, wu_b[...], preferred_element_type=F32)
        mid = (_silu(gate) * up).astype(BF16)
        y_ref[...] = _pack_bf16_pair(jnp.dot(mid, wd_b[...], preferred_element_type=F32))

    @pl.when(b >= used_ref[0])
    def _():
        y_ref[...] = jnp.zeros(y_ref.shape, y_ref.dtype)


def _expert_mlp(block_expert, n_used, xs, w_gate, w_up, w_down, layer):
    n_rows, dp = xs.shape
    n_blocks = n_rows // ROW_BLOCK
    d, ff = w_gate.shape[-2:]
    return pl.pallas_call(
        _expert_kernel,
        grid_spec=pltpu.PrefetchScalarGridSpec(
            num_scalar_prefetch=2, grid=(n_blocks,),
            in_specs=[pl.BlockSpec((ROW_BLOCK, dp), lambda b, be, nu: (b, 0)),
                      pl.BlockSpec((None, None, d, ff), lambda b, be, nu: (layer, be[b], 0, 0)),
                      pl.BlockSpec((None, None, d, ff), lambda b, be, nu: (layer, be[b], 0, 0)),
                      pl.BlockSpec((None, None, ff, d), lambda b, be, nu: (layer, be[b], 0, 0))],
            out_specs=pl.BlockSpec((ROW_BLOCK, dp), lambda b, be, nu: (b, 0)),
            scratch_shapes=[pltpu.VMEM((d, ff), BF16), pltpu.VMEM((d, ff), BF16), pltpu.VMEM((ff, d), BF16)]),
        out_shape=jax.ShapeDtypeStruct((n_rows, dp), jnp.uint32),
        compiler_params=_params("arbitrary"), name="expert_mlp")(
            block_expert, n_used, xs, w_gate, w_up, w_down)


def _combine_kernel(dest_ref, y_ref, h_ref, gate_ref, g_ref, b_ref, o_ref, buf_ref, sem):
    tm = o_ref.shape[0]

    def start(i, carry):
        base = pl.multiple_of(i * SUBLANES, SUBLANES)
        for k in range(TOP_K):
            tile = buf_ref.at[k, pl.ds(base, SUBLANES)]
            for u in range(SUBLANES):
                _row_copy(y_ref, dest_ref[0, 0, TOP_K * (base + u) + k], tile, u, sem).start()
        return carry

    lax.fori_loop(0, tm // SUBLANES, start, 0)
    pltpu.make_async_copy(buf_ref, buf_ref, sem).wait()
    gate = gate_ref[...]
    lo0, hi0 = _unpack_bf16_pair(buf_ref[0])
    lo1, hi1 = _unpack_bf16_pair(buf_ref[1])
    g0 = gate[:, 0:1]
    g1 = gate[:, 1:2]
    ffn = jnp.concatenate([g0 * lo0 + g1 * lo1, g0 * hi0 + g1 * hi1], axis=1)
    o_ref[...] = _layer_norm(DEEPNORM_ALPHA * h_ref[...] + ffn, g_ref[...], b_ref[...])


def _combine(dest3, ys, h1, gate, ln_g, ln_b):
    n_tok, d = h1.shape
    tm = TOKEN_TILE
    row = lambda n: pl.BlockSpec((tm, n), lambda i: (i, 0))
    vec = pl.BlockSpec((1, d), lambda i: (0, 0))
    return pl.pallas_call(
        _combine_kernel, grid=(n_tok // tm,),
        in_specs=[pl.BlockSpec((1, 1, dest3.shape[2]), lambda i: (i, 0, 0), memory_space=pltpu.SMEM),
                  pl.BlockSpec(memory_space=pl.ANY), row(d), row(LANES), vec, vec],
        out_specs=row(d),
        out_shape=jax.ShapeDtypeStruct((n_tok, d), F32),
        scratch_shapes=[pltpu.VMEM((TOP_K, tm, ys.shape[1]), ys.dtype), pltpu.SemaphoreType.DMA(())],
        compiler_params=_params("arbitrary"), name="expert_combine")(dest3, ys, h1, gate, ln_g, ln_b)


def _pad_lanes(a, n):
    return jnp.pad(a, [(0, 0)] * (a.ndim - 1) + [(0, n - a.shape[-1])])


def _row(a, n=None):
    a = a.reshape(1, -1).astype(F32)
    return a if n is None else _pad_lanes(a, n)


def _mixer_layer(h, h_g, h_b, normalize_input, w_in, ssd_conv_w, ssd_conv_b, ssd_dt_bias, ssd_a_log, ssd_d, ssd_norm_g,
                 rwkv_mu, rwkv_w0, rwkv_w2, rwkv_a0, rwkv_a2, rwkv_g2, rwkv_k_k, rwkv_k_a, rwkv_r_k,
                 rwkv_ln_g, rwkv_ln_b, gla_w_a2, gla_b_a, gla_norm_g, bsz, seq):
    o = 0
    cols = {}
    for name, n in (("z", SSD_WIDTH), ("xbc", SSD_XBC), ("dt", SSD_HEADS), ("rwkv", RWKV_IN),
                    ("q", GLA_QK_WIDTH), ("k", GLA_QK_WIDTH), ("v", GLA_WIDTH), ("g", GLA_WIDTH),
                    ("ga", GLA_GATE_RANK)):
        cols[name] = w_in[:, o:o + n]
        o += n
    w_ssd = jnp.concatenate([cols["z"], cols["xbc"], _pad_lanes(cols["dt"], LANES)], 1).astype(BF16)
    w_rwkv = cols["rwkv"].astype(BF16)
    w_gla = jnp.concatenate([cols["q"], cols["k"], cols["v"], cols["g"], _pad_lanes(cols["ga"], LANES)],
                            1).astype(BF16)
    p_ssd, p_rwkv, p_gla = _in_projection(h, h_g, h_b, w_ssd, w_rwkv, w_gla, normalize_input)

    y_ssd = _ssd_mixer(
        p_ssd.reshape(bsz, seq, SSD_IN), ssd_conv_w.astype(F32), _row(ssd_conv_b), _row(ssd_dt_bias, LANES),
        _row(-jnp.exp(ssd_a_log.astype(F32)), LANES), _row(jnp.repeat(ssd_d.astype(F32), SSD_HEAD_DIM)),
        _row(ssd_norm_g))

    w_lr = jnp.zeros((RWKV_LR, 3 * RWKV_WIDTH), F32)
    w_lr = w_lr.at[0:RWKV_DECAY_RANK, 0:RWKV_WIDTH].set(rwkv_w2)
    w_lr = w_lr.at[RWKV_DECAY_RANK:RWKV_DECAY_RANK + RWKV_ICLR_RANK, RWKV_WIDTH:2 * RWKV_WIDTH].set(rwkv_a2)
    w_lr = w_lr.at[RWKV_DECAY_RANK + RWKV_ICLR_RANK:, 2 * RWKV_WIDTH:].set(rwkv_g2)
    y_rwkv = _rwkv_mixer(
        p_rwkv.reshape(bsz, seq, RWKV_IN), _row(rwkv_mu), w_lr, _row(rwkv_w0), _row(rwkv_a0), _row(rwkv_k_k),
        _row(rwkv_k_a), _row(rwkv_r_k), _row(rwkv_ln_g), _row(rwkv_ln_b))

    w_a2 = jnp.zeros((LANES, GLA_QK_WIDTH), F32).at[0:GLA_GATE_RANK].set(gla_w_a2)
    y_gla = _gla_mixer(p_gla.reshape(bsz, seq, GLA_IN), w_a2, _row(gla_b_a), _row(gla_norm_g))

    n_tok = bsz * seq
    return (y_ssd.reshape(n_tok, SSD_WIDTH), y_rwkv.reshape(n_tok, RWKV_WIDTH),
            y_gla.reshape(n_tok, GLA_WIDTH))


def _moe_layer(h1, h1p, eid, gate, counts, layer, w_gate, w_up, w_down, ln_g, ln_b):
    n_tok = h1.shape[0]
    tm = TOKEN_TILE
    n_blocks = n_tok * TOP_K // ROW_BLOCK + N_EXPERTS
    dest = _dispatch_plan(eid, counts)
    blocks = jnp.ceil(counts[0, 0:N_EXPERTS] / ROW_BLOCK).astype(jnp.int32)
    block_end = jnp.cumsum(blocks)
    block_ids = jnp.arange(n_blocks, dtype=jnp.int32)
    block_expert = jnp.minimum(jnp.sum((block_end[None, :] <= block_ids[:, None]).astype(jnp.int32), axis=1),
                               N_EXPERTS - 1)
    n_used = block_end[N_EXPERTS - 1:N_EXPERTS]
    dest3 = dest[:, 0:TOP_K].reshape(n_tok // tm, 1, tm * TOP_K)
    xs = _scatter_rows(dest3, h1p, n_blocks * ROW_BLOCK)
    ys = _expert_mlp(block_expert, n_used, xs, w_gate, w_up, w_down, layer)
    return _combine(dest3, ys, h1, gate, ln_g, ln_b)


def kernel(x, ln_in_g, ln_in_b, w_in, ssd_conv_w, ssd_conv_b, ssd_dt_bias, ssd_a_log, ssd_d, ssd_norm_g, rwkv_mu, rwkv_w0, rwkv_w2, rwkv_a0, rwkv_a2, rwkv_g2, rwkv_k_k, rwkv_k_a, rwkv_r_k, rwkv_ln_g, rwkv_ln_b, gla_w_a2, gla_b_a, gla_norm_g, w_out, ln1_g, ln1_b, moe_w_rg, moe_b_rg, moe_w_re, moe_b_re, moe_w_gate, moe_w_up, moe_w_down, ln2_g, ln2_b):
    bsz, seq, d = x.shape
    n_tok = bsz * seq
    h = x.reshape(n_tok, d)
    h_g = _row(ln_in_g)
    h_b = _row(ln_in_b)
    for i in range(w_in.shape[0]):
        first = i == 0
        y_ssd, y_rwkv, y_gla = _mixer_layer(
            h, h_g, h_b, first, w_in[i], ssd_conv_w[i], ssd_conv_b[i], ssd_dt_bias[i], ssd_a_log[i], ssd_d[i], ssd_norm_g[i],
            rwkv_mu[i], rwkv_w0[i], rwkv_w2[i], rwkv_a0[i], rwkv_a2[i], rwkv_g2[i], rwkv_k_k[i],
            rwkv_k_a[i], rwkv_r_k[i], rwkv_ln_g[i], rwkv_ln_b[i], gla_w_a2[i], gla_b_a[i], gla_norm_g[i],
            bsz, seq)
        w_r = _pad_lanes(jnp.concatenate([moe_w_rg[i], moe_w_re[i]], 1).astype(F32), LANES)
        b_r = _row(jnp.concatenate([moe_b_rg[i], moe_b_re[i]]), LANES)
        h1, h1p, eid, gate, counts = _out_projection(
            y_ssd, y_rwkv, y_gla, h, w_out[i].astype(BF16), _row(ln1_g[i]), _row(ln1_b[i]), w_r, b_r,
            h_g, h_b, first)
        h = _moe_layer(h1, h1p, eid, gate, counts, i, moe_w_gate, moe_w_up, moe_w_down, _row(ln2_g[i]), _row(ln2_b[i]))
    return h.reshape(bsz, seq, d)
```

```python
import functools
import math

import jax
import jax.numpy as jnp
from jax import lax
from jax.experimental import pallas as pl
from jax.experimental.pallas import tpu as pltpu

F32 = jnp.float32
BF16 = jnp.bfloat16

D_MODEL = 1024
DEPTH = 2

SSD_HEAD_DIM = 64
SSD_WIDTH = D_MODEL // 2
SSD_HEADS = SSD_WIDTH // SSD_HEAD_DIM
SSD_GROUPS = 2
SSD_STATE = 64
SSD_CONV = 4
SSD_BC = SSD_GROUPS * SSD_STATE
SSD_XBC = SSD_WIDTH + 2 * SSD_BC

RWKV_HEAD_DIM = 64
RWKV_WIDTH = D_MODEL // 4
RWKV_HEADS = RWKV_WIDTH // RWKV_HEAD_DIM
RWKV_DECAY_RANK = 32
RWKV_ICLR_RANK = 32
RWKV_GATE_RANK = 64
RWKV_LR = RWKV_DECAY_RANK + RWKV_ICLR_RANK + RWKV_GATE_RANK
RWKV_IN = 3 * RWKV_WIDTH + RWKV_LR
RWKV_GN_EPS = 64e-5

GLA_VALUE_DIM = 64
GLA_WIDTH = D_MODEL // 4
GLA_HEADS = GLA_WIDTH // GLA_VALUE_DIM
GLA_KEY_DIM = GLA_VALUE_DIM // 2
GLA_QK_WIDTH = GLA_HEADS * GLA_KEY_DIM
GLA_GATE_RANK = 16
GLA_GATE_TEMP = 16.0

MIX_WIDTH = SSD_WIDTH + RWKV_WIDTH + GLA_WIDTH

N_GROUPS = 4
EXPERTS_PER_GROUP = 8
N_EXPERTS = N_GROUPS * EXPERTS_PER_GROUP
TOP_K = 2
EXPERT_FF = 512

DEEPNORM_ALPHA = (2 * DEPTH) ** 0.25
LN_EPS = 1e-5
RMS_EPS = 1e-6

LANES = 128
SUBLANES = 8
SSD_IN = SSD_WIDTH + SSD_XBC + LANES
GLA_IN = 2 * GLA_QK_WIDTH + 2 * GLA_WIDTH + LANES

TOKEN_TILE = 512
SSD_BLOCK = 128
RWKV_BLOCK = 64
GLA_BLOCK = 64
CHUNKS_PER_STEP = 8
RWKV_CHUNKS_PER_STEP = 2
ROW_BLOCK = 512
DISPATCH_TILE = 1024
VMEM_LIMIT = 48 * 1024 * 1024


def _dot(a, b):
    return jnp.dot(a.astype(BF16), b.astype(BF16), preferred_element_type=F32)


def _split_bf16(x):
    hi = x.astype(BF16)
    return hi, (x - hi.astype(F32)).astype(BF16)


def _dot_rhs2(a_exact, b):
    b_hi, b_lo = _split_bf16(b)
    return (jnp.dot(a_exact, b_hi, preferred_element_type=F32)
            + jnp.dot(a_exact, b_lo, preferred_element_type=F32))


def _dot_lhs2(a, b_exact):
    a_hi, a_lo = _split_bf16(a)
    return (jnp.dot(a_hi, b_exact, preferred_element_type=F32)
            + jnp.dot(a_lo, b_exact, preferred_element_type=F32))


def _cumsum_rows(mask, x):
    m = mask.astype(BF16)
    x_hi, rest = _split_bf16(x)
    x_mid, x_lo = _split_bf16(x - x_hi.astype(F32))
    del rest
    return (jnp.dot(m, x_hi, preferred_element_type=F32)
            + (jnp.dot(m, x_mid, preferred_element_type=F32) + jnp.dot(m, x_lo, preferred_element_type=F32)))


def _dot_x3(a, b):
    a_hi, a_lo = _split_bf16(a)
    b_hi, b_lo = _split_bf16(b)
    return (jnp.dot(a_hi, b_hi, preferred_element_type=F32)
            + (jnp.dot(a_hi, b_lo, preferred_element_type=F32)
               + jnp.dot(a_lo, b_hi, preferred_element_type=F32)))


def _pack_bf16_pair(x):
    m = x.shape[1] // 2
    lo = lax.bitcast_convert_type(x[:, 0:m].astype(BF16).astype(F32), jnp.uint32)
    hi = lax.bitcast_convert_type(x[:, m:2 * m].astype(BF16).astype(F32), jnp.uint32)
    return (lo >> 16) | (hi & jnp.uint32(0xFFFF0000))


def _unpack_bf16_pair(p):
    lo = lax.bitcast_convert_type(p << 16, F32)
    hi = lax.bitcast_convert_type(p & jnp.uint32(0xFFFF0000), F32)
    return lo, hi


def _dot_nt(a, b, precision=None):
    return lax.dot_general(a, b, (((1,), (1,)), ((), ())), precision=precision,
                           preferred_element_type=F32)


def _dot_tn(a, b, precision=None):
    return lax.dot_general(a, b, (((0,), (0,)), ((), ())), precision=precision,
                           preferred_element_type=F32)


def _sigmoid(x):
    return 1.0 / (1.0 + jnp.exp(-x))


def _silu(x):
    return x * _sigmoid(x)


def _softplus(x):
    return jnp.maximum(x, 0.0) + jnp.log1p(jnp.exp(-jnp.abs(x)))


def _iota(shape, dim):
    return lax.broadcasted_iota(jnp.int32, shape, dim)


def _block_indicator(rows, cols, row_seg, col_seg):
    r = _iota((rows, cols), 0) // row_seg
    c = _iota((rows, cols), 1) // col_seg
    return (r == c).astype(F32)


def _layer_norm(x, g, b):
    mu = jnp.mean(x, -1, keepdims=True)
    xc = x - mu
    var = jnp.mean(xc * xc, -1, keepdims=True)
    return xc * lax.rsqrt(var + LN_EPS) * g + b


def _params(*sem):
    return pltpu.CompilerParams(dimension_semantics=sem, vmem_limit_bytes=VMEM_LIMIT)


def _inproj_kernel(h_ref, g_ref, b_ref, w1_ref, w2_ref, w3_ref, o1_ref, o2_ref, o3_ref, *, normalize_input):
    h = h_ref[...]
    if normalize_input:
        h = _layer_norm(h, g_ref[...], b_ref[...])
    hb = h.astype(BF16)
    o1_ref[...] = jnp.dot(hb, w1_ref[...], preferred_element_type=F32)
    o2_ref[...] = jnp.dot(hb, w2_ref[...], preferred_element_type=F32)
    o3_ref[...] = jnp.dot(hb, w3_ref[...], preferred_element_type=F32)


def _in_projection(h, h_g, h_b, w_ssd, w_rwkv, w_gla, normalize_input):
    n_tok, d = h.shape
    tm = TOKEN_TILE
    consts = (h_g, h_b, w_ssd, w_rwkv, w_gla)
    ws = (w_ssd, w_rwkv, w_gla)
    return pl.pallas_call(
        functools.partial(_inproj_kernel, normalize_input=normalize_input), grid=(n_tok // tm,),
        in_specs=[pl.BlockSpec((tm, d), lambda i: (i, 0))]
        + [pl.BlockSpec(a.shape, lambda i: (0, 0)) for a in consts],
        out_specs=[pl.BlockSpec((tm, w.shape[1]), lambda i: (i, 0)) for w in ws],
        out_shape=[jax.ShapeDtypeStruct((n_tok, w.shape[1]), F32) for w in ws],
        compiler_params=_params("parallel"), name="in_projection")(h, *consts)


def _ssd_kernel(p_ref, cw_ref, cb_ref, dtb_ref, a_ref, d_ref, ng_ref, o_ref, xs_ref, st_ref):
    @pl.when(pl.program_id(1) == 0)
    def _():
        xs_ref[0:SUBLANES, :] = jnp.zeros((SUBLANES, SSD_XBC), F32)
        st_ref[...] = jnp.zeros(st_ref.shape, F32)

    for s in range(o_ref.shape[0] // SSD_BLOCK):
        rows = pl.ds(s * SSD_BLOCK, SSD_BLOCK)
        _ssd_chunk(p_ref.at[rows], cw_ref, cb_ref, dtb_ref, a_ref, d_ref, ng_ref, o_ref.at[rows], xs_ref, st_ref)


def _ssd_chunk(p_ref, cw_ref, cb_ref, dtb_ref, a_ref, d_ref, ng_ref, o_ref, xs_ref, st_ref):
    blk = o_ref.shape[0]
    tail = SUBLANES
    z = p_ref[:, 0:SSD_WIDTH]
    xs_ref[tail:tail + blk, :] = p_ref[:, SSD_WIDTH:SSD_WIDTH + SSD_XBC]
    dt_raw = p_ref[:, SSD_WIDTH + SSD_XBC:SSD_IN]

    acc = jnp.broadcast_to(cb_ref[...], (blk, SSD_XBC))
    for i in range(SSD_CONV):
        acc = acc + cw_ref[i:i + 1, :] * xs_ref[pl.ds(tail - (SSD_CONV - 1) + i, blk), :]
    xs_ref[0:tail, :] = xs_ref[blk:blk + tail, :]
    xbc = _silu(acc)
    xh = xbc[:, 0:SSD_WIDTH]
    bm = xbc[:, SSD_WIDTH:SSD_WIDTH + SSD_BC]
    cm = xbc[:, SSD_WIDTH + SSD_BC:SSD_XBC]

    dt = _softplus(dt_raw + dtb_ref[...])
    a_dt = dt * a_ref[...]
    row = _iota((blk, blk), 0)
    col = _iota((blk, blk), 1)
    causal = col <= row
    a_cs = _cumsum_rows(causal, a_dt)
    a_cs_t = a_cs.T
    a_last = a_cs[blk - 1:blk, :]

    expand = _block_indicator(LANES, SSD_WIDTH, 1, SSD_HEAD_DIM).astype(BF16)
    per_head = jnp.concatenate([dt, jnp.exp(a_cs), jnp.exp(a_last - a_cs),
                                jnp.broadcast_to(jnp.exp(a_last), (SUBLANES, LANES))], axis=0)
    per_chan = _dot_lhs2(per_head, expand)
    dt_e = per_chan[0:blk]
    dec_e = per_chan[blk:2 * blk]
    te_e = per_chan[2 * blk:3 * blk]
    cd_e = per_chan[3 * blk:3 * blk + 1]

    x_dt = xh * dt_e
    state = st_ref[...]
    y_off = _dot(cm, state) * dec_e

    lane = _iota((blk, LANES), 1)
    hpg = SSD_HEADS // SSD_GROUPS
    pairs = []
    for j in range(SSD_HEADS // 2):
        grp = (2 * j) // hpg
        in_grp = (lane // SSD_STATE) == grp
        cb = _dot_nt(jnp.where(in_grp, cm, 0.0).astype(BF16), bm.astype(BF16))
        xp = x_dt[:, j * LANES:(j + 1) * LANES]
        ys = []
        for h in (2 * j, 2 * j + 1):
            diff = jnp.broadcast_to(a_cs[:, h:h + 1], (blk, blk)) - a_cs_t[h:h + 1, :]
            seg = jnp.exp(jnp.where(causal, diff, -jnp.inf))
            ys.append(_dot(cb * seg, xp))
        pairs.append(jnp.where(lane < SSD_HEAD_DIM, ys[0], ys[1]))
    y = jnp.concatenate(pairs, axis=1) + y_off + d_ref[...] * xh

    new = _dot_tn(bm.astype(BF16), (x_dt * te_e).astype(BF16))
    keep = _block_indicator(SSD_BC, SSD_WIDTH, SSD_STATE, SSD_WIDTH // SSD_GROUPS)
    st_ref[...] = state * cd_e + keep * new

    y = y * _silu(z)
    gw = SSD_WIDTH // SSD_GROUPS
    for g in range(SSD_GROUPS):
        yg = y[:, g * gw:(g + 1) * gw]
        ms = jnp.mean(yg * yg, -1, keepdims=True)
        o_ref[:, g * gw:(g + 1) * gw] = (yg * lax.rsqrt(ms + RMS_EPS)
                                         * ng_ref[:, g * gw:(g + 1) * gw]).astype(o_ref.dtype)


def _ssd_mixer(p, conv_w, conv_b, dt_bias, a_neg, d_skip, norm_g):
    bsz, seq, _ = p.shape
    blk = SSD_BLOCK * CHUNKS_PER_STEP
    small = lambda a: pl.BlockSpec(a.shape, lambda b, c: (0, 0))
    args = (conv_w, conv_b, dt_bias, a_neg, d_skip, norm_g)
    return pl.pallas_call(
        _ssd_kernel, grid=(bsz, seq // blk),
        in_specs=[pl.BlockSpec((None, blk, SSD_IN), lambda b, c: (b, c, 0))] + [small(a) for a in args],
        out_specs=pl.BlockSpec((None, blk, SSD_WIDTH), lambda b, c: (b, c, 0)),
        out_shape=jax.ShapeDtypeStruct((bsz, seq, SSD_WIDTH), BF16),
        scratch_shapes=[pltpu.VMEM((SSD_BLOCK + 2 * SUBLANES, SSD_XBC), F32),
                        pltpu.VMEM((SSD_BC, SSD_WIDTH), F32)],
        compiler_params=_params("parallel", "arbitrary"), name="ssd_mixer")(p, *args)


def _rwkv_kernel(p_ref, mu_ref, wlr_ref, w0_ref, a0_ref, kk_ref, ka_ref, rk_ref, lng_ref, lnb_ref,
                 o_ref, xs_ref, st_ref):
    @pl.when(pl.program_id(0) == 0)
    def _():
        xs_ref[:, 0:SUBLANES, :] = jnp.zeros((xs_ref.shape[0], SUBLANES, RWKV_IN), F32)
        st_ref[...] = jnp.zeros(st_ref.shape, F32)

    for s in range(o_ref.shape[1] // RWKV_BLOCK):
        _rwkv_chunk(p_ref, mu_ref, wlr_ref, w0_ref, a0_ref, kk_ref, ka_ref, rk_ref, lng_ref, lnb_ref,
                    o_ref, xs_ref, st_ref, s * RWKV_BLOCK)


def _rwkv_chunk(p_ref, mu_ref, wlr_ref, w0_ref, a0_ref, kk_ref, ka_ref, rk_ref, lng_ref, lnb_ref,
                o_ref, xs_ref, st_ref, frame0):
    nb = o_ref.shape[0]
    blk = RWKV_BLOCK
    rows = nb * blk
    tail = SUBLANES
    width = RWKV_WIDTH

    ps, prevs = [], []
    for b in range(nb):
        ps.append(p_ref[b, frame0:frame0 + blk, :])
        xs_ref[b, tail:tail + blk, :] = ps[b]
        prevs.append(xs_ref[b, pl.ds(tail - 1, blk), :])
        xs_ref[b, 0:tail, :] = xs_ref[b, blk:blk + tail, :]
    p = jnp.concatenate(ps, axis=0)
    prev = jnp.concatenate(prevs, axis=0)
    pm = p + (prev - p) * mu_ref[...]
    r = pm[:, 0:width]
    k = pm[:, width:2 * width]
    v = pm[:, 2 * width:3 * width]
    lr = pm[:, 3 * width:RWKV_IN]

    lane = _iota((rows, RWKV_LR), 1)
    lr_act = jnp.where(lane < RWKV_DECAY_RANK, jnp.tanh(lr),
                       jnp.where(lane < RWKV_DECAY_RANK + RWKV_ICLR_RANK, lr, _sigmoid(lr)))
    proj = _dot(lr_act, wlr_ref[...])
    w = -_softplus(-(w0_ref[...] + proj[:, 0:width])) - 0.5
    log_decay = -jnp.exp(w)
    a = _sigmoid(a0_ref[...] + proj[:, width:2 * width])
    g = proj[:, 2 * width:3 * width]

    head_sum = _block_indicator(width, width, RWKV_HEAD_DIM, RWKV_HEAD_DIM)
    head_sum_b = head_sum.astype(BF16)
    kk = k * kk_ref[...]
    kk = kk * lax.rsqrt(_dot(kk * kk, head_sum_b) + 1e-12)
    k2 = k * (1.0 + (a - 1.0) * ka_ref[...])
    alpha = -kk
    beta = kk * a

    row = _iota((rows, rows), 0)
    col = _iota((rows, rows), 1)
    same_row = (row // blk) == (col // blk)
    incl = same_row & (col <= row)
    strict = same_row & (col < row)
    cs = _dot_rhs2(incl.astype(BF16), log_decay)
    e_neg = jnp.exp(-cs)
    a_t = alpha * jnp.exp(cs - log_decay)
    b_t = (beta * e_neg).astype(BF16)
    k_t = (k2 * e_neg).astype(BF16)
    r_t = r * jnp.exp(cs)
    v_b = v.astype(BF16)

    states = [st_ref[b] for b in range(nb)]
    reads = []
    for b in range(nb):
        sl = slice(b * blk, (b + 1) * blk)
        reads.append(_dot(jnp.concatenate([a_t[sl], r_t[sl]], axis=0), states[b]))
    a_s = jnp.concatenate([x[0:blk] for x in reads], axis=0)
    r_s = jnp.concatenate([x[blk:2 * blk] for x in reads], axis=0)

    wlane = _iota((rows, width), 1)
    in_head = [(wlane // RWKV_HEAD_DIM) == h for h in range(RWKV_HEADS)]
    parts = []
    for h in range(RWKV_HEADS):
        parts += [jnp.where(in_head[h], a_t, 0.0), jnp.where(in_head[h], r_t, 0.0)]
    lhs = jnp.concatenate(parts, axis=0).astype(BF16)
    pb_all = _dot_nt(lhs, b_t)
    pk_all = _dot_nt(lhs, k_t)

    heads = range(RWKV_HEADS)
    base = [2 * h * rows for h in heads]
    npow = [jnp.where(strict, pb_all[base[h]:base[h] + rows], 0.0).astype(BF16) for h in heads]
    u = [a_s + _dot(jnp.where(strict, pk_all[base[h]:base[h] + rows], 0.0), v_b) for h in heads]
    steps = int(math.log2(blk))
    for i in range(steps):
        u = [u[h] + _dot(npow[h], u[h]) for h in heads]
        if i + 1 < steps:
            npow = [_dot(npow[h], npow[h]).astype(BF16) for h in heads]
    y = [_dot(jnp.where(incl, pb_all[base[h] + rows:base[h] + 2 * rows], 0.0), u[h])
         + _dot(jnp.where(incl, pk_all[base[h] + rows:base[h] + 2 * rows], 0.0), v_b) for h in heads]
    u_all = u[0]
    y_all = y[0]
    for h in range(1, RWKV_HEADS):
        u_all = jnp.where(in_head[h], u[h], u_all)
        y_all = jnp.where(in_head[h], y[h], y_all)
    y_all = y_all + r_s

    for b in range(nb):
        sl = slice(b * blk, (b + 1) * blk)
        c_last = cs[(b + 1) * blk - 1:(b + 1) * blk, :]
        to_end = jnp.exp(c_last - cs[sl])
        new = (_dot_tn((beta[sl] * to_end).astype(BF16), u_all[sl].astype(BF16))
               + _dot_tn((k2[sl] * to_end).astype(BF16), v_b[sl]))
        chunk_decay = jnp.broadcast_to(jnp.exp(c_last), (SUBLANES, width)).T[:, 0:1]
        st_ref[b] = states[b] * chunk_decay + head_sum * new

    head_mean_b = (head_sum * (1.0 / RWKV_HEAD_DIM)).astype(BF16)
    mean = _dot_lhs2(y_all, head_mean_b)
    yc = y_all - mean
    var = _dot(yc * yc, head_mean_b)
    yn = yc * lax.rsqrt(var + RWKV_GN_EPS) * lng_ref[...] + lnb_ref[...]
    bonus = _dot(r * k2 * rk_ref[...], head_sum_b) * v
    out = ((yn + bonus) * g).astype(o_ref.dtype)
    for b in range(nb):
        o_ref[b, frame0:frame0 + blk, :] = out[b * blk:(b + 1) * blk]


def _rwkv_mixer(p, mu, wlr, w0, a0, k_k, k_a, r_k, ln_g, ln_b):
    bsz, seq, _ = p.shape
    blk = RWKV_BLOCK * RWKV_CHUNKS_PER_STEP
    small = lambda a: pl.BlockSpec(a.shape, lambda c: (0, 0))
    args = (mu, wlr, w0, a0, k_k, k_a, r_k, ln_g, ln_b)
    return pl.pallas_call(
        _rwkv_kernel, grid=(seq // blk,),
        in_specs=[pl.BlockSpec((bsz, blk, RWKV_IN), lambda c: (0, c, 0))] + [small(a) for a in args],
        out_specs=pl.BlockSpec((bsz, blk, RWKV_WIDTH), lambda c: (0, c, 0)),
        out_shape=jax.ShapeDtypeStruct((bsz, seq, RWKV_WIDTH), BF16),
        scratch_shapes=[pltpu.VMEM((bsz, RWKV_BLOCK + 2 * SUBLANES, RWKV_IN), F32),
                        pltpu.VMEM((bsz, RWKV_WIDTH, RWKV_WIDTH), F32)],
        compiler_params=_params("arbitrary"), name="rwkv7_mixer")(p, *args)


def _gla_kernel(p_ref, wa_ref, ba_ref, ng_ref, o_ref, st_ref, z_ref, acc_ref):
    @pl.when(pl.program_id(1) == 0)
    def _():
        st_ref[...] = jnp.zeros(st_ref.shape, F32)

    for s in range(o_ref.shape[0] // GLA_BLOCK):
        rows = pl.ds(s * GLA_BLOCK, GLA_BLOCK)
        _gla_chunk(p_ref.at[rows], wa_ref, ba_ref, ng_ref, o_ref.at[rows], st_ref, z_ref, acc_ref)


def _gla_chunk(p_ref, wa_ref, ba_ref, ng_ref, o_ref, st_ref, z_ref, acc_ref):
    blk = o_ref.shape[0]
    qk = GLA_QK_WIDTH
    vw = GLA_WIDTH
    q = p_ref[:, 0:qk] * (GLA_KEY_DIM ** -0.5)
    k = p_ref[:, qk:2 * qk]
    g = p_ref[:, 2 * qk + vw:2 * qk + 2 * vw]
    a_lr = p_ref[:, 2 * qk + 2 * vw:GLA_IN]
    v_off = 2 * qk

    logit = _dot_x3(a_lr, wa_ref[...]) + ba_ref[...]
    log_a = -_softplus(-logit) / GLA_GATE_TEMP
    row = _iota((blk, blk), 0)
    col = _iota((blk, blk), 1)
    cum = _cumsum_rows(col <= row, log_a)

    state = st_ref[...]
    acc_ref[...] = _dot_nt((q * jnp.exp(cum)).astype(BF16), state.astype(BF16))

    spread = _block_indicator(qk, vw, GLA_KEY_DIM, GLA_VALUE_DIM).astype(BF16)
    rows = _iota((blk, qk), 0)
    group = 2 * SUBLANES
    for jb in range(blk // group):
        r0 = jb * group
        n = blk - r0
        for jj in range(group):
            j = r0 + jj
            decay = jnp.exp(jnp.where(rows[r0:] >= j, cum[r0:] - cum[j:j + 1, :], -jnp.inf))
            z_ref[jj * n:(jj + 1) * n, :] = (q[r0:] * decay * k[j:j + 1, :]).astype(BF16)
        res = jnp.dot(z_ref[0:group * n, :], spread, preferred_element_type=F32)
        part = jnp.zeros((n, vw), F32)
        for jj in range(group):
            j = r0 + jj
            part = part + res[jj * n:(jj + 1) * n] * p_ref[j:j + 1, v_off:v_off + vw]
        acc_ref[r0:blk, :] = acc_ref[r0:blk, :] + part

    v = p_ref[:, v_off:v_off + vw]
    last = cum[blk - 1:blk, :]
    k_dec = k * jnp.exp(last - cum)
    keep = _block_indicator(vw, qk, GLA_VALUE_DIM, GLA_KEY_DIM)
    st_ref[...] = state * jnp.exp(last) + keep * _dot_tn(v.astype(BF16), k_dec.astype(BF16))

    o = acc_ref[...]
    head_mean = (_block_indicator(vw, vw, GLA_VALUE_DIM, GLA_VALUE_DIM) * (1.0 / GLA_VALUE_DIM)).astype(BF16)
    ms = _dot_lhs2(o * o, head_mean)
    o_ref[...] = (o * lax.rsqrt(ms + RMS_EPS) * ng_ref[...] * _silu(g)).astype(o_ref.dtype)


def _gla_mixer(p, w_a2, b_a, norm_g):
    bsz, seq, _ = p.shape
    blk = GLA_BLOCK * CHUNKS_PER_STEP
    small = lambda a: pl.BlockSpec(a.shape, lambda b, c: (0, 0))
    args = (w_a2, b_a, norm_g)
    return pl.pallas_call(
        _gla_kernel, grid=(bsz, seq // blk),
        in_specs=[pl.BlockSpec((None, blk, GLA_IN), lambda b, c: (b, c, 0))] + [small(a) for a in args],
        out_specs=pl.BlockSpec((None, blk, GLA_WIDTH), lambda b, c: (b, c, 0)),
        out_shape=jax.ShapeDtypeStruct((bsz, seq, GLA_WIDTH), BF16),
        scratch_shapes=[pltpu.VMEM((GLA_WIDTH, GLA_QK_WIDTH), F32),
                        pltpu.VMEM((2 * SUBLANES * GLA_BLOCK, GLA_QK_WIDTH), BF16),
                        pltpu.VMEM((GLA_BLOCK, GLA_WIDTH), F32)],
        compiler_params=_params("parallel", "arbitrary"), name="gla_mixer")(p, *args)


def _outproj_kernel(ys_ref, yr_ref, yg_ref, h_ref, w1_ref, w2_ref, w3_ref, g_ref, b_ref, wr_ref, br_ref,
                    hg_ref, hb_ref, h1_ref, h1p_ref, eid_ref, gate_ref, cnt_ref, *, normalize_input):
    mix = (jnp.dot(ys_ref[...], w1_ref[...], preferred_element_type=F32)
           + jnp.dot(yr_ref[...], w2_ref[...], preferred_element_type=F32)
           + jnp.dot(yg_ref[...], w3_ref[...], preferred_element_type=F32))
    h = h_ref[...]
    if normalize_input:
        h = _layer_norm(h, hg_ref[...], hb_ref[...])
    h1 = _layer_norm(DEEPNORM_ALPHA * h + mix, g_ref[...], b_ref[...])
    h1_ref[...] = h1
    h1p_ref[...] = _pack_bf16_pair(h1)

    logits = _dot_x3(h1, wr_ref[...]) + br_ref[...]
    tm = logits.shape[0]
    lane = _iota((tm, LANES), 1)
    lane_f = lane.astype(F32)

    def masked_softmax(mask):
        m = jnp.max(jnp.where(mask, logits, -jnp.inf), -1, keepdims=True)
        e = jnp.where(mask, jnp.exp(logits - m), 0.0)
        return e / jnp.sum(e, -1, keepdims=True)

    def first_argmax(vals, mask):
        m = jnp.max(jnp.where(mask, vals, -jnp.inf), -1, keepdims=True)
        idx = jnp.min(jnp.where(mask & (vals == m), lane_f, float(LANES)), -1, keepdims=True)
        return m, idx.astype(jnp.int32)

    is_group = lane < N_GROUPS
    g_prob = masked_softmax(is_group)
    g_w, g_idx = first_argmax(g_prob, is_group)
    e_lo = N_GROUPS + g_idx * EXPERTS_PER_GROUP
    in_group = (lane >= e_lo) & (lane < e_lo + EXPERTS_PER_GROUP)
    e_prob = masked_softmax(in_group)
    p1, i1 = first_argmax(e_prob, in_group)
    p2, i2 = first_argmax(e_prob, in_group & (lane != i1))
    denom = p1 + p2
    eid_ref[...] = jnp.where(lane == 0, i1 - N_GROUPS, jnp.where(lane == 1, i2 - N_GROUPS, 0))
    gate_ref[...] = jnp.where(lane == 0, g_w * (p1 / denom), jnp.where(lane == 1, g_w * (p2 / denom), 0.0))

    chosen = ((lane == i1 - N_GROUPS) | (lane == i2 - N_GROUPS)).astype(F32)

    @pl.when(pl.program_id(0) == 0)
    def _():
        cnt_ref[...] = jnp.zeros(cnt_ref.shape, F32)

    cnt_ref[...] = cnt_ref[...] + jnp.sum(chosen, 0, keepdims=True)


def _out_projection(y_ssd, y_rwkv, y_gla, h, w_out, ln_g, ln_b, w_r, b_r, h_g, h_b, normalize_input):
    n_tok, d = h.shape
    tm = TOKEN_TILE
    row = lambda n: pl.BlockSpec((tm, n), lambda i: (i, 0))
    full = lambda a: pl.BlockSpec(a.shape, lambda i: (0, 0))
    w1 = w_out[0:SSD_WIDTH]
    w2 = w_out[SSD_WIDTH:SSD_WIDTH + RWKV_WIDTH]
    w3 = w_out[SSD_WIDTH + RWKV_WIDTH:MIX_WIDTH]
    consts = (w1, w2, w3, ln_g, ln_b, w_r, b_r, h_g, h_b)
    return pl.pallas_call(
        functools.partial(_outproj_kernel, normalize_input=normalize_input), grid=(n_tok // tm,),
        in_specs=[row(SSD_WIDTH), row(RWKV_WIDTH), row(GLA_WIDTH), row(d)] + [full(a) for a in consts],
        out_specs=[row(d), row(d // 2), row(LANES), row(LANES), pl.BlockSpec((1, LANES), lambda i: (0, 0))],
        out_shape=[jax.ShapeDtypeStruct((n_tok, d), F32),
                   jax.ShapeDtypeStruct((n_tok, d // 2), jnp.uint32),
                   jax.ShapeDtypeStruct((n_tok, LANES), jnp.int32),
                   jax.ShapeDtypeStruct((n_tok, LANES), F32),
                   jax.ShapeDtypeStruct((1, LANES), F32)],
        compiler_params=_params("arbitrary"), name="out_projection")(y_ssd, y_rwkv, y_gla, h, *consts)


def _dest_kernel(eid_ref, cnt_ref, dest_ref, run_ref, base_ref):
    tm = eid_ref.shape[0]
    e = eid_ref[...]
    lane = _iota((tm, LANES), 1)
    oh0 = (lane == e[:, 0:1]).astype(F32)
    oh1 = (lane == e[:, 1:2]).astype(F32)
    oh = oh0 + oh1

    @pl.when(pl.program_id(0) == 0)
    def _():
        run_ref[...] = jnp.zeros(run_ref.shape, F32)
        blocks = jnp.ceil(cnt_ref[...] * (1.0 / ROW_BLOCK))
        before = (_iota((LANES, LANES), 0) < _iota((LANES, LANES), 1)).astype(BF16)
        start = jnp.dot(jnp.broadcast_to(blocks, (SUBLANES, LANES)).astype(BF16), before,
                        preferred_element_type=F32)[0:1, :]
        base_ref[...] = start * ROW_BLOCK

    earlier = (_iota((tm, tm), 1) < _iota((tm, tm), 0)).astype(BF16)
    rank = jnp.dot(earlier, oh.astype(BF16), preferred_element_type=F32)
    pos = rank + run_ref[...] + base_ref[...]
    d0 = jnp.sum(oh0 * pos, -1, keepdims=True)
    d1 = jnp.sum(oh1 * pos, -1, keepdims=True)
    dest_ref[...] = jnp.where(lane == 0, d0, jnp.where(lane == 1, d1, 0.0)).astype(jnp.int32)
    run_ref[...] = run_ref[...] + jnp.sum(oh, 0, keepdims=True)


def _dispatch_plan(eid, counts):
    n_tok = eid.shape[0]
    tm = DISPATCH_TILE
    return pl.pallas_call(
        _dest_kernel, grid=(n_tok // tm,),
        in_specs=[pl.BlockSpec((tm, LANES), lambda i: (i, 0)), pl.BlockSpec((1, LANES), lambda i: (0, 0))],
        out_specs=pl.BlockSpec((tm, LANES), lambda i: (i, 0)),
        out_shape=jax.ShapeDtypeStruct((n_tok, LANES), jnp.int32),
        scratch_shapes=[pltpu.VMEM((1, LANES), F32), pltpu.VMEM((1, LANES), F32)],
        compiler_params=_params("arbitrary"), name="dispatch_plan")(eid, counts)


def _row_copy(src_ref, src_row, dst_ref, dst_row, sem):
    return pltpu.make_async_copy(src_ref.at[pl.ds(src_row, 1)], dst_ref.at[pl.ds(dst_row, 1)], sem)


def _scatter_kernel(dest_ref, h_ref, xs_in_ref, xs_ref, sem):
    del xs_in_ref
    tm = h_ref.shape[0]

    def start(i, carry):
        base = pl.multiple_of(i * SUBLANES, SUBLANES)
        tile = h_ref.at[pl.ds(base, SUBLANES)]
        for u in range(SUBLANES):
            for k in range(TOP_K):
                _row_copy(tile, u, xs_ref, dest_ref[0, 0, TOP_K * (base + u) + k], sem).start()
        return carry

    lax.fori_loop(0, tm // SUBLANES, start, 0)
    all_rows = xs_ref.at[pl.ds(0, TOP_K * tm)]
    pltpu.make_async_copy(all_rows, all_rows, sem).wait()


def _scatter_rows(dest3, h1, n_rows):
    n_tok, d = h1.shape
    n_tiles = dest3.shape[0]
    xs0 = jnp.zeros((n_rows, d), h1.dtype)
    return pl.pallas_call(
        _scatter_kernel, grid=(n_tiles,),
        in_specs=[pl.BlockSpec((1, 1, dest3.shape[2]), lambda i: (i, 0, 0), memory_space=pltpu.SMEM),
                  pl.BlockSpec((n_tok // n_tiles, d), lambda i: (i, 0)), pl.BlockSpec(memory_space=pl.ANY)],
        out_specs=pl.BlockSpec(memory_space=pl.ANY),
        out_shape=jax.ShapeDtypeStruct((n_rows, d), h1.dtype),
        scratch_shapes=[pltpu.SemaphoreType.DMA(())],
        input_output_aliases={2: 0},
        compiler_params=_params("arbitrary"), name="expert_scatter")(dest3, h1, xs0)


def _expert_kernel(be_ref, used_ref, x_ref, wg_ref, wu_ref, wd_ref, y_ref, wg_b, wu_b, wd_b):
    b = pl.program_id(0)

    @pl.when((b == 0) | (be_ref[b] != be_ref[jnp.maximum(b - 1, 0)]))
    def _():
        wg_b[...] = wg_ref[...].astype(BF16)
        wu_b[...] = wu_ref[...].astype(BF16)
        wd_b[...] = wd_ref[...].astype(BF16)

    @pl.when(b < used_ref[0])
    def _():
        x_lo, x_hi = _unpack_bf16_pair(x_ref[...])
        x = jnp.concatenate([x_lo.astype(BF16), x_hi.astype(BF16)], axis=1)
        gate = jnp.dot(x, wg_b[...], preferred_element_type=F32)
        up = jnp.dot(x, wu_b[...], preferred_element_type=F32)
        mid = (_silu(gate) * up).astype(BF16)
        y_ref[...] = _pack_bf16_pair(jnp.dot(mid, wd_b[...], preferred_element_type=F32))

    @pl.when(b >= used_ref[0])
    def _():
        y_ref[...] = jnp.zeros(y_ref.shape, y_ref.dtype)


def _expert_mlp(block_expert, n_used, xs, w_gate, w_up, w_down, layer):
    n_rows, dp = xs.shape
    n_blocks = n_rows // ROW_BLOCK
    d, ff = w_gate.shape[-2:]
    return pl.pallas_call(
        _expert_kernel,
        grid_spec=pltpu.PrefetchScalarGridSpec(
            num_scalar_prefetch=2, grid=(n_blocks,),
            in_specs=[pl.BlockSpec((ROW_BLOCK, dp), lambda b, be, nu: (b, 0)),
                      pl.BlockSpec((None, None, d, ff), lambda b, be, nu: (layer, be[b], 0, 0)),
                      pl.BlockSpec((None, None, d, ff), lambda b, be, nu: (layer, be[b], 0, 0)),
                      pl.BlockSpec((None, None, ff, d), lambda b, be, nu: (layer, be[b], 0, 0))],
            out_specs=pl.BlockSpec((ROW_BLOCK, dp), lambda b, be, nu: (b, 0)),
            scratch_shapes=[pltpu.VMEM((d, ff), BF16), pltpu.VMEM((d, ff), BF16), pltpu.VMEM((ff, d), BF16)]),
        out_shape=jax.ShapeDtypeStruct((n_rows, dp), jnp.uint32),
        compiler_params=_params("arbitrary"), name="expert_mlp")(
            block_expert, n_used, xs, w_gate, w_up, w_down)


def _combine_kernel(dest_ref, y_ref, h_ref, gate_ref, g_ref, b_ref, o_ref, buf_ref, sem):
    tm = o_ref.shape[0]

    def start(i, carry):
        base = pl.multiple_of(i * SUBLANES, SUBLANES)
        for k in range(TOP_K):
            tile = buf_ref.at[k, pl.ds(base, SUBLANES)]
            for u in range(SUBLANES):
                _row_copy(y_ref, dest_ref[0, 0, TOP_K * (base + u) + k], tile, u, sem).start()
        return carry

    lax.fori_loop(0, tm // SUBLANES, start, 0)
    pltpu.make_async_copy(buf_ref, buf_ref, sem).wait()
    gate = gate_ref[...]
    lo0, hi0 = _unpack_bf16_pair(buf_ref[0])
    lo1, hi1 = _unpack_bf16_pair(buf_ref[1])
    g0 = gate[:, 0:1]
    g1 = gate[:, 1:2]
    ffn = jnp.concatenate([g0 * lo0 + g1 * lo1, g0 * hi0 + g1 * hi1], axis=1)
    o_ref[...] = _layer_norm(DEEPNORM_ALPHA * h_ref[...] + ffn, g_ref[...], b_ref[...])


def _combine(dest3, ys, h1, gate, ln_g, ln_b):
    n_tok, d = h1.shape
    tm = DISPATCH_TILE
    row = lambda n: pl.BlockSpec((tm, n), lambda i: (i, 0))
    vec = pl.BlockSpec((1, d), lambda i: (0, 0))
    return pl.pallas_call(
        _combine_kernel, grid=(n_tok // tm,),
        in_specs=[pl.BlockSpec((1, 1, dest3.shape[2]), lambda i: (i, 0, 0), memory_space=pltpu.SMEM),
                  pl.BlockSpec(memory_space=pl.ANY), row(d), row(LANES), vec, vec],
        out_specs=row(d),
        out_shape=jax.ShapeDtypeStruct((n_tok, d), F32),
        scratch_shapes=[pltpu.VMEM((TOP_K, tm, ys.shape[1]), ys.dtype), pltpu.SemaphoreType.DMA(())],
        compiler_params=_params("arbitrary"), name="expert_combine")(dest3, ys, h1, gate, ln_g, ln_b)


def _pad_lanes(a, n):
    return jnp.pad(a, [(0, 0)] * (a.ndim - 1) + [(0, n - a.shape[-1])])


def _row(a, n=None):
    a = a.reshape(1, -1).astype(F32)
    return a if n is None else _pad_lanes(a, n)


def _mixer_layer(h, h_g, h_b, normalize_input, w_in, ssd_conv_w, ssd_conv_b, ssd_dt_bias, ssd_a_log, ssd_d, ssd_norm_g,
                 rwkv_mu, rwkv_w0, rwkv_w2, rwkv_a0, rwkv_a2, rwkv_g2, rwkv_k_k, rwkv_k_a, rwkv_r_k,
                 rwkv_ln_g, rwkv_ln_b, gla_w_a2, gla_b_a, gla_norm_g, bsz, seq):
    o = 0
    cols = {}
    for name, n in (("z", SSD_WIDTH), ("xbc", SSD_XBC), ("dt", SSD_HEADS), ("rwkv", RWKV_IN),
                    ("q", GLA_QK_WIDTH), ("k", GLA_QK_WIDTH), ("v", GLA_WIDTH), ("g", GLA_WIDTH),
                    ("ga", GLA_GATE_RANK)):
        cols[name] = w_in[:, o:o + n]
        o += n
    w_ssd = jnp.concatenate([cols["z"], cols["xbc"], _pad_lanes(cols["dt"], LANES)], 1).astype(BF16)
    w_rwkv = cols["rwkv"].astype(BF16)
    w_gla = jnp.concatenate([cols["q"], cols["k"], cols["v"], cols["g"], _pad_lanes(cols["ga"], LANES)],
                            1).astype(BF16)
    p_ssd, p_rwkv, p_gla = _in_projection(h, h_g, h_b, w_ssd, w_rwkv, w_gla, normalize_input)

    y_ssd = _ssd_mixer(
        p_ssd.reshape(bsz, seq, SSD_IN), ssd_conv_w.astype(F32), _row(ssd_conv_b), _row(ssd_dt_bias, LANES),
        _row(-jnp.exp(ssd_a_log.astype(F32)), LANES), _row(jnp.repeat(ssd_d.astype(F32), SSD_HEAD_DIM)),
        _row(ssd_norm_g))

    w_lr = jnp.zeros((RWKV_LR, 3 * RWKV_WIDTH), F32)
    w_lr = w_lr.at[0:RWKV_DECAY_RANK, 0:RWKV_WIDTH].set(rwkv_w2)
    w_lr = w_lr.at[RWKV_DECAY_RANK:RWKV_DECAY_RANK + RWKV_ICLR_RANK, RWKV_WIDTH:2 * RWKV_WIDTH].set(rwkv_a2)
    w_lr = w_lr.at[RWKV_DECAY_RANK + RWKV_ICLR_RANK:, 2 * RWKV_WIDTH:].set(rwkv_g2)
    y_rwkv = _rwkv_mixer(
        p_rwkv.reshape(bsz, seq, RWKV_IN), _row(rwkv_mu), w_lr, _row(rwkv_w0), _row(rwkv_a0), _row(rwkv_k_k),
        _row(rwkv_k_a), _row(rwkv_r_k), _row(rwkv_ln_g), _row(rwkv_ln_b))

    w_a2 = jnp.zeros((LANES, GLA_QK_WIDTH), F32).at[0:GLA_GATE_RANK].set(gla_w_a2)
    y_gla = _gla_mixer(p_gla.reshape(bsz, seq, GLA_IN), w_a2, _row(gla_b_a), _row(gla_norm_g))

    n_tok = bsz * seq
    return (y_ssd.reshape(n_tok, SSD_WIDTH), y_rwkv.reshape(n_tok, RWKV_WIDTH),
            y_gla.reshape(n_tok, GLA_WIDTH))


def _moe_layer(h1, h1p, eid, gate, counts, layer, w_gate, w_up, w_down, ln_g, ln_b):
    n_tok = h1.shape[0]
    tm = DISPATCH_TILE
    n_blocks = n_tok * TOP_K // ROW_BLOCK + N_EXPERTS
    dest = _dispatch_plan(eid, counts)
    blocks = jnp.ceil(counts[0, 0:N_EXPERTS] / ROW_BLOCK).astype(jnp.int32)
    block_end = jnp.cumsum(blocks)
    block_ids = jnp.arange(n_blocks, dtype=jnp.int32)
    block_expert = jnp.minimum(jnp.sum((block_end[None, :] <= block_ids[:, None]).astype(jnp.int32), axis=1),
                               N_EXPERTS - 1)
    n_used = block_end[N_EXPERTS - 1:N_EXPERTS]
    dest3 = dest[:, 0:TOP_K].reshape(n_tok // tm, 1, tm * TOP_K)
    xs = _scatter_rows(dest3, h1p, n_blocks * ROW_BLOCK)
    ys = _expert_mlp(block_expert, n_used, xs, w_gate, w_up, w_down, layer)
    return _combine(dest3, ys, h1, gate, ln_g, ln_b)


def kernel(x, ln_in_g, ln_in_b, w_in, ssd_conv_w, ssd_conv_b, ssd_dt_bias, ssd_a_log, ssd_d, ssd_norm_g, rwkv_mu, rwkv_w0, rwkv_w2, rwkv_a0, rwkv_a2, rwkv_g2, rwkv_k_k, rwkv_k_a, rwkv_r_k, rwkv_ln_g, rwkv_ln_b, gla_w_a2, gla_b_a, gla_norm_g, w_out, ln1_g, ln1_b, moe_w_rg, moe_b_rg, moe_w_re, moe_b_re, moe_w_gate, moe_w_up, moe_w_down, ln2_g, ln2_b):
    bsz, seq, d = x.shape
    n_tok = bsz * seq
    h = x.reshape(n_tok, d)
    h_g = _row(ln_in_g)
    h_b = _row(ln_in_b)
    for i in range(w_in.shape[0]):
        first = i == 0
        y_ssd, y_rwkv, y_gla = _mixer_layer(
            h, h_g, h_b, first, w_in[i], ssd_conv_w[i], ssd_conv_b[i], ssd_dt_bias[i], ssd_a_log[i], ssd_d[i], ssd_norm_g[i],
            rwkv_mu[i], rwkv_w0[i], rwkv_w2[i], rwkv_a0[i], rwkv_a2[i], rwkv_g2[i], rwkv_k_k[i],
            rwkv_k_a[i], rwkv_r_k[i], rwkv_ln_g[i], rwkv_ln_b[i], gla_w_a2[i], gla_b_a[i], gla_norm_g[i],
            bsz, seq)
        w_r = _pad_lanes(jnp.concatenate([moe_w_rg[i], moe_w_re[i]], 1).astype(F32), LANES)
        b_r = _row(jnp.concatenate([moe_b_rg[i], moe_b_re[i]]), LANES)
        h1, h1p, eid, gate, counts = _out_projection(
            y_ssd, y_rwkv, y_gla, h, w_out[i].astype(BF16), _row(ln1_g[i]), _row(ln1_b[i]), w_r, b_r,
            h_g, h_b, first)
        h = _moe_layer(h1, h1p, eid, gate, counts, i, moe_w_gate, moe_w_up, moe_w_down, _row(ln2_g[i]), _row(ln2_b[i]))
    return h.reshape(bsz, seq, d)
```

```python
import functools
import math

import jax
import jax.numpy as jnp
from jax import lax
from jax.experimental import pallas as pl
from jax.experimental.pallas import tpu as pltpu

F32 = jnp.float32
BF16 = jnp.bfloat16

D_MODEL = 1024
DEPTH = 2

SSD_HEAD_DIM = 64
SSD_WIDTH = D_MODEL // 2
SSD_HEADS = SSD_WIDTH // SSD_HEAD_DIM
SSD_GROUPS = 2
SSD_STATE = 64
SSD_CONV = 4
SSD_BC = SSD_GROUPS * SSD_STATE
SSD_XBC = SSD_WIDTH + 2 * SSD_BC

RWKV_HEAD_DIM = 64
RWKV_WIDTH = D_MODEL // 4
RWKV_HEADS = RWKV_WIDTH // RWKV_HEAD_DIM
RWKV_DECAY_RANK = 32
RWKV_ICLR_RANK = 32
RWKV_GATE_RANK = 64
RWKV_LR = RWKV_DECAY_RANK + RWKV_ICLR_RANK + RWKV_GATE_RANK
RWKV_IN = 3 * RWKV_WIDTH + RWKV_LR
RWKV_GN_EPS = 64e-5

GLA_VALUE_DIM = 64
GLA_WIDTH = D_MODEL // 4
GLA_HEADS = GLA_WIDTH // GLA_VALUE_DIM
GLA_KEY_DIM = GLA_VALUE_DIM // 2
GLA_QK_WIDTH = GLA_HEADS * GLA_KEY_DIM
GLA_GATE_RANK = 16
GLA_GATE_TEMP = 16.0

MIX_WIDTH = SSD_WIDTH + RWKV_WIDTH + GLA_WIDTH

N_GROUPS = 4
EXPERTS_PER_GROUP = 8
N_EXPERTS = N_GROUPS * EXPERTS_PER_GROUP
TOP_K = 2
EXPERT_FF = 512

DEEPNORM_ALPHA = (2 * DEPTH) ** 0.25
LN_EPS = 1e-5
RMS_EPS = 1e-6

LANES = 128
SUBLANES = 8
SSD_IN = SSD_WIDTH + SSD_XBC + LANES
GLA_IN = 2 * GLA_QK_WIDTH + 2 * GLA_WIDTH + LANES

TOKEN_TILE = 512
SSD_BLOCK = 128
RWKV_BLOCK = 64
GLA_BLOCK = 64
CHUNKS_PER_STEP = 8
RWKV_CHUNKS_PER_STEP = 2
ROW_BLOCK = 512
DISPATCH_TILE = 1024
VMEM_LIMIT = 48 * 1024 * 1024


def _dot(a, b):
    return jnp.dot(a.astype(BF16), b.astype(BF16), preferred_element_type=F32)


def _split_bf16(x):
    hi = x.astype(BF16)
    return hi, (x - hi.astype(F32)).astype(BF16)


def _dot_rhs2(a_exact, b):
    b_hi, b_lo = _split_bf16(b)
    return (jnp.dot(a_exact, b_hi, preferred_element_type=F32)
            + jnp.dot(a_exact, b_lo, preferred_element_type=F32))


def _dot_lhs2(a, b_exact):
    a_hi, a_lo = _split_bf16(a)
    return (jnp.dot(a_hi, b_exact, preferred_element_type=F32)
            + jnp.dot(a_lo, b_exact, preferred_element_type=F32))


def _cumsum_rows(mask, x):
    m = mask.astype(BF16)
    x_hi, rest = _split_bf16(x)
    x_mid, x_lo = _split_bf16(x - x_hi.astype(F32))
    del rest
    return (jnp.dot(m, x_hi, preferred_element_type=F32)
            + (jnp.dot(m, x_mid, preferred_element_type=F32) + jnp.dot(m, x_lo, preferred_element_type=F32)))


def _dot_x3(a, b):
    a_hi, a_lo = _split_bf16(a)
    b_hi, b_lo = _split_bf16(b)
    return (jnp.dot(a_hi, b_hi, preferred_element_type=F32)
            + (jnp.dot(a_hi, b_lo, preferred_element_type=F32)
               + jnp.dot(a_lo, b_hi, preferred_element_type=F32)))


def _pack_bf16_pair(x):
    m = x.shape[1] // 2
    lo = lax.bitcast_convert_type(x[:, 0:m].astype(BF16).astype(F32), jnp.uint32)
    hi = lax.bitcast_convert_type(x[:, m:2 * m].astype(BF16).astype(F32), jnp.uint32)
    return (lo >> 16) | (hi & jnp.uint32(0xFFFF0000))


def _unpack_bf16_pair(p):
    lo = lax.bitcast_convert_type(p << 16, F32)
    hi = lax.bitcast_convert_type(p & jnp.uint32(0xFFFF0000), F32)
    return lo, hi


def _dot_nt(a, b, precision=None):
    return lax.dot_general(a, b, (((1,), (1,)), ((), ())), precision=precision,
                           preferred_element_type=F32)


def _dot_tn(a, b, precision=None):
    return lax.dot_general(a, b, (((0,), (0,)), ((), ())), precision=precision,
                           preferred_element_type=F32)


def _sigmoid(x):
    return 1.0 / (1.0 + jnp.exp(-x))


def _silu(x):
    return x * _sigmoid(x)


def _softplus(x):
    return jnp.maximum(x, 0.0) + jnp.log1p(jnp.exp(-jnp.abs(x)))


def _iota(shape, dim):
    return lax.broadcasted_iota(jnp.int32, shape, dim)


def _block_indicator(rows, cols, row_seg, col_seg):
    r = _iota((rows, cols), 0) // row_seg
    c = _iota((rows, cols), 1) // col_seg
    return (r == c).astype(F32)


def _layer_norm(x, g, b):
    mu = jnp.mean(x, -1, keepdims=True)
    xc = x - mu
    var = jnp.mean(xc * xc, -1, keepdims=True)
    return xc * lax.rsqrt(var + LN_EPS) * g + b


def _params(*sem):
    return pltpu.CompilerParams(dimension_semantics=sem, vmem_limit_bytes=VMEM_LIMIT)


def _inproj_kernel(h_ref, g_ref, b_ref, w1_ref, w2_ref, w3_ref, o1_ref, o2_ref, o3_ref, *, normalize_input):
    h = h_ref[...]
    if normalize_input:
        h = _layer_norm(h, g_ref[...], b_ref[...])
    hb = h.astype(BF16)
    o1_ref[...] = jnp.dot(hb, w1_ref[...], preferred_element_type=F32)
    o2_ref[...] = jnp.dot(hb, w2_ref[...], preferred_element_type=F32)
    o3_ref[...] = jnp.dot(hb, w3_ref[...], preferred_element_type=F32)


def _in_projection(h, h_g, h_b, w_ssd, w_rwkv, w_gla, normalize_input):
    n_tok, d = h.shape
    tm = TOKEN_TILE
    consts = (h_g, h_b, w_ssd, w_rwkv, w_gla)
    ws = (w_ssd, w_rwkv, w_gla)
    return pl.pallas_call(
        functools.partial(_inproj_kernel, normalize_input=normalize_input), grid=(n_tok // tm,),
        in_specs=[pl.BlockSpec((tm, d), lambda i: (i, 0))]
        + [pl.BlockSpec(a.shape, lambda i: (0, 0)) for a in consts],
        out_specs=[pl.BlockSpec((tm, w.shape[1]), lambda i: (i, 0)) for w in ws],
        out_shape=[jax.ShapeDtypeStruct((n_tok, w.shape[1]), F32) for w in ws],
        compiler_params=_params("parallel"), name="in_projection")(h, *consts)


def _ssd_kernel(p_ref, cw_ref, cb_ref, dtb_ref, a_ref, d_ref, ng_ref, o_ref, xs_ref, st_ref):
    @pl.when(pl.program_id(1) == 0)
    def _():
        xs_ref[0:SUBLANES, :] = jnp.zeros((SUBLANES, SSD_XBC), F32)
        st_ref[...] = jnp.zeros(st_ref.shape, F32)

    for s in range(o_ref.shape[0] // SSD_BLOCK):
        rows = pl.ds(s * SSD_BLOCK, SSD_BLOCK)
        _ssd_chunk(p_ref.at[rows], cw_ref, cb_ref, dtb_ref, a_ref, d_ref, ng_ref, o_ref.at[rows], xs_ref, st_ref)


def _ssd_chunk(p_ref, cw_ref, cb_ref, dtb_ref, a_ref, d_ref, ng_ref, o_ref, xs_ref, st_ref):
    blk = o_ref.shape[0]
    tail = SUBLANES
    z = p_ref[:, 0:SSD_WIDTH]
    xs_ref[tail:tail + blk, :] = p_ref[:, SSD_WIDTH:SSD_WIDTH + SSD_XBC]
    dt_raw = p_ref[:, SSD_WIDTH + SSD_XBC:SSD_IN]

    acc = jnp.broadcast_to(cb_ref[...], (blk, SSD_XBC))
    for i in range(SSD_CONV):
        acc = acc + cw_ref[i:i + 1, :] * xs_ref[pl.ds(tail - (SSD_CONV - 1) + i, blk), :]
    xs_ref[0:tail, :] = xs_ref[blk:blk + tail, :]
    xbc = _silu(acc)
    xh = xbc[:, 0:SSD_WIDTH]
    bm = xbc[:, SSD_WIDTH:SSD_WIDTH + SSD_BC]
    cm = xbc[:, SSD_WIDTH + SSD_BC:SSD_XBC]

    dt = _softplus(dt_raw + dtb_ref[...])
    a_dt = dt * a_ref[...]
    row = _iota((blk, blk), 0)
    col = _iota((blk, blk), 1)
    causal = col <= row
    a_cs = _cumsum_rows(causal, a_dt)
    a_cs_t = a_cs.T
    a_last = a_cs[blk - 1:blk, :]

    expand = _block_indicator(LANES, SSD_WIDTH, 1, SSD_HEAD_DIM).astype(BF16)
    per_head = jnp.concatenate([dt, jnp.exp(a_cs), jnp.exp(a_last - a_cs),
                                jnp.broadcast_to(jnp.exp(a_last), (SUBLANES, LANES))], axis=0)
    per_chan = _dot_lhs2(per_head, expand)
    dt_e = per_chan[0:blk]
    dec_e = per_chan[blk:2 * blk]
    te_e = per_chan[2 * blk:3 * blk]
    cd_e = per_chan[3 * blk:3 * blk + 1]

    x_dt = xh * dt_e
    state = st_ref[...]
    y_off = _dot(cm, state) * dec_e

    lane = _iota((blk, LANES), 1)
    hpg = SSD_HEADS // SSD_GROUPS
    pairs = []
    for j in range(SSD_HEADS // 2):
        grp = (2 * j) // hpg
        in_grp = (lane // SSD_STATE) == grp
        cb = _dot_nt(jnp.where(in_grp, cm, 0.0).astype(BF16), bm.astype(BF16))
        xp = x_dt[:, j * LANES:(j + 1) * LANES]
        ys = []
        for h in (2 * j, 2 * j + 1):
            diff = jnp.broadcast_to(a_cs[:, h:h + 1], (blk, blk)) - a_cs_t[h:h + 1, :]
            seg = jnp.exp(jnp.where(causal, diff, -jnp.inf))
            ys.append(_dot(cb * seg, xp))
        pairs.append(jnp.where(lane < SSD_HEAD_DIM, ys[0], ys[1]))
    y = jnp.concatenate(pairs, axis=1) + y_off + d_ref[...] * xh

    new = _dot_tn(bm.astype(BF16), (x_dt * te_e).astype(BF16))
    keep = _block_indicator(SSD_BC, SSD_WIDTH, SSD_STATE, SSD_WIDTH // SSD_GROUPS)
    st_ref[...] = state * cd_e + keep * new

    y = y * _silu(z)
    gw = SSD_WIDTH // SSD_GROUPS
    for g in range(SSD_GROUPS):
        yg = y[:, g * gw:(g + 1) * gw]
        ms = jnp.mean(yg * yg, -1, keepdims=True)
        o_ref[:, g * gw:(g + 1) * gw] = (yg * lax.rsqrt(ms + RMS_EPS)
                                         * ng_ref[:, g * gw:(g + 1) * gw]).astype(o_ref.dtype)


def _ssd_mixer(p, conv_w, conv_b, dt_bias, a_neg, d_skip, norm_g):
    bsz, seq, _ = p.shape
    blk = SSD_BLOCK * CHUNKS_PER_STEP
    small = lambda a: pl.BlockSpec(a.shape, lambda b, c: (0, 0))
    args = (conv_w, conv_b, dt_bias, a_neg, d_skip, norm_g)
    return pl.pallas_call(
        _ssd_kernel, grid=(bsz, seq // blk),
        in_specs=[pl.BlockSpec((None, blk, SSD_IN), lambda b, c: (b, c, 0))] + [small(a) for a in args],
        out_specs=pl.BlockSpec((None, blk, SSD_WIDTH), lambda b, c: (b, c, 0)),
        out_shape=jax.ShapeDtypeStruct((bsz, seq, SSD_WIDTH), BF16),
        scratch_shapes=[pltpu.VMEM((SSD_BLOCK + 2 * SUBLANES, SSD_XBC), F32),
                        pltpu.VMEM((SSD_BC, SSD_WIDTH), F32)],
        compiler_params=_params("parallel", "arbitrary"), name="ssd_mixer")(p, *args)


def _rwkv_kernel(p_ref, mu_ref, wlr_ref, w0_ref, a0_ref, kk_ref, ka_ref, rk_ref, lng_ref, lnb_ref,
                 o_ref, xs_ref, st_ref):
    @pl.when(pl.program_id(0) == 0)
    def _():
        xs_ref[:, 0:SUBLANES, :] = jnp.zeros((xs_ref.shape[0], SUBLANES, RWKV_IN), F32)
        st_ref[...] = jnp.zeros(st_ref.shape, F32)

    for s in range(o_ref.shape[1] // RWKV_BLOCK):
        _rwkv_chunk(p_ref, mu_ref, wlr_ref, w0_ref, a0_ref, kk_ref, ka_ref, rk_ref, lng_ref, lnb_ref,
                    o_ref, xs_ref, st_ref, s * RWKV_BLOCK)


def _rwkv_chunk(p_ref, mu_ref, wlr_ref, w0_ref, a0_ref, kk_ref, ka_ref, rk_ref, lng_ref, lnb_ref,
                o_ref, xs_ref, st_ref, frame0):
    nb = o_ref.shape[0]
    blk = RWKV_BLOCK
    rows = nb * blk
    tail = SUBLANES
    width = RWKV_WIDTH

    ps, prevs = [], []
    for b in range(nb):
        ps.append(p_ref[b, frame0:frame0 + blk, :])
        xs_ref[b, tail:tail + blk, :] = ps[b]
        prevs.append(xs_ref[b, pl.ds(tail - 1, blk), :])
        xs_ref[b, 0:tail, :] = xs_ref[b, blk:blk + tail, :]
    p = jnp.concatenate(ps, axis=0)
    prev = jnp.concatenate(prevs, axis=0)
    pm = p + (prev - p) * mu_ref[...]
    r = pm[:, 0:width]
    k = pm[:, width:2 * width]
    v = pm[:, 2 * width:3 * width]
    lr = pm[:, 3 * width:RWKV_IN]

    lane = _iota((rows, RWKV_LR), 1)
    lr_act = jnp.where(lane < RWKV_DECAY_RANK, jnp.tanh(lr),
                       jnp.where(lane < RWKV_DECAY_RANK + RWKV_ICLR_RANK, lr, _sigmoid(lr)))
    proj = _dot(lr_act, wlr_ref[...])
    w = -_softplus(-(w0_ref[...] + proj[:, 0:width])) - 0.5
    log_decay = -jnp.exp(w)
    a = _sigmoid(a0_ref[...] + proj[:, width:2 * width])
    g = proj[:, 2 * width:3 * width]

    head_sum = _block_indicator(width, width, RWKV_HEAD_DIM, RWKV_HEAD_DIM)
    head_sum_b = head_sum.astype(BF16)
    kk = k * kk_ref[...]
    kk = kk * lax.rsqrt(_dot(kk * kk, head_sum_b) + 1e-12)
    k2 = k * (1.0 + (a - 1.0) * ka_ref[...])
    alpha = -kk
    beta = kk * a

    row = _iota((rows, rows), 0)
    col = _iota((rows, rows), 1)
    same_row = (row // blk) == (col // blk)
    incl = same_row & (col <= row)
    strict = same_row & (col < row)
    cs = _dot_rhs2(incl.astype(BF16), log_decay)
    e_neg = jnp.exp(-cs)
    a_t = alpha * jnp.exp(cs - log_decay)
    b_t = (beta * e_neg).astype(BF16)
    k_t = (k2 * e_neg).astype(BF16)
    r_t = r * jnp.exp(cs)
    v_b = v.astype(BF16)

    states = [st_ref[b] for b in range(nb)]
    reads = []
    for b in range(nb):
        sl = slice(b * blk, (b + 1) * blk)
        reads.append(_dot(jnp.concatenate([a_t[sl], r_t[sl]], axis=0), states[b]))
    a_s = jnp.concatenate([x[0:blk] for x in reads], axis=0)
    r_s = jnp.concatenate([x[blk:2 * blk] for x in reads], axis=0)

    wlane = _iota((rows, width), 1)
    in_head = [(wlane // RWKV_HEAD_DIM) == h for h in range(RWKV_HEADS)]
    parts = []
    for h in range(RWKV_HEADS):
        parts += [jnp.where(in_head[h], a_t, 0.0), jnp.where(in_head[h], r_t, 0.0)]
    lhs = jnp.concatenate(parts, axis=0).astype(BF16)
    pb_all = _dot_nt(lhs, b_t)
    pk_all = _dot_nt(lhs, k_t)

    heads = range(RWKV_HEADS)
    base = [2 * h * rows for h in heads]
    npow = [jnp.where(strict, pb_all[base[h]:base[h] + rows], 0.0).astype(BF16) for h in heads]
    u = [a_s + _dot(jnp.where(strict, pk_all[base[h]:base[h] + rows], 0.0), v_b) for h in heads]
    steps = int(math.log2(blk))
    for i in range(steps):
        u = [u[h] + _dot(npow[h], u[h]) for h in heads]
        if i + 1 < steps:
            npow = [_dot(npow[h], npow[h]).astype(BF16) for h in heads]
    y = [_dot(jnp.where(incl, pb_all[base[h] + rows:base[h] + 2 * rows], 0.0), u[h])
         + _dot(jnp.where(incl, pk_all[base[h] + rows:base[h] + 2 * rows], 0.0), v_b) for h in heads]
    u_all = u[0]
    y_all = y[0]
    for h in range(1, RWKV_HEADS):
        u_all = jnp.where(in_head[h], u[h], u_all)
        y_all = jnp.where(in_head[h], y[h], y_all)
    y_all = y_all + r_s

    for b in range(nb):
        sl = slice(b * blk, (b + 1) * blk)
        c_last = cs[(b + 1) * blk - 1:(b + 1) * blk, :]
        to_end = jnp.exp(c_last - cs[sl])
        new = (_dot_tn((beta[sl] * to_end).astype(BF16), u_all[sl].astype(BF16))
               + _dot_tn((k2[sl] * to_end).astype(BF16), v_b[sl]))
        chunk_decay = jnp.broadcast_to(jnp.exp(c_last), (SUBLANES, width)).T[:, 0:1]
        st_ref[b] = states[b] * chunk_decay + head_sum * new

    head_mean_b = (head_sum * (1.0 / RWKV_HEAD_DIM)).astype(BF16)
    mean = _dot_lhs2(y_all, head_mean_b)
    yc = y_all - mean
    var = _dot(yc * yc, head_mean_b)
    yn = yc * lax.rsqrt(var + RWKV_GN_EPS) * lng_ref[...] + lnb_ref[...]
    bonus = _dot(r * k2 * rk_ref[...], head_sum_b) * v
    out = ((yn + bonus) * g).astype(o_ref.dtype)
    for b in range(nb):
        o_ref[b, frame0:frame0 + blk, :] = out[b * blk:(b + 1) * blk]


def _rwkv_mixer(p, mu, wlr, w0, a0, k_k, k_a, r_k, ln_g, ln_b):
    bsz, seq, _ = p.shape
    blk = RWKV_BLOCK * RWKV_CHUNKS_PER_STEP
    small = lambda a: pl.BlockSpec(a.shape, lambda c: (0, 0))
    args = (mu, wlr, w0, a0, k_k, k_a, r_k, ln_g, ln_b)
    return pl.pallas_call(
        _rwkv_kernel, grid=(seq // blk,),
        in_specs=[pl.BlockSpec((bsz, blk, RWKV_IN), lambda c: (0, c, 0))] + [small(a) for a in args],
        out_specs=pl.BlockSpec((bsz, blk, RWKV_WIDTH), lambda c: (0, c, 0)),
        out_shape=jax.ShapeDtypeStruct((bsz, seq, RWKV_WIDTH), BF16),
        scratch_shapes=[pltpu.VMEM((bsz, RWKV_BLOCK + 2 * SUBLANES, RWKV_IN), F32),
                        pltpu.VMEM((bsz, RWKV_WIDTH, RWKV_WIDTH), F32)],
        compiler_params=_params("arbitrary"), name="rwkv7_mixer")(p, *args)


def _gla_kernel(p_ref, wa_ref, ba_ref, ng_ref, o_ref, st_ref, z_ref, acc_ref):
    @pl.when(pl.program_id(1) == 0)
    def _():
        st_ref[...] = jnp.zeros(st_ref.shape, F32)

    for s in range(o_ref.shape[0] // GLA_BLOCK):
        rows = pl.ds(s * GLA_BLOCK, GLA_BLOCK)
        _gla_chunk(p_ref.at[rows], wa_ref, ba_ref, ng_ref, o_ref.at[rows], st_ref, z_ref, acc_ref)


def _gla_chunk(p_ref, wa_ref, ba_ref, ng_ref, o_ref, st_ref, z_ref, acc_ref):
    blk = o_ref.shape[0]
    qk = GLA_QK_WIDTH
    vw = GLA_WIDTH
    q = p_ref[:, 0:qk] * (GLA_KEY_DIM ** -0.5)
    k = p_ref[:, qk:2 * qk]
    g = p_ref[:, 2 * qk + vw:2 * qk + 2 * vw]
    a_lr = p_ref[:, 2 * qk + 2 * vw:GLA_IN]
    v_off = 2 * qk

    logit = _dot_x3(a_lr, wa_ref[...]) + ba_ref[...]
    log_a = -_softplus(-logit) / GLA_GATE_TEMP
    row = _iota((blk, blk), 0)
    col = _iota((blk, blk), 1)
    cum = _cumsum_rows(col <= row, log_a)

    state = st_ref[...]
    acc_ref[...] = _dot_nt((q * jnp.exp(cum)).astype(BF16), state.astype(BF16))

    spread = _block_indicator(qk, vw, GLA_KEY_DIM, GLA_VALUE_DIM).astype(BF16)
    rows = _iota((blk, qk), 0)
    group = 2 * SUBLANES
    for jb in range(blk // group):
        r0 = jb * group
        n = blk - r0
        for jj in range(group):
            j = r0 + jj
            decay = jnp.exp(jnp.where(rows[r0:] >= j, cum[r0:] - cum[j:j + 1, :], -jnp.inf))
            z_ref[jj * n:(jj + 1) * n, :] = (q[r0:] * decay * k[j:j + 1, :]).astype(BF16)
        res = jnp.dot(z_ref[0:group * n, :], spread, preferred_element_type=F32)
        part = jnp.zeros((n, vw), F32)
        for jj in range(group):
            j = r0 + jj
            part = part + res[jj * n:(jj + 1) * n] * p_ref[j:j + 1, v_off:v_off + vw]
        acc_ref[r0:blk, :] = acc_ref[r0:blk, :] + part

    v = p_ref[:, v_off:v_off + vw]
    last = cum[blk - 1:blk, :]
    k_dec = k * jnp.exp(last - cum)
    keep = _block_indicator(vw, qk, GLA_VALUE_DIM, GLA_KEY_DIM)
    st_ref[...] = state * jnp.exp(last) + keep * _dot_tn(v.astype(BF16), k_dec.astype(BF16))

    o = acc_ref[...]
    head_mean = (_block_indicator(vw, vw, GLA_VALUE_DIM, GLA_VALUE_DIM) * (1.0 / GLA_VALUE_DIM)).astype(BF16)
    ms = _dot_lhs2(o * o, head_mean)
    o_ref[...] = (o * lax.rsqrt(ms + RMS_EPS) * ng_ref[...] * _silu(g)).astype(o_ref.dtype)


def _gla_mixer(p, w_a2, b_a, norm_g):
    bsz, seq, _ = p.shape
    blk = GLA_BLOCK * CHUNKS_PER_STEP
    small = lambda a: pl.BlockSpec(a.shape, lambda b, c: (0, 0))
    args = (w_a2, b_a, norm_g)
    return pl.pallas_call(
        _gla_kernel, grid=(bsz, seq // blk),
        in_specs=[pl.BlockSpec((None, blk, GLA_IN), lambda b, c: (b, c, 0))] + [small(a) for a in args],
        out_specs=pl.BlockSpec((None, blk, GLA_WIDTH), lambda b, c: (b, c, 0)),
        out_shape=jax.ShapeDtypeStruct((bsz, seq, GLA_WIDTH), BF16),
        scratch_shapes=[pltpu.VMEM((GLA_WIDTH, GLA_QK_WIDTH), F32),
                        pltpu.VMEM((2 * SUBLANES * GLA_BLOCK, GLA_QK_WIDTH), BF16),
                        pltpu.VMEM((GLA_BLOCK, GLA_WIDTH), F32)],
        compiler_params=_params("parallel", "arbitrary"), name="gla_mixer")(p, *args)


def _outproj_kernel(ys_ref, yr_ref, yg_ref, h_ref, w1_ref, w2_ref, w3_ref, g_ref, b_ref, wr_ref, br_ref,
                    hg_ref, hb_ref, h1_ref, h1p_ref, eid_ref, gate_ref, cnt_ref, *, normalize_input):
    mix = (jnp.dot(ys_ref[...], w1_ref[...], preferred_element_type=F32)
           + jnp.dot(yr_ref[...], w2_ref[...], preferred_element_type=F32)
           + jnp.dot(yg_ref[...], w3_ref[...], preferred_element_type=F32))
    h = h_ref[...]
    if normalize_input:
        h = _layer_norm(h, hg_ref[...], hb_ref[...])
    h1 = _layer_norm(DEEPNORM_ALPHA * h + mix, g_ref[...], b_ref[...])
    h1_ref[...] = h1
    h1p_ref[...] = _pack_bf16_pair(h1)

    logits = _dot_x3(h1, wr_ref[...]) + br_ref[...]
    tm = logits.shape[0]
    lane = _iota((tm, LANES), 1)
    lane_f = lane.astype(F32)

    def masked_softmax(mask):
        m = jnp.max(jnp.where(mask, logits, -jnp.inf), -1, keepdims=True)
        e = jnp.where(mask, jnp.exp(logits - m), 0.0)
        return e / jnp.sum(e, -1, keepdims=True)

    def first_argmax(vals, mask):
        m = jnp.max(jnp.where(mask, vals, -jnp.inf), -1, keepdims=True)
        idx = jnp.min(jnp.where(mask & (vals == m), lane_f, float(LANES)), -1, keepdims=True)
        return m, idx.astype(jnp.int32)

    is_group = lane < N_GROUPS
    g_prob = masked_softmax(is_group)
    g_w, g_idx = first_argmax(g_prob, is_group)
    e_lo = N_GROUPS + g_idx * EXPERTS_PER_GROUP
    in_group = (lane >= e_lo) & (lane < e_lo + EXPERTS_PER_GROUP)
    e_prob = masked_softmax(in_group)
    p1, i1 = first_argmax(e_prob, in_group)
    p2, i2 = first_argmax(e_prob, in_group & (lane != i1))
    denom = p1 + p2
    eid_ref[...] = jnp.where(lane == 0, i1 - N_GROUPS, jnp.where(lane == 1, i2 - N_GROUPS, 0))
    gate_ref[...] = jnp.where(lane == 0, g_w * (p1 / denom), jnp.where(lane == 1, g_w * (p2 / denom), 0.0))

    chosen = ((lane == i1 - N_GROUPS) | (lane == i2 - N_GROUPS)).astype(F32)

    @pl.when(pl.program_id(0) == 0)
    def _():
        cnt_ref[...] = jnp.zeros(cnt_ref.shape, F32)

    cnt_ref[...] = cnt_ref[...] + jnp.sum(chosen, 0, keepdims=True)


def _out_projection(y_ssd, y_rwkv, y_gla, h, w_out, ln_g, ln_b, w_r, b_r, h_g, h_b, normalize_input):
    n_tok, d = h.shape
    tm = TOKEN_TILE
    row = lambda n: pl.BlockSpec((tm, n), lambda i: (i, 0))
    full = lambda a: pl.BlockSpec(a.shape, lambda i: (0, 0))
    w1 = w_out[0:SSD_WIDTH]
    w2 = w_out[SSD_WIDTH:SSD_WIDTH + RWKV_WIDTH]
    w3 = w_out[SSD_WIDTH + RWKV_WIDTH:MIX_WIDTH]
    consts = (w1, w2, w3, ln_g, ln_b, w_r, b_r, h_g, h_b)
    return pl.pallas_call(
        functools.partial(_outproj_kernel, normalize_input=normalize_input), grid=(n_tok // tm,),
        in_specs=[row(SSD_WIDTH), row(RWKV_WIDTH), row(GLA_WIDTH), row(d)] + [full(a) for a in consts],
        out_specs=[row(d), row(d // 2), row(LANES), row(LANES), pl.BlockSpec((1, LANES), lambda i: (0, 0))],
        out_shape=[jax.ShapeDtypeStruct((n_tok, d), F32),
                   jax.ShapeDtypeStruct((n_tok, d // 2), jnp.uint32),
                   jax.ShapeDtypeStruct((n_tok, LANES), jnp.int32),
                   jax.ShapeDtypeStruct((n_tok, LANES), F32),
                   jax.ShapeDtypeStruct((1, LANES), F32)],
        compiler_params=_params("arbitrary"), name="out_projection")(y_ssd, y_rwkv, y_gla, h, *consts)


def _dest_kernel(eid_ref, cnt_ref, dest_ref, run_ref, base_ref):
    tm = eid_ref.shape[0]
    e = eid_ref[...]
    lane = _iota((tm, LANES), 1)
    oh0 = (lane == e[:, 0:1]).astype(F32)
    oh1 = (lane == e[:, 1:2]).astype(F32)
    oh = oh0 + oh1

    @pl.when(pl.program_id(0) == 0)
    def _():
        run_ref[...] = jnp.zeros(run_ref.shape, F32)
        blocks = jnp.ceil(cnt_ref[...] * (1.0 / ROW_BLOCK))
        before = (_iota((LANES, LANES), 0) < _iota((LANES, LANES), 1)).astype(BF16)
        start = jnp.dot(jnp.broadcast_to(blocks, (SUBLANES, LANES)).astype(BF16), before,
                        preferred_element_type=F32)[0:1, :]
        base_ref[...] = start * ROW_BLOCK

    earlier = (_iota((tm, tm), 1) < _iota((tm, tm), 0)).astype(BF16)
    rank = jnp.dot(earlier, oh.astype(BF16), preferred_element_type=F32)
    pos = rank + run_ref[...] + base_ref[...]
    d0 = jnp.sum(oh0 * pos, -1, keepdims=True)
    d1 = jnp.sum(oh1 * pos, -1, keepdims=True)
    dest_ref[...] = jnp.where(lane == 0, d0, jnp.where(lane == 1, d1, 0.0)).astype(jnp.int32)
    run_ref[...] = run_ref[...] + jnp.sum(oh, 0, keepdims=True)


def _dispatch_plan(eid, counts):
    n_tok = eid.shape[0]
    tm = DISPATCH_TILE
    return pl.pallas_call(
        _dest_kernel, grid=(n_tok // tm,),
        in_specs=[pl.BlockSpec((tm, LANES), lambda i: (i, 0)), pl.BlockSpec((1, LANES), lambda i: (0, 0))],
        out_specs=pl.BlockSpec((tm, LANES), lambda i: (i, 0)),
        out_shape=jax.ShapeDtypeStruct((n_tok, LANES), jnp.int32),
        scratch_shapes=[pltpu.VMEM((1, LANES), F32), pltpu.VMEM((1, LANES), F32)],
        compiler_params=_params("arbitrary"), name="dispatch_plan")(eid, counts)


def _row_copy(src_ref, src_row, dst_ref, dst_row, sem):
    return pltpu.make_async_copy(src_ref.at[pl.ds(src_row, 1)], dst_ref.at[pl.ds(dst_row, 1)], sem)


def _scatter_kernel(dest_ref, h_ref, xs_in_ref, xs_ref, sem):
    del xs_in_ref
    tm = h_ref.shape[0]

    def start(i, carry):
        base = pl.multiple_of(i * SUBLANES, SUBLANES)
        tile = h_ref.at[pl.ds(base, SUBLANES)]
        for u in range(SUBLANES):
            for k in range(TOP_K):
                _row_copy(tile, u, xs_ref, dest_ref[0, 0, TOP_K * (base + u) + k], sem).start(
                    priority=(TOP_K * u + k) % 2)
        return carry

    lax.fori_loop(0, tm // SUBLANES, start, 0)
    all_rows = xs_ref.at[pl.ds(0, TOP_K * tm)]
    pltpu.make_async_copy(all_rows, all_rows, sem).wait()


def _scatter_rows(dest3, h1, n_rows):
    n_tok, d = h1.shape
    n_tiles = dest3.shape[0]
    xs0 = jnp.zeros((n_rows, d), h1.dtype)
    return pl.pallas_call(
        _scatter_kernel, grid=(n_tiles,),
        in_specs=[pl.BlockSpec((1, 1, dest3.shape[2]), lambda i: (i, 0, 0), memory_space=pltpu.SMEM),
                  pl.BlockSpec((n_tok // n_tiles, d), lambda i: (i, 0)), pl.BlockSpec(memory_space=pl.ANY)],
        out_specs=pl.BlockSpec(memory_space=pl.ANY),
        out_shape=jax.ShapeDtypeStruct((n_rows, d), h1.dtype),
        scratch_shapes=[pltpu.SemaphoreType.DMA(())],
        input_output_aliases={2: 0},
        compiler_params=_params("arbitrary"), name="expert_scatter")(dest3, h1, xs0)


def _expert_kernel(be_ref, used_ref, x_ref, wg_ref, wu_ref, wd_ref, y_ref, wg_b, wu_b, wd_b):
    b = pl.program_id(0)

    @pl.when((b == 0) | (be_ref[b] != be_ref[jnp.maximum(b - 1, 0)]))
    def _():
        wg_b[...] = wg_ref[...].astype(BF16)
        wu_b[...] = wu_ref[...].astype(BF16)
        wd_b[...] = wd_ref[...].astype(BF16)

    @pl.when(b < used_ref[0])
    def _():
        x_lo, x_hi = _unpack_bf16_pair(x_ref[...])
        x = jnp.concatenate([x_lo.astype(BF16), x_hi.astype(BF16)], axis=1)
        gate = jnp.dot(x, wg_b[...], preferred_element_type=F32)
        up = jnp.dot(x, wu_b[...], preferred_element_type=F32)
        mid = (_silu(gate) * up).astype(BF16)
        y_ref[...] = _pack_bf16_pair(jnp.dot(mid, wd_b[...], preferred_element_type=F32))

    @pl.when(b >= used_ref[0])
    def _():
        y_ref[...] = jnp.zeros(y_ref.shape, y_ref.dtype)


def _expert_mlp(block_expert, n_used, xs, w_gate, w_up, w_down, layer):
    n_rows, dp = xs.shape
    n_blocks = n_rows // ROW_BLOCK
    d, ff = w_gate.shape[-2:]
    return pl.pallas_call(
        _expert_kernel,
        grid_spec=pltpu.PrefetchScalarGridSpec(
            num_scalar_prefetch=2, grid=(n_blocks,),
            in_specs=[pl.BlockSpec((ROW_BLOCK, dp), lambda b, be, nu: (b, 0)),
                      pl.BlockSpec((None, None, d, ff), lambda b, be, nu: (layer, be[b], 0, 0)),
                      pl.BlockSpec((None, None, d, ff), lambda b, be, nu: (layer, be[b], 0, 0)),
                      pl.BlockSpec((None, None, ff, d), lambda b, be, nu: (layer, be[b], 0, 0))],
            out_specs=pl.BlockSpec((ROW_BLOCK, dp), lambda b, be, nu: (b, 0)),
            scratch_shapes=[pltpu.VMEM((d, ff), BF16), pltpu.VMEM((d, ff), BF16), pltpu.VMEM((ff, d), BF16)]),
        out_shape=jax.ShapeDtypeStruct((n_rows, dp), jnp.uint32),
        compiler_params=_params("arbitrary"), name="expert_mlp")(
            block_expert, n_used, xs, w_gate, w_up, w_down)


def _combine_kernel(dest_ref, y_ref, h_ref, gate_ref, g_ref, b_ref, o_ref, buf_ref, sem):
    tm = o_ref.shape[0]

    def start(i, carry):
        base = pl.multiple_of(i * SUBLANES, SUBLANES)
        for k in range(TOP_K):
            tile = buf_ref.at[k, pl.ds(base, SUBLANES)]
            for u in range(SUBLANES):
                _row_copy(y_ref, dest_ref[0, 0, TOP_K * (base + u) + k], tile, u, sem).start(priority=u % 2)
        return carry

    lax.fori_loop(0, tm // SUBLANES, start, 0)
    pltpu.make_async_copy(buf_ref, buf_ref, sem).wait()
    gate = gate_ref[...]
    lo0, hi0 = _unpack_bf16_pair(buf_ref[0])
    lo1, hi1 = _unpack_bf16_pair(buf_ref[1])
    g0 = gate[:, 0:1]
    g1 = gate[:, 1:2]
    ffn = jnp.concatenate([g0 * lo0 + g1 * lo1, g0 * hi0 + g1 * hi1], axis=1)
    o_ref[...] = _layer_norm(DEEPNORM_ALPHA * h_ref[...] + ffn, g_ref[...], b_ref[...])


def _combine(dest3, ys, h1, gate, ln_g, ln_b):
    n_tok, d = h1.shape
    tm = DISPATCH_TILE
    row = lambda n: pl.BlockSpec((tm, n), lambda i: (i, 0))
    vec = pl.BlockSpec((1, d), lambda i: (0, 0))
    return pl.pallas_call(
        _combine_kernel, grid=(n_tok // tm,),
        in_specs=[pl.BlockSpec((1, 1, dest3.shape[2]), lambda i: (i, 0, 0), memory_space=pltpu.SMEM),
                  pl.BlockSpec(memory_space=pl.ANY), row(d), row(LANES), vec, vec],
        out_specs=row(d),
        out_shape=jax.ShapeDtypeStruct((n_tok, d), F32),
        scratch_shapes=[pltpu.VMEM((TOP_K, tm, ys.shape[1]), ys.dtype), pltpu.SemaphoreType.DMA(())],
        compiler_params=_params("arbitrary"), name="expert_combine")(dest3, ys, h1, gate, ln_g, ln_b)


def _pad_lanes(a, n):
    return jnp.pad(a, [(0, 0)] * (a.ndim - 1) + [(0, n - a.shape[-1])])


def _row(a, n=None):
    a = a.reshape(1, -1).astype(F32)
    return a if n is None else _pad_lanes(a, n)


def _mixer_layer(h, h_g, h_b, normalize_input, w_in, ssd_conv_w, ssd_conv_b, ssd_dt_bias, ssd_a_log, ssd_d, ssd_norm_g,
                 rwkv_mu, rwkv_w0, rwkv_w2, rwkv_a0, rwkv_a2, rwkv_g2, rwkv_k_k, rwkv_k_a, rwkv_r_k,
                 rwkv_ln_g, rwkv_ln_b, gla_w_a2, gla_b_a, gla_norm_g, bsz, seq):
    o = 0
    cols = {}
    for name, n in (("z", SSD_WIDTH), ("xbc", SSD_XBC), ("dt", SSD_HEADS), ("rwkv", RWKV_IN),
                    ("q", GLA_QK_WIDTH), ("k", GLA_QK_WIDTH), ("v", GLA_WIDTH), ("g", GLA_WIDTH),
                    ("ga", GLA_GATE_RANK)):
        cols[name] = w_in[:, o:o + n]
        o += n
    w_ssd = jnp.concatenate([cols["z"], cols["xbc"], _pad_lanes(cols["dt"], LANES)], 1).astype(BF16)
    w_rwkv = cols["rwkv"].astype(BF16)
    w_gla = jnp.concatenate([cols["q"], cols["k"], cols["v"], cols["g"], _pad_lanes(cols["ga"], LANES)],
                            1).astype(BF16)
    p_ssd, p_rwkv, p_gla = _in_projection(h, h_g, h_b, w_ssd, w_rwkv, w_gla, normalize_input)

    y_ssd = _ssd_mixer(
        p_ssd.reshape(bsz, seq, SSD_IN), ssd_conv_w.astype(F32), _row(ssd_conv_b), _row(ssd_dt_bias, LANES),
        _row(-jnp.exp(ssd_a_log.astype(F32)), LANES), _row(jnp.repeat(ssd_d.astype(F32), SSD_HEAD_DIM)),
        _row(ssd_norm_g))

    w_lr = jnp.zeros((RWKV_LR, 3 * RWKV_WIDTH), F32)
    w_lr = w_lr.at[0:RWKV_DECAY_RANK, 0:RWKV_WIDTH].set(rwkv_w2)
    w_lr = w_lr.at[RWKV_DECAY_RANK:RWKV_DECAY_RANK + RWKV_ICLR_RANK, RWKV_WIDTH:2 * RWKV_WIDTH].set(rwkv_a2)
    w_lr = w_lr.at[RWKV_DECAY_RANK + RWKV_ICLR_RANK:, 2 * RWKV_WIDTH:].set(rwkv_g2)
    y_rwkv = _rwkv_mixer(
        p_rwkv.reshape(bsz, seq, RWKV_IN), _row(rwkv_mu), w_lr, _row(rwkv_w0), _row(rwkv_a0), _row(rwkv_k_k),
        _row(rwkv_k_a), _row(rwkv_r_k), _row(rwkv_ln_g), _row(rwkv_ln_b))

    w_a2 = jnp.zeros((LANES, GLA_QK_WIDTH), F32).at[0:GLA_GATE_RANK].set(gla_w_a2)
    y_gla = _gla_mixer(p_gla.reshape(bsz, seq, GLA_IN), w_a2, _row(gla_b_a), _row(gla_norm_g))

    n_tok = bsz * seq
    return (y_ssd.reshape(n_tok, SSD_WIDTH), y_rwkv.reshape(n_tok, RWKV_WIDTH),
            y_gla.reshape(n_tok, GLA_WIDTH))


def _moe_layer(h1, h1p, eid, gate, counts, layer, w_gate, w_up, w_down, ln_g, ln_b):
    n_tok = h1.shape[0]
    tm = DISPATCH_TILE
    n_blocks = n_tok * TOP_K // ROW_BLOCK + N_EXPERTS
    dest = _dispatch_plan(eid, counts)
    blocks = jnp.ceil(counts[0, 0:N_EXPERTS] / ROW_BLOCK).astype(jnp.int32)
    block_end = jnp.cumsum(blocks)
    block_ids = jnp.arange(n_blocks, dtype=jnp.int32)
    block_expert = jnp.minimum(jnp.sum((block_end[None, :] <= block_ids[:, None]).astype(jnp.int32), axis=1),
                               N_EXPERTS - 1)
    n_used = block_end[N_EXPERTS - 1:N_EXPERTS]
    dest3 = dest[:, 0:TOP_K].reshape(n_tok // tm, 1, tm * TOP_K)
    xs = _scatter_rows(dest3, h1p, n_blocks * ROW_BLOCK)
    ys = _expert_mlp(block_expert, n_used, xs, w_gate, w_up, w_down, layer)
    return _combine(dest3, ys, h1, gate, ln_g, ln_b)


def kernel(x, ln_in_g, ln_in_b, w_in, ssd_conv_w, ssd_conv_b, ssd_dt_bias, ssd_a_log, ssd_d, ssd_norm_g, rwkv_mu, rwkv_w0, rwkv_w2, rwkv_a0, rwkv_a2, rwkv_g2, rwkv_k_k, rwkv_k_a, rwkv_r_k, rwkv_ln_g, rwkv_ln_b, gla_w_a2, gla_b_a, gla_norm_g, w_out, ln1_g, ln1_b, moe_w_rg, moe_b_rg, moe_w_re, moe_b_re, moe_w_gate, moe_w_up, moe_w_down, ln2_g, ln2_b):
    bsz, seq, d = x.shape
    n_tok = bsz * seq
    h = x.reshape(n_tok, d)
    h_g = _row(ln_in_g)
    h_b = _row(ln_in_b)
    for i in range(w_in.shape[0]):
        first = i == 0
        y_ssd, y_rwkv, y_gla = _mixer_layer(
            h, h_g, h_b, first, w_in[i], ssd_conv_w[i], ssd_conv_b[i], ssd_dt_bias[i], ssd_a_log[i], ssd_d[i], ssd_norm_g[i],
            rwkv_mu[i], rwkv_w0[i], rwkv_w2[i], rwkv_a0[i], rwkv_a2[i], rwkv_g2[i], rwkv_k_k[i],
            rwkv_k_a[i], rwkv_r_k[i], rwkv_ln_g[i], rwkv_ln_b[i], gla_w_a2[i], gla_b_a[i], gla_norm_g[i],
            bsz, seq)
        w_r = _pad_lanes(jnp.concatenate([moe_w_rg[i], moe_w_re[i]], 1).astype(F32), LANES)
        b_r = _row(jnp.concatenate([moe_b_rg[i], moe_b_re[i]]), LANES)
        h1, h1p, eid, gate, counts = _out_projection(
            y_ssd, y_rwkv, y_gla, h, w_out[i].astype(BF16), _row(ln1_g[i]), _row(ln1_b[i]), w_r, b_r,
            h_g, h_b, first)
        h = _moe_layer(h1, h1p, eid, gate, counts, i, moe_w_gate, moe_w_up, moe_w_down, _row(ln2_g[i]), _row(ln2_b[i]))
    return h.reshape(bsz, seq, d)
```
